```python
import math
import jax, jax.numpy as jnp
from jax import lax
import numpy as np

D_MODEL = 1024
BATCH = 8
SEQ = 8192
DEPTH = 4
DEC_BATCH = 1
DEC_SEQ = 16384
PAST_LEN = 128

RMS_EPS = 1e-6
N_FOURIER_GROUPS = 4
FOURIER_WIDTH = D_MODEL // 2
FOURIER_GROUP = FOURIER_WIDTH // N_FOURIER_GROUPS
HEAD_DIM = 64
N_Q_HEADS = (D_MODEL // 2) // HEAD_DIM
N_KV_HEADS = N_Q_HEADS // 4
Q_PER_KV = N_Q_HEADS // N_KV_HEADS
ATTN_WIDTH = N_Q_HEADS * HEAD_DIM
KV_WIDTH = N_KV_HEADS * HEAD_DIM
WINDOW = 128
ATTN_BLOCK = 128
ROPE_THETA = 10000.0
EVEN_IN = FOURIER_WIDTH + ATTN_WIDTH + 2 * KV_WIDTH
EVEN_MIX = FOURIER_WIDTH + ATTN_WIDTH
D_INNER = 2 * D_MODEL
SSM_HEAD_DIM = 64
SSM_HEADS = D_INNER // SSM_HEAD_DIM
SSM_STATE = 128
SSM_GROUPS = 4
HEADS_PER_GROUP = SSM_HEADS // SSM_GROUPS
CONV_K = 5
CHUNK = 128
GN = SSM_GROUPS * SSM_STATE
CONV_DIM = D_INNER + 2 * GN
ODD_IN = D_INNER + CONV_DIM + 2 * SSM_HEADS
N_EXPERT_GROUPS = 4
EXPERTS_PER_GROUP = 8
N_EXPERTS = N_EXPERT_GROUPS * EXPERTS_PER_GROUP
TOP_K_INNER = 2
D_EXPERT = D_MODEL // 2
MOE_BLOCK = 256
PLE_DIM = 256
N_EVEN = (DEPTH + 1) // 2
N_ODD = DEPTH // 2

kernel_name = "fourier_swa_ssd_hmoe_encoder"


def rms_norm(x, g):
    xf = x.astype(jnp.float32)
    y = xf * lax.rsqrt(jnp.mean(xf * xf, axis=-1, keepdims=True) + RMS_EPS)
    return (y * g.astype(jnp.float32)).astype(x.dtype)


def rope_tables(seq):
    inv = 1.0 / (ROPE_THETA ** (jnp.arange(0, HEAD_DIM, 2, dtype=jnp.float32) / HEAD_DIM))
    ang = jnp.arange(seq, dtype=jnp.float32)[:, None] * inv[None, :]
    return jnp.cos(ang), jnp.sin(ang)


def apply_rope(t, cos, sin):
    tf = t.astype(jnp.float32)
    t1, t2 = tf[..., :HEAD_DIM // 2], tf[..., HEAD_DIM // 2:]
    c, s = cos[None, :, None, :], sin[None, :, None, :]
    return jnp.concatenate([t1 * c - t2 * s, t2 * c + t1 * s], axis=-1).astype(t.dtype)


def fourier_mix(a):
    bsz, seq, _ = a.shape
    ag = a.reshape(bsz, seq, N_FOURIER_GROUPS, FOURIER_GROUP).astype(jnp.float32)
    f = jnp.fft.fftn(ag, axes=(1, 3), norm="ortho").real
    return f.reshape(bsz, seq, FOURIER_WIDTH).astype(a.dtype)


def banded_sink_attention(q, k, v, sink):
    bsz, seq = q.shape[:2]
    nb = seq // ATTN_BLOCK
    qb = q.reshape(bsz, nb, ATTN_BLOCK, N_KV_HEADS, Q_PER_KV, HEAD_DIM)

    def neighbours(t):
        tb = t.reshape(bsz, nb, ATTN_BLOCK, N_KV_HEADS, HEAD_DIM)
        tp = jnp.pad(tb, ((0, 0), (1, 1), (0, 0), (0, 0), (0, 0)))
        return jnp.concatenate([tp[:, :-2], tp[:, 1:-1], tp[:, 2:]], axis=2)

    kn, vn = neighbours(k), neighbours(v)
    scores = jnp.einsum("bnqhgd,bnkhd->bnhgqk", qb, kn).astype(jnp.float32) * (HEAD_DIM ** -0.5)
    qpos = jnp.arange(ATTN_BLOCK)[:, None] + ATTN_BLOCK
    kpos = jnp.arange(3 * ATTN_BLOCK)[None, :]
    band = jnp.abs(kpos - qpos) <= WINDOW
    kglob = (jnp.arange(nb)[:, None] - 1) * ATTN_BLOCK + kpos
    inside = (kglob >= 0) & (kglob < seq)
    mask = band[None] & inside[:, None, :]
    scores = jnp.where(mask[None, :, None, None], scores, -jnp.inf)
    sk = sink.astype(jnp.float32).reshape(N_KV_HEADS, Q_PER_KV)[None, None, :, :, None, None]
    m = jnp.maximum(jnp.max(scores, axis=-1, keepdims=True), sk)
    e = jnp.exp(scores - m)
    probs = e / (jnp.sum(e, axis=-1, keepdims=True) + jnp.exp(sk - m))
    o = jnp.einsum("bnhgqk,bnkhd->bnqhgd", probs.astype(v.dtype), vn)
    return o.reshape(bsz, seq, ATTN_WIDTH)


def fourier_window_mixer(h, w_in, q_gain, k_gain, sink, w_out, cos, sin):
    bsz, seq, _ = h.shape
    u = h @ w_in
    o1 = FOURIER_WIDTH
    o2 = o1 + ATTN_WIDTH
    o3 = o2 + KV_WIDTH
    a = u[..., :o1]
    q = u[..., o1:o2].reshape(bsz, seq, N_Q_HEADS, HEAD_DIM)
    k = u[..., o2:o3].reshape(bsz, seq, N_KV_HEADS, HEAD_DIM)
    v = u[..., o3:].reshape(bsz, seq, N_KV_HEADS, HEAD_DIM)
    q = apply_rope(rms_norm(q, q_gain), cos, sin)
    k = apply_rope(rms_norm(k, k_gain), cos, sin)
    mixed = jnp.concatenate([fourier_mix(a), banded_sink_attention(q, k, v, sink)], axis=-1)
    return mixed @ w_out


def centred_depthwise_conv(x, w, b):
    y = lax.conv_general_dilated(x, w[:, None, :].astype(x.dtype), window_strides=(1,),
                                 padding=[(CONV_K // 2, CONV_K // 2)],
                                 dimension_numbers=("NWC", "WIO", "NWC"),
                                 feature_group_count=CONV_DIM)
    return y + b.astype(x.dtype)


def ssd_chunked(x, dt, a, bm, cm):
    bsz, seq = x.shape[:2]
    nc = seq // CHUNK
    xdt = (x.astype(jnp.float32) * dt[..., None]).reshape(
        bsz, nc, CHUNK, SSM_GROUPS, HEADS_PER_GROUP, SSM_HEAD_DIM)
    la = jnp.moveaxis((dt * a).reshape(bsz, nc, CHUNK, SSM_GROUPS, HEADS_PER_GROUP), 2, -1)
    acs = jnp.cumsum(la, axis=-1)
    bc = bm.astype(jnp.float32).reshape(bsz, nc, CHUNK, SSM_GROUPS, SSM_STATE)
    cc = cm.astype(jnp.float32).reshape(bsz, nc, CHUNK, SSM_GROUPS, SSM_STATE)
    lower = jnp.tril(jnp.ones((CHUNK, CHUNK), dtype=bool))
    seg = acs[..., :, None] - acs[..., None, :]
    lmat = jnp.exp(jnp.where(lower, seg, -jnp.inf))
    cb = jnp.einsum("bclgn,bcsgn->bcgls", cc, bc)
    y_diag = jnp.einsum("bcgls,bcgjls,bcsgjp->bclgjp", cb, lmat, xdt)
    decay_to_end = jnp.exp(acs[..., -1:] - acs)
    states = jnp.einsum("bclgn,bcgjl,bclgjp->bcgjpn", bc, decay_to_end, xdt)
    chunk_decay = jnp.exp(acs[..., -1])

    def step(hs, inp):
        st, dec = inp
        return hs * dec[..., None, None] + st, hs

    h0 = jnp.zeros((bsz, SSM_GROUPS, HEADS_PER_GROUP, SSM_HEAD_DIM, SSM_STATE), jnp.float32)
    _, h_prev = lax.scan(step, h0, (jnp.moveaxis(states, 1, 0), jnp.moveaxis(chunk_decay, 1, 0)))
    h_prev = jnp.moveaxis(h_prev, 0, 1)
    y_off = jnp.einsum("bclgn,bcgjpn,bcgjl->bclgjp", cc, h_prev, jnp.exp(acs))
    return (y_diag + y_off).reshape(bsz, seq, SSM_HEADS, SSM_HEAD_DIM)


def bidir_ssd_mixer(h, w_in, conv_w, conv_b, dt_bias, a_log, d_skip, norm_g, w_out):
    bsz, seq, _ = h.shape
    u = h @ w_in
    z = u[..., :D_INNER]
    xbc = jax.nn.silu(centred_depthwise_conv(u[..., D_INNER:D_INNER + CONV_DIM], conv_w, conv_b))
    dt_raw = u[..., D_INNER + CONV_DIM:].reshape(bsz, seq, 2, SSM_HEADS)
    xs = xbc[..., :D_INNER].reshape(bsz, seq, SSM_HEADS, SSM_HEAD_DIM)
    bm = xbc[..., D_INNER:D_INNER + GN].reshape(bsz, seq, SSM_GROUPS, SSM_STATE)
    cm = xbc[..., D_INNER + GN:].reshape(bsz, seq, SSM_GROUPS, SSM_STATE)
    dt = jax.nn.softplus(dt_raw.astype(jnp.float32) + dt_bias.astype(jnp.float32))
    a = -jnp.exp(a_log.astype(jnp.float32))
    y_f = ssd_chunked(xs, dt[:, :, 0], a[0], bm, cm)
    y_b = jnp.flip(ssd_chunked(jnp.flip(xs, 1), jnp.flip(dt[:, :, 1], 1), a[1],
                               jnp.flip(bm, 1), jnp.flip(cm, 1)), 1)
    y = y_f + y_b + d_skip.astype(jnp.float32)[:, None] * xs.astype(jnp.float32)
    y = y.reshape(bsz, seq, D_INNER) * jax.nn.silu(z.astype(jnp.float32))
    yg = y.reshape(bsz, seq, SSM_GROUPS, D_INNER // SSM_GROUPS)
    yg = yg * lax.rsqrt(jnp.mean(yg * yg, axis=-1, keepdims=True) + RMS_EPS)
    y = (yg.reshape(bsz, seq, D_INNER) * norm_g.astype(jnp.float32)).astype(h.dtype)
    return y @ w_out


def expert_dispatch(hf, expert_ids, expert_w, w1, w3, w2):
    t_tokens = hf.shape[0]
    n_assign = t_tokens * TOP_K_INNER
    n_blocks = -(-n_assign // MOE_BLOCK) + N_EXPERTS
    flat_e = expert_ids.reshape(-1)
    order = jnp.argsort(flat_e)
    sorted_e = flat_e[order]
    tok = (order // TOP_K_INNER).astype(jnp.int32)
    counts = jnp.bincount(flat_e, length=N_EXPERTS)
    padded = (counts + MOE_BLOCK - 1) // MOE_BLOCK * MOE_BLOCK
    pad_end = jnp.cumsum(padded)
    start = jnp.cumsum(counts) - counts
    dest = (pad_end - padded)[sorted_e] + jnp.arange(n_assign) - start[sorted_e]
    rows_tok = jnp.full((n_blocks * MOE_BLOCK,), t_tokens, jnp.int32).at[dest].set(tok)
    block_e = jnp.minimum(jnp.searchsorted(pad_end, jnp.arange(n_blocks) * MOE_BLOCK, side="right"),
                          N_EXPERTS - 1)
    h_pad = jnp.concatenate([hf, jnp.zeros((1, hf.shape[1]), hf.dtype)], axis=0)
    xin = h_pad[rows_tok].reshape(n_blocks, MOE_BLOCK, hf.shape[1])

    def run_block(args):
        xb, e = args
        return (jax.nn.silu(xb @ w1[e]) * (xb @ w3[e])) @ w2[e]

    yout = lax.map(run_block, (xin, block_e)).reshape(n_blocks * MOE_BLOCK, hf.shape[1])
    contrib = yout[dest].astype(jnp.float32) * expert_w.reshape(-1)[order][:, None]
    return jnp.zeros_like(hf).at[tok].add(contrib.astype(hf.dtype))


def hier_moe(h, w_rg, b_rg, w_re, b_re, w1, w3, w2):
    bsz, seq, d = h.shape
    hf = h.reshape(bsz * seq, d)
    g_prob = jax.nn.softmax((hf @ w_rg).astype(jnp.float32) + b_rg.astype(jnp.float32), axis=-1)
    g_p, g_idx = lax.top_k(g_prob, 1)
    e_logits = ((hf @ w_re).astype(jnp.float32) + b_re.astype(jnp.float32)).reshape(
        bsz * seq, N_EXPERT_GROUPS, EXPERTS_PER_GROUP)
    e_sel = e_logits[jnp.arange(bsz * seq), g_idx[:, 0]]
    e_p, e_idx = lax.top_k(jax.nn.softmax(e_sel, axis=-1), TOP_K_INNER)
    w = g_p * e_p / jnp.sum(e_p, axis=-1, keepdims=True)
    ids = g_idx * EXPERTS_PER_GROUP + e_idx
    return expert_dispatch(hf, ids, w, w1, w3, w2).reshape(bsz, seq, d)


def trunk(x, p, prm):
    cos, sin = rope_tables(x.shape[1])
    for i in range(DEPTH):
        j = i // 2
        if i % 2 == 0:
            h = rms_norm(x, prm["ln_mix_e"][j])
            x = x + fourier_window_mixer(h, prm["w_in_e"][j], prm["q_gain"][j], prm["k_gain"][j],
                                         prm["sink"][j], prm["w_out_e"][j], cos, sin)
        else:
            h = rms_norm(x, prm["ln_mix_o"][j])
            x = x + bidir_ssd_mixer(h, prm["w_in_o"][j], prm["conv_w"][j], prm["conv_b"][j],
                                    prm["dt_bias"][j], prm["a_log"][j], prm["d_skip"][j],
                                    prm["ssm_gain"][j], prm["w_out_o"][j])
        h = rms_norm(x, prm["ln_ffn"][i])
        x = x + hier_moe(h, prm["w_router_g"][i], prm["b_router_g"][i], prm["w_router_e"][i],
                         prm["b_router_e"][i], prm["w1"][i], prm["w3"][i], prm["w2"][i])
        gate = jax.nn.sigmoid((rms_norm(x, prm["ln_ple"][i]) @ prm["w_ple_gate"][i]).astype(jnp.float32))
        x = x + ((p[i] @ prm["w_ple_proj"][i]).astype(jnp.float32) * gate).astype(x.dtype)
    return x


def setup_inputs(seed: int = 0) -> dict:
    key = jax.random.key(seed)
    keys = iter(jax.random.split(key, 40))
    f32 = jnp.float32

    def normal(shape, scale):
        return jax.random.normal(next(keys), shape, f32) * scale

    def gain(shape):
        return 1.0 + 0.02 * jax.random.normal(next(keys), shape, f32)

    x_prompt = normal((BATCH, SEQ, D_MODEL), 1.0)
    x_sample = normal((DEC_BATCH, DEC_SEQ, D_MODEL), 1.0)
    p_prompt = normal((DEPTH, BATCH, SEQ, PLE_DIM), 1.0)
    p_sample = normal((DEPTH, DEC_BATCH, DEC_SEQ, PLE_DIM), 1.0)
    ln_mix_e = gain((N_EVEN, D_MODEL))
    w_in_e = normal((N_EVEN, D_MODEL, EVEN_IN), D_MODEL ** -0.5)
    q_gain = gain((N_EVEN, HEAD_DIM))
    k_gain = gain((N_EVEN, HEAD_DIM))
    sink = normal((N_EVEN, N_Q_HEADS), 0.5)
    w_out_e = normal((N_EVEN, EVEN_MIX, D_MODEL), EVEN_MIX ** -0.5)
    ln_mix_o = gain((N_ODD, D_MODEL))
    w_in_o = normal((N_ODD, D_MODEL, ODD_IN), D_MODEL ** -0.5)
    conv_w = normal((N_ODD, CONV_K, CONV_DIM), CONV_K ** -0.5)
    conv_b = normal((N_ODD, CONV_DIM), 0.01)
    dt0 = jnp.exp(jax.random.uniform(next(keys), (N_ODD, 2, SSM_HEADS), f32,
                                     math.log(1e-3), math.log(1e-1)))
    dt_bias = dt0 + jnp.log(-jnp.expm1(-dt0))
    a_log = jnp.log(jax.random.uniform(next(keys), (N_ODD, 2, SSM_HEADS), f32, 1.0, 16.0))
    d_skip = 1.0 + 0.1 * jax.random.normal(next(keys), (N_ODD, SSM_HEADS), f32)
    ssm_gain = gain((N_ODD, D_INNER))
    w_out_o = normal((N_ODD, D_INNER, D_MODEL), D_INNER ** -0.5)
    ln_ffn = gain((DEPTH, D_MODEL))
    w_router_g = normal((DEPTH, D_MODEL, N_EXPERT_GROUPS), D_MODEL ** -0.5)
    b_router_g = normal((DEPTH, N_EXPERT_GROUPS), 0.01)
    w_router_e = normal((DEPTH, D_MODEL, N_EXPERTS), D_MODEL ** -0.5)
    b_router_e = normal((DEPTH, N_EXPERTS), 0.01)
    w1 = normal((DEPTH, N_EXPERTS, D_MODEL, D_EXPERT), D_MODEL ** -0.5)
    w3 = normal((DEPTH, N_EXPERTS, D_MODEL, D_EXPERT), D_MODEL ** -0.5)
    w2 = normal((DEPTH, N_EXPERTS, D_EXPERT, D_MODEL), D_EXPERT ** -0.5)
    ln_ple = gain((DEPTH, D_MODEL))
    w_ple_gate = normal((DEPTH, D_MODEL, D_MODEL), D_MODEL ** -0.5)
    w_ple_proj = normal((DEPTH, PLE_DIM, D_MODEL), PLE_DIM ** -0.5)
    return {"x_prompt": x_prompt, "x_sample": x_sample, "p_prompt": p_prompt, "p_sample": p_sample,
            "ln_mix_e": ln_mix_e, "w_in_e": w_in_e, "q_gain": q_gain, "k_gain": k_gain,
            "sink": sink, "w_out_e": w_out_e,
            "ln_mix_o": ln_mix_o, "w_in_o": w_in_o, "conv_w": conv_w, "conv_b": conv_b,
            "dt_bias": dt_bias, "a_log": a_log, "d_skip": d_skip, "ssm_gain": ssm_gain,
            "w_out_o": w_out_o,
            "ln_ffn": ln_ffn, "w_router_g": w_router_g, "b_router_g": b_router_g,
            "w_router_e": w_router_e, "b_router_e": b_router_e, "w1": w1, "w3": w3, "w2": w2,
            "ln_ple": ln_ple, "w_ple_gate": w_ple_gate, "w_ple_proj": w_ple_proj}


def reference(x_prompt, x_sample, p_prompt, p_sample,
              ln_mix_e, w_in_e, q_gain, k_gain, sink, w_out_e,
              ln_mix_o, w_in_o, conv_w, conv_b, dt_bias, a_log, d_skip, ssm_gain, w_out_o,
              ln_ffn, w_router_g, b_router_g, w_router_e, b_router_e, w1, w3, w2,
              ln_ple, w_ple_gate, w_ple_proj):
    prm = dict(ln_mix_e=ln_mix_e, w_in_e=w_in_e, q_gain=q_gain, k_gain=k_gain, sink=sink,
               w_out_e=w_out_e, ln_mix_o=ln_mix_o, w_in_o=w_in_o, conv_w=conv_w, conv_b=conv_b,
               dt_bias=dt_bias, a_log=a_log, d_skip=d_skip, ssm_gain=ssm_gain, w_out_o=w_out_o,
               ln_ffn=ln_ffn, w_router_g=w_router_g, b_router_g=b_router_g,
               w_router_e=w_router_e, b_router_e=b_router_e, w1=w1, w3=w3, w2=w2,
               ln_ple=ln_ple, w_ple_gate=w_ple_gate, w_ple_proj=w_ple_proj)
    y_prompt = trunk(x_prompt, p_prompt, prm)
    y_sample = trunk(x_sample, p_sample, prm)
    return (y_prompt, y_sample)
```

```python
import functools
import math

import jax
import jax.numpy as jnp
from jax import lax
from jax.experimental import pallas as pl
from jax.experimental.pallas import tpu as pltpu

D_MODEL = 1024
DEPTH = 4
RMS_EPS = 1e-6
N_FOURIER_GROUPS = 4
FOURIER_WIDTH = 512
FOURIER_GROUP = 128
FFT_N2 = 128
HEAD_DIM = 64
N_Q_HEADS = 8
N_KV_HEADS = 2
ATTN_WIDTH = 512
KV_WIDTH = 128
ATTN_BLOCK = 128
ROPE_THETA = 10000.0
D_INNER = 2048
SSM_HEAD_DIM = 64
SSM_HEADS = 32
SSM_STATE = 128
SSM_GROUPS = 4
HEADS_PER_GROUP = 8
CONV_K = 5
CHUNK = 128
GN = SSM_GROUPS * SSM_STATE
CONV_DIM = D_INNER + 2 * GN
N_EXPERT_GROUPS = 4
EXPERTS_PER_GROUP = 8
N_EXPERTS = 32
D_EXPERT = 512
MOE_BLOCK = 256
PLE_DIM = 256

LANES = 128
NEG_BIG = -1e30
VMEM_LIMIT = 52 * 1024 * 1024

F32 = jnp.float32
BF16 = jnp.bfloat16


def _params(sem, vmem=VMEM_LIMIT):
    return pltpu.CompilerParams(dimension_semantics=sem, vmem_limit_bytes=vmem)


def _dot(a, b):
    return jnp.dot(a, b, preferred_element_type=F32)


def _dot_nt(a, b):
    return lax.dot_general(a, b, (((1,), (1,)), ((), ())), preferred_element_type=F32)


def _split3(v):
    hi = v.astype(BF16)
    r1 = v - hi.astype(F32)
    mid = r1.astype(BF16)
    lo = (r1 - mid.astype(F32)).astype(BF16)
    return hi, mid, lo


def _dot3(v, m_bf16):
    hi, mid, lo = _split3(v)
    return _dot(hi, m_bf16) + _dot(mid, m_bf16) + _dot(lo, m_bf16)


def _dot3_left(m_bf16, v):
    hi, mid, lo = _split3(v)
    return _dot(m_bf16, hi) + _dot(m_bf16, mid) + _dot(m_bf16, lo)


def _rms_rows(x, g):
    return x * lax.rsqrt(jnp.mean(x * x, axis=-1, keepdims=True) + RMS_EPS) * g


def _silu(x):
    return x * (1.0 / (1.0 + jnp.exp(-x)))


def _full(shape):
    return pl.BlockSpec(shape, lambda *_: (0,) * len(shape))


def _even_in_kernel(x_ref, g_ref, w_ref, dft_ref, bd_ref, qg_ref, kg_ref, cos_ref, sin_ref,
                    gout_ref, q_ref, k_ref, v_ref):
    h = _rms_rows(x_ref[...], g_ref[...]).astype(BF16)
    u = _dot(h, w_ref[...])
    a = u[:, :FOURIER_WIDTH].astype(BF16)
    for g in range(N_FOURIER_GROUPS):
        r = _dot(a[:, g * LANES:(g + 1) * LANES], dft_ref[...])
        gout_ref[:, g * LANES:(g + 1) * LANES] = r[:, :LANES].astype(BF16)
        gout_ref[:, FOURIER_WIDTH + g * LANES:FOURIER_WIDTH + (g + 1) * LANES] = r[:, LANES:].astype(BF16)

    cos = cos_ref[...]
    sin = sin_ref[...]

    def norm_rope(t, gain, width):
        n = width // LANES
        bd = bd_ref[:width, :width]
        t2 = t * t
        hi = t2.astype(BF16)
        lo = (t2 - hi.astype(F32)).astype(BF16)
        ms = _dot(hi, bd) + _dot(lo, bd)
        tn = t * lax.rsqrt(ms + RMS_EPS) * gain
        lane = lax.broadcasted_iota(jnp.int32, tn.shape, 1)
        first_half = (lane % HEAD_DIM) < (HEAD_DIM // 2)
        rot = jnp.where(first_half, pltpu.roll(tn, width - HEAD_DIM // 2, 1), pltpu.roll(tn, HEAD_DIM // 2, 1))
        c = jnp.concatenate([cos] * n, axis=1)
        s = jnp.concatenate([sin] * n, axis=1)
        return tn * c + rot * s

    q = norm_rope(u[:, 512:1024], qg_ref[...], ATTN_WIDTH) * (HEAD_DIM ** -0.5)
    k = norm_rope(u[:, 1024:1280], kg_ref[...], 2 * KV_WIDTH)
    q_ref[...] = q.astype(BF16)
    k_ref[...] = k.astype(BF16)
    v_ref[...] = u[:, 1280:1536].astype(BF16)


def _even_in(x2, seq, ln, w, dft, bd, qg, kg, cos_t, sin_t, tm=512):
    t = x2.shape[0]
    nseq = seq // tm
    row = lambda i: (i, 0)
    pos = lambda i: (i % nseq, 0)
    return pl.pallas_call(
        _even_in_kernel,
        grid=(t // tm,),
        in_specs=[pl.BlockSpec((tm, D_MODEL), row), _full((1, D_MODEL)), _full((D_MODEL, 1536)),
                  _full((LANES, 2 * LANES)), _full((512, 512)), _full((1, 512)), _full((1, 256)),
                  pl.BlockSpec((tm, LANES), pos), pl.BlockSpec((tm, LANES), pos)],
        out_specs=[pl.BlockSpec((tm, 1024), row), pl.BlockSpec((tm, 512), row),
                   pl.BlockSpec((tm, 256), row), pl.BlockSpec((tm, 256), row)],
        out_shape=[jax.ShapeDtypeStruct((t, 1024), BF16), jax.ShapeDtypeStruct((t, 512), BF16),
                   jax.ShapeDtypeStruct((t, 256), BF16), jax.ShapeDtypeStruct((t, 256), BF16)],
        compiler_params=_params(("parallel",)),
        name="even_in",
    )(x2, ln, w, dft, bd, qg, kg, cos_t, sin_t)


def _fft1_kernel(g_ref, c_ref, s_ref, y_ref):
    g = g_ref[0]
    p = _dot(c_ref[...], g)
    q = _dot(s_ref[...], g)
    for j in range(g.shape[1] // 1024):
        o = j * 1024
        y_ref[0, :, o:o + 512] = (p[:, o:o + 512] + q[:, o + 512:o + 1024]).astype(BF16)
        y_ref[0, :, o + 512:o + 1024] = (p[:, o + 512:o + 1024] - q[:, o:o + 512]).astype(BF16)


def _fft1(g3, c1, s1, cb=8192):
    b, n1, width = g3.shape
    blk = lambda i, j: (i, 0, j)
    return pl.pallas_call(
        _fft1_kernel,
        grid=(b, width // cb),
        in_specs=[pl.BlockSpec((1, n1, cb), blk), _full((n1, n1)), _full((n1, n1))],
        out_specs=pl.BlockSpec((1, n1, cb), blk),
        out_shape=jax.ShapeDtypeStruct(g3.shape, BF16),
        compiler_params=_params(("parallel", "parallel")),
        name="fft_stage1",
    )(g3, c1, s1)


def _fft2_kernel(y_ref, mc_ref, ms_ref, o_ref, *, scale):
    for j in range(y_ref.shape[1]):
        yr = y_ref[0, j, :, :512]
        yi = y_ref[0, j, :, 512:]
        r = _dot(mc_ref[j], yr) + _dot(ms_ref[j], yi)
        o_ref[0, :, j * 512:(j + 1) * 512] = (r * scale).astype(BF16)


def _fft2(y4, mc, ms, scale, kb=8):
    b, n1, n2, _ = y4.shape
    return pl.pallas_call(
        functools.partial(_fft2_kernel, scale=scale),
        grid=(n1 // kb, b),
        in_specs=[pl.BlockSpec((1, kb, n2, 1024), lambda k, i: (i, k, 0, 0)),
                  pl.BlockSpec((kb, n2, n2), lambda k, i: (k, 0, 0)),
                  pl.BlockSpec((kb, n2, n2), lambda k, i: (k, 0, 0))],
        out_specs=pl.BlockSpec((1, n2, kb * 512), lambda k, i: (i, 0, k)),
        out_shape=jax.ShapeDtypeStruct((b, n2, n1 * 512), BF16),
        compiler_params=_params(("parallel", "parallel")),
        name="fft_stage2",
    )(y4, mc, ms)


def _dft_tables(seq):
    n1, n2 = seq // FFT_N2, FFT_N2
    i1 = jnp.arange(n1, dtype=jnp.int32)
    ang1 = ((i1[:, None] * i1[None, :]) % n1).astype(F32) * (2.0 * math.pi / n1)
    c1, s1 = jnp.cos(ang1).astype(BF16), jnp.sin(ang1).astype(BF16)
    i2 = jnp.arange(n2, dtype=jnp.int32)
    ph = (i2[None, None, :] * i1[:, None, None] + n1 * i2[None, :, None] * i2[None, None, :]) % seq
    ang2 = ph.astype(F32) * (2.0 * math.pi / seq)
    mc, ms = jnp.cos(ang2).astype(BF16), jnp.sin(ang2).astype(BF16)
    ic = jnp.arange(FOURIER_GROUP, dtype=jnp.int32)
    angc = ((ic[:, None] * ic[None, :]) % FOURIER_GROUP).astype(F32) * (2.0 * math.pi / FOURIER_GROUP)
    dftc = jnp.concatenate([jnp.cos(angc), -jnp.sin(angc)], axis=1).astype(BF16)
    return c1, s1, mc, ms, dftc


def _attn_kernel(sink_ref, q_ref, kp_ref, ko_ref, kn_ref, vp_ref, vo_ref, vn_ref, o_ref, kc_ref, vc_ref):
    i = pl.program_id(1)
    last = pl.num_programs(1) - 1
    tq = q_ref.shape[0]
    nb = tq // ATTN_BLOCK
    kc_ref[0:ATTN_BLOCK] = kp_ref[...]
    kc_ref[ATTN_BLOCK:ATTN_BLOCK + tq] = ko_ref[...]
    kc_ref[ATTN_BLOCK + tq:] = kn_ref[...]
    vc_ref[0:ATTN_BLOCK] = vp_ref[...]
    vc_ref[ATTN_BLOCK:ATTN_BLOCK + tq] = vo_ref[...]
    vc_ref[ATTN_BLOCK + tq:] = vn_ref[...]

    w3 = 3 * ATTN_BLOCK
    qpos = lax.broadcasted_iota(jnp.int32, (ATTN_BLOCK, w3), 0) + ATTN_BLOCK
    kpos = lax.broadcasted_iota(jnp.int32, (ATTN_BLOCK, w3), 1)
    band = jnp.abs(kpos - qpos) <= ATTN_BLOCK
    lane = lax.broadcasted_iota(jnp.int32, (w3, LANES), 1)
    lo_half = lane < HEAD_DIM
    out_lane = lax.broadcasted_iota(jnp.int32, (ATTN_BLOCK, LANES), 1) < HEAD_DIM

    for jb in range(nb):
        ok = band
        if jb == 0:
            ok = ok & ((kpos >= ATTN_BLOCK) | (i > 0))
        if jb == nb - 1:
            ok = ok & ((kpos < 2 * ATTN_BLOCK) | (i < last))
        bias = jnp.where(ok, 0.0, NEG_BIG).astype(F32)
        qb = q_ref[jb * ATTN_BLOCK:(jb + 1) * ATTN_BLOCK, :]
        kw = kc_ref[jb * ATTN_BLOCK:jb * ATTN_BLOCK + w3, :]
        vw = vc_ref[jb * ATTN_BLOCK:jb * ATTN_BLOCK + w3, :]
        for g in range(N_KV_HEADS):
            kg = kw[:, g * LANES:(g + 1) * LANES]
            vg = vw[:, g * LANES:(g + 1) * LANES]
            zero = jnp.zeros_like(kg)
            kbd = jnp.concatenate([jnp.where(lo_half, kg, zero), jnp.where(lo_half, zero, kg)], axis=0)
            vbd = jnp.concatenate([jnp.where(lo_half, vg, zero), jnp.where(lo_half, zero, vg)], axis=0)
            for pr in range(2):
                col = (2 * g + pr) * LANES
                s = _dot_nt(qb[:, col:col + LANES], kbd)
                es, rs = [], []
                for half in range(2):
                    sk = sink_ref[4 * g + 2 * pr + half]
                    sh = s[:, half * w3:(half + 1) * w3] + bias
                    m = jnp.maximum(jnp.max(sh, axis=-1, keepdims=True), sk)
                    e = jnp.exp(sh - m)
                    den = jnp.sum(e, axis=-1, keepdims=True) + jnp.exp(sk - m)
                    es.append(e.astype(BF16))
                    rs.append(1.0 / den)
                o = _dot(jnp.concatenate(es, axis=1), vbd)
                o = o * jnp.where(out_lane, rs[0], rs[1])
                o_ref[jb * ATTN_BLOCK:(jb + 1) * ATTN_BLOCK, col:col + LANES] = o.astype(BF16)


def _attention(q, k, v, sink, bsz, seq, tq=512):
    t = q.shape[0]
    nq = seq // tq
    r = tq // ATTN_BLOCK
    nblk = t // ATTN_BLOCK
    own = lambda b, i, s: (b * nq + i, 0)
    prev = lambda b, i, s: (jnp.maximum((b * nq + i) * r - 1, 0), 0)
    nxt = lambda b, i, s: (jnp.minimum((b * nq + i + 1) * r, nblk - 1), 0)
    kv = lambda m: pl.BlockSpec((ATTN_BLOCK if m is not own else tq, 256), m)
    return pl.pallas_call(
        _attn_kernel,
        grid_spec=pltpu.PrefetchScalarGridSpec(
            num_scalar_prefetch=1,
            grid=(bsz, nq),
            in_specs=[pl.BlockSpec((tq, 512), own), kv(prev), kv(own), kv(nxt), kv(prev), kv(own), kv(nxt)],
            out_specs=pl.BlockSpec((tq, 512), own),
            scratch_shapes=[pltpu.VMEM((tq + 2 * ATTN_BLOCK, 256), BF16),
                            pltpu.VMEM((tq + 2 * ATTN_BLOCK, 256), BF16)]),
        out_shape=jax.ShapeDtypeStruct((t, 512), BF16),
        compiler_params=_params(("parallel", "parallel")),
        name="window_attention",
    )(sink, q, k, k, k, v, v, v)


def _even_out_kernel(x_ref, f_ref, o_ref, w_ref, out_ref):
    out_ref[...] = (x_ref[...] + _dot(f_ref[...], w_ref[:512, :]) + _dot(o_ref[...], w_ref[512:, :]))


def _even_out(x2, f, o, w, tm=512):
    t = x2.shape[0]
    row = lambda i: (i, 0)
    return pl.pallas_call(
        _even_out_kernel,
        grid=(t // tm,),
        in_specs=[pl.BlockSpec((tm, D_MODEL), row), pl.BlockSpec((tm, 512), row),
                  pl.BlockSpec((tm, 512), row), _full((1024, D_MODEL))],
        out_specs=pl.BlockSpec((tm, D_MODEL), row),
        out_shape=jax.ShapeDtypeStruct((t, D_MODEL), F32),
        compiler_params=_params(("parallel",)),
        name="even_out",
    )(x2, f, o, w)


def _even_mixer(x2, bsz, seq, ln, w_in, q_gain, k_gain, sink, w_out):
    c1, s1, mc, ms, dftc = _dft_tables(seq)
    n1 = seq // FFT_N2
    kcols = w_in[:, 1024:1152]
    vcols = w_in[:, 1152:1280]
    dup = lambda c: jnp.concatenate([c[:, :64], c[:, :64], c[:, 64:], c[:, 64:]], axis=1)
    w = jnp.concatenate([w_in[:, :1024], dup(kcols), dup(vcols)], axis=1).astype(BF16)
    hid = jnp.arange(512, dtype=jnp.int32) // HEAD_DIM
    bd = jnp.where(hid[:, None] == hid[None, :], 1.0 / HEAD_DIM, 0.0).astype(BF16)
    inv = 1.0 / (ROPE_THETA ** (jnp.arange(0, HEAD_DIM, 2, dtype=F32) / HEAD_DIM))
    ang = jnp.arange(seq, dtype=F32)[:, None] * inv[None, :]
    cos, sin = jnp.cos(ang), jnp.sin(ang)
    cos_t = jnp.concatenate([cos, cos, cos, cos], axis=1)
    sin_t = jnp.concatenate([-sin, sin, -sin, sin], axis=1)
    qg = jnp.tile(q_gain.astype(F32), N_Q_HEADS)[None, :]
    kg = jnp.tile(k_gain.astype(F32), 2 * N_KV_HEADS)[None, :]
    g, q, k, v = _even_in(x2, seq, ln[None, :], w, dftc, bd, qg, kg, cos_t, sin_t)
    y = _fft1(g.reshape(bsz, n1, FFT_N2 * 1024), c1, s1)
    scale = 1.0 / math.sqrt(seq * FOURIER_GROUP)
    f = _fft2(y.reshape(bsz, n1, FFT_N2, 1024), mc, ms, scale).reshape(bsz * seq, 512)
    o = _attention(q, k, v, sink.astype(F32), bsz, seq)
    return _even_out(x2, f, o, w_out.astype(BF16))


def _odd_in_kernel(x_ref, g_ref, w_ref, z_ref, xbc_ref, dt_ref):
    h = _rms_rows(x_ref[...], g_ref[...]).astype(BF16)
    z_ref[...] = _dot(h, w_ref[:, :D_INNER]).astype(BF16)
    xbc_ref[...] = _dot(h, w_ref[:, D_INNER:D_INNER + CONV_DIM]).astype(BF16)
    dt_ref[...] = _dot(h, w_ref[:, D_INNER + CONV_DIM:])


def _odd_in(x2, ln, w, tm=256):
    t = x2.shape[0]
    row = lambda i: (i, 0)
    wcols = D_INNER + CONV_DIM + LANES
    return pl.pallas_call(
        _odd_in_kernel,
        grid=(t // tm,),
        in_specs=[pl.BlockSpec((tm, D_MODEL), row), _full((1, D_MODEL)), _full((D_MODEL, wcols))],
        out_specs=[pl.BlockSpec((tm, D_INNER), row), pl.BlockSpec((tm, CONV_DIM), row),
                   pl.BlockSpec((tm, LANES), row)],
        out_shape=[jax.ShapeDtypeStruct((t, D_INNER), BF16), jax.ShapeDtypeStruct((t, CONV_DIM), BF16),
                   jax.ShapeDtypeStruct((t, LANES), F32)],
        compiler_params=_params(("parallel",)),
        name="odd_in",
    )(x2, ln, w)


CONV_HALO = 16


def _conv_kernel(prev_ref, main_ref, next_ref, w_ref, b_ref, o_ref, win_ref, *, tiles_per_seq):
    i = pl.program_id(0) % tiles_per_seq
    tc = main_ref.shape[0]
    keep_prev = (i > 0).astype(F32)
    keep_next = (i < tiles_per_seq - 1).astype(F32)
    win_ref[0:CONV_HALO, :] = prev_ref[...].astype(F32) * keep_prev
    win_ref[CONV_HALO:CONV_HALO + tc, :] = main_ref[...].astype(F32)
    win_ref[CONV_HALO + tc:, :] = next_ref[...].astype(F32) * keep_next
    cb = 512
    for c in range(CONV_DIM // cb):
        cols = slice(c * cb, (c + 1) * cb)
        acc = jnp.broadcast_to(b_ref[:, cols], (tc, cb))
        for k in range(CONV_K):
            start = CONV_HALO - CONV_K // 2 + k
            acc = acc + win_ref[start:start + tc, cols] * w_ref[k:k + 1, cols]
        o_ref[:, cols] = _silu(acc).astype(BF16)


def _conv(xbc, seq, w, b, tc=256):
    t = xbc.shape[0]
    r = tc // CONV_HALO
    nh = t // CONV_HALO
    return pl.pallas_call(
        functools.partial(_conv_kernel, tiles_per_seq=seq // tc),
        grid=(t // tc,),
        in_specs=[pl.BlockSpec((CONV_HALO, CONV_DIM), lambda i: (jnp.maximum(i * r - 1, 0), 0)),
                  pl.BlockSpec((tc, CONV_DIM), lambda i: (i, 0)),
                  pl.BlockSpec((CONV_HALO, CONV_DIM), lambda i: (jnp.minimum((i + 1) * r, nh - 1), 0)),
                  _full((CONV_K, CONV_DIM)), _full((1, CONV_DIM))],
        out_specs=pl.BlockSpec((tc, CONV_DIM), lambda i: (i, 0)),
        out_shape=jax.ShapeDtypeStruct((t, CONV_DIM), BF16),
        scratch_shapes=[pltpu.VMEM((tc + 2 * CONV_HALO, CONV_DIM), F32)],
        compiler_params=_params(("parallel",)),
        name="ssd_conv",
    )(xbc, xbc, xbc, w, b)


N_PAIRS = SSM_HEADS // 2
PAIRS_PER_GROUP = HEADS_PER_GROUP // 2


def _ssd_kernel(xbc_ref, dt_ref, bias_ref, alog_ref, exp_ref, y_ref, state_ref, *, reverse):
    c = pl.program_id(1)

    @pl.when(c == 0)
    def _():
        state_ref[...] = jnp.zeros_like(state_ref)

    off = SSM_HEADS if reverse else 0
    r_io = lax.broadcasted_iota(jnp.int32, (CHUNK, CHUNK), 0)
    c_io = lax.broadcasted_iota(jnp.int32, (CHUNK, CHUNK), 1)
    ge = r_io >= c_io
    le = r_io <= c_io
    mask = le if reverse else ge
    tri_col = jnp.where(mask, 1.0, 0.0).astype(BF16)
    tri_row = jnp.where(ge if reverse else le, 1.0, 0.0).astype(BF16)

    dt_lh = dt_ref[...] + bias_ref[...]
    dt_lh = jnp.maximum(dt_lh, 0.0) + jnp.log(1.0 + jnp.exp(-jnp.abs(dt_lh)))
    la_lh = dt_lh * (-jnp.exp(alog_ref[...]))
    acs_lh = _dot3_left(tri_col, la_lh)
    dt_hl = dt_lh.T
    la_hl = la_lh.T
    acs_hl = _dot3(la_hl, tri_row)
    edge = 0 if reverse else CHUNK - 1
    tot_hl = acs_hl[:, edge:edge + 1]
    wrow_hl = jnp.exp(tot_hl - acs_hl) * dt_hl
    tot_row = acs_lh[edge:edge + 1, :]
    cd_lanes = jnp.exp(_dot3(jnp.broadcast_to(tot_row, (8, LANES)), exp_ref[...])[0:1, :])

    lane = lax.broadcasted_iota(jnp.int32, (CHUNK, LANES), 1)
    lo_half = lane < SSM_HEAD_DIM

    for g in range(SSM_GROUPS):
        b_g = xbc_ref[:, D_INNER + g * SSM_STATE:D_INNER + (g + 1) * SSM_STATE]
        c_g = xbc_ref[:, D_INNER + GN + g * SSM_STATE:D_INNER + GN + (g + 1) * SSM_STATE]
        cb = _dot_nt(c_g, b_g)
        c_f = c_g.astype(F32)
        bt = b_g.astype(F32).T
        for pp in range(PAIRS_PER_GROUP):
            p = g * PAIRS_PER_GROUP + pp
            ms, cs, bw = [], [], []
            for half in range(2):
                hd = off + 2 * p + half
                colb = jnp.broadcast_to(acs_lh[:, hd:hd + 1], (CHUNK, CHUNK))
                rowb = acs_hl[hd:hd + 1, :]
                seg = jnp.where(mask, colb - rowb, NEG_BIG)
                ms.append((cb * jnp.exp(seg) * dt_hl[hd:hd + 1, :]).astype(BF16))
                cs.append((c_f * jnp.exp(colb)).astype(BF16))
                bw.append((bt * wrow_hl[hd:hd + 1, :]).astype(BF16))
            xp = xbc_ref[:, p * LANES:(p + 1) * LANES]
            zx = jnp.zeros_like(xp)
            xbd = jnp.concatenate([jnp.where(lo_half, xp, zx), jnp.where(lo_half, zx, xp)], axis=0)
            st = state_ref[:, p * LANES:(p + 1) * LANES]
            stb = st.astype(BF16)
            zs = jnp.zeros_like(stb)
            hbd = jnp.concatenate([jnp.where(lo_half, stb, zs), jnp.where(lo_half, zs, stb)], axis=0)
            lhs_y = jnp.concatenate(ms + cs, axis=1)
            y = _dot(lhs_y, jnp.concatenate([xbd, hbd], axis=0))
            y_ref[:, p * LANES:(p + 1) * LANES] = y.astype(BF16)
            snew = _dot(jnp.concatenate(bw, axis=1), xbd)
            state_ref[:, p * LANES:(p + 1) * LANES] = st * cd_lanes[:, p * LANES:(p + 1) * LANES] + snew


def _ssd(xbc_c, dt_raw, bias_row, alog_row, expand, bsz, seq, reverse):
    t = xbc_c.shape[0]
    nc = seq // CHUNK
    if reverse:
        blk = lambda b, c: (b * nc + nc - 1 - c, 0)
    else:
        blk = lambda b, c: (b * nc + c, 0)
    return pl.pallas_call(
        functools.partial(_ssd_kernel, reverse=reverse),
        grid=(bsz, nc),
        in_specs=[pl.BlockSpec((CHUNK, CONV_DIM), blk), pl.BlockSpec((CHUNK, LANES), blk),
                  _full((1, LANES)), _full((1, LANES)), _full((LANES, D_INNER))],
        out_specs=pl.BlockSpec((CHUNK, D_INNER), blk),
        out_shape=jax.ShapeDtypeStruct((t, D_INNER), BF16),
        scratch_shapes=[pltpu.VMEM((SSM_STATE, D_INNER), F32)],
        compiler_params=_params(("parallel", "arbitrary")),
        name="ssd_scan_bwd" if reverse else "ssd_scan_fwd",
    )(xbc_c, dt_raw, bias_row, alog_row, expand)


def _odd_out_kernel(x_ref, yf_ref, yb_ref, xs_ref, z_ref, d_ref, g_ref, w_ref, out_ref):
    y = yf_ref[...].astype(F32) + yb_ref[...].astype(F32) + d_ref[...] * xs_ref[...].astype(F32)
    y = y * _silu(z_ref[...].astype(F32))
    gw = D_INNER // SSM_GROUPS
    parts = []
    for g in range(SSM_GROUPS):
        yg = y[:, g * gw:(g + 1) * gw]
        parts.append(yg * lax.rsqrt(jnp.mean(yg * yg, axis=-1, keepdims=True) + RMS_EPS))
    yn = (jnp.concatenate(parts, axis=1) * g_ref[...]).astype(BF16)
    out_ref[...] = x_ref[...] + _dot(yn, w_ref[...])


def _odd_out(x2, yf, yb, xbc_c, z, dskip, gain, w, tm=256):
    t = x2.shape[0]
    row = lambda i: (i, 0)
    return pl.pallas_call(
        _odd_out_kernel,
        grid=(t // tm,),
        in_specs=[pl.BlockSpec((tm, D_MODEL), row), pl.BlockSpec((tm, D_INNER), row),
                  pl.BlockSpec((tm, D_INNER), row), pl.BlockSpec((tm, D_INNER), row),
                  pl.BlockSpec((tm, D_INNER), row), _full((1, D_INNER)), _full((1, D_INNER)),
                  _full((D_INNER, D_MODEL))],
        out_specs=pl.BlockSpec((tm, D_MODEL), row),
        out_shape=jax.ShapeDtypeStruct((t, D_MODEL), F32),
        compiler_params=_params(("parallel",)),
        name="odd_out",
    )(x2, yf, yb, xbc_c, z, dskip, gain, w)


def _odd_mixer(x2, bsz, seq, ln, w_in, conv_w, conv_b, dt_bias, a_log, d_skip, norm_g, w_out):
    pad = jnp.zeros((D_MODEL, LANES - 2 * SSM_HEADS), w_in.dtype)
    w = jnp.concatenate([w_in, pad], axis=1).astype(BF16)
    z, xbc, dt_raw = _odd_in(x2, ln[None, :], w)
    xbc_c = _conv(xbc, seq, conv_w.astype(F32), conv_b.astype(F32)[None, :])
    lane_pad = jnp.zeros((LANES - 2 * SSM_HEADS,), F32)
    bias_row = jnp.concatenate([dt_bias.astype(F32).reshape(-1), lane_pad])[None, :]
    alog_row = jnp.concatenate([a_log.astype(F32).reshape(-1), lane_pad])[None, :]
    ch = jnp.arange(D_INNER, dtype=jnp.int32) // SSM_HEAD_DIM
    hd = jnp.arange(LANES, dtype=jnp.int32)
    ys = []
    for reverse in (False, True):
        off = SSM_HEADS if reverse else 0
        expand = jnp.where(hd[:, None] == off + ch[None, :], 1.0, 0.0).astype(BF16)
        ys.append(_ssd(xbc_c, dt_raw, bias_row, alog_row, expand, bsz, seq, reverse))
    dskip = jnp.repeat(d_skip.astype(F32), SSM_HEAD_DIM)[None, :]
    return _odd_out(x2, ys[0], ys[1], xbc_c, z, dskip, norm_g.astype(F32)[None, :], w_out.astype(BF16))


ROUTE_LANES = N_EXPERT_GROUPS + N_EXPERTS


def _router_kernel(x_ref, g_ref, w_ref, b_ref, ls_ref, h_ref, meta_ref, cnt_ref, run_ref):
    i = pl.program_id(0)

    @pl.when(i == 0)
    def _():
        run_ref[...] = jnp.zeros_like(run_ref)

    h = _rms_rows(x_ref[...], g_ref[...])
    h_ref[...] = h
    hh = h.astype(BF16)
    hl = (h - hh.astype(F32)).astype(BF16)
    w_hi = w_ref[0]
    w_lo = w_ref[1]
    logits = _dot(hh, w_hi) + _dot(hl, w_hi) + _dot(hh, w_lo) + b_ref[...]
    tm = logits.shape[0]
    lane = lax.broadcasted_iota(jnp.int32, (tm, LANES), 1).astype(F32)

    def top1(v):
        m = jnp.max(v, axis=-1, keepdims=True)
        idx = jnp.min(jnp.where(v == m, lane, float(LANES)), axis=-1, keepdims=True)
        return m, idx

    gl = jnp.where(lane < N_EXPERT_GROUPS, logits, NEG_BIG)
    gmax, gidx = top1(gl)
    g_p = 1.0 / jnp.sum(jnp.exp(gl - gmax), axis=-1, keepdims=True)
    lo = N_EXPERT_GROUPS + EXPERTS_PER_GROUP * gidx
    e1 = jnp.where((lane >= lo) & (lane < lo + EXPERTS_PER_GROUP), logits, NEG_BIG)
    m1, i1 = top1(e1)
    e2 = jnp.where(lane == i1, NEG_BIG, e1)
    m2, i2 = top1(e2)
    r = jnp.exp(m2 - m1)
    w_a = g_p / (1.0 + r)
    w_b = g_p * r / (1.0 + r)
    e_a = i1 - N_EXPERT_GROUPS
    e_b = i2 - N_EXPERT_GROUPS

    onehot = jnp.where((lane == e_a) | (lane == e_b), 1.0, 0.0)
    before = _dot(ls_ref[...], onehot.astype(BF16)) + run_ref[...]
    rank_a = jnp.sum(jnp.where(lane == e_a, before, 0.0), axis=-1, keepdims=True)
    rank_b = jnp.sum(jnp.where(lane == e_b, before, 0.0), axis=-1, keepdims=True)
    run_ref[...] = run_ref[...] + jnp.sum(onehot, axis=0, keepdims=True)
    cnt_ref[...] = jnp.broadcast_to(run_ref[...], cnt_ref.shape)

    meta = jnp.where(lane == 0, e_a, 0.0)
    meta = jnp.where(lane == 1, e_b, meta)
    meta = jnp.where(lane == 2, rank_a, meta)
    meta = jnp.where(lane == 3, rank_b, meta)
    meta = jnp.where(lane == 4, w_a, meta)
    meta = jnp.where(lane == 5, w_b, meta)
    meta_ref[...] = meta


def _router(x2, ln, w2, b, lstrict, tm):
    t = x2.shape[0]
    row = lambda i: (i, 0)
    return pl.pallas_call(
        _router_kernel,
        grid=(t // tm,),
        in_specs=[pl.BlockSpec((tm, D_MODEL), row), _full((1, D_MODEL)), _full((2, D_MODEL, LANES)),
                  _full((1, LANES)), _full((tm, tm))],
        out_specs=[pl.BlockSpec((tm, D_MODEL), row), pl.BlockSpec((tm, LANES), row), _full((8, LANES))],
        out_shape=[jax.ShapeDtypeStruct((t, D_MODEL), F32), jax.ShapeDtypeStruct((t, LANES), F32),
                   jax.ShapeDtypeStruct((8, LANES), F32)],
        scratch_shapes=[pltpu.VMEM((1, LANES), F32)],
        compiler_params=_params(("arbitrary",)),
        name="moe_router",
    )(x2, ln, w2, b, lstrict)


def _row_copy(src_hbm, src_row, dst, dst_row, sem):
    return pltpu.make_async_copy(src_hbm.at[pl.ds(src_row, 1), :], dst.at[pl.ds(dst_row, 1), :], sem)


def _dispatch_kernel(start_ref, er_ref, h_hbm, xin_in, xin_hbm, sem, *, tm):
    del xin_in
    base = pl.program_id(0) * tm

    def issue(tk, carry):
        for k in range(2):
            dst = start_ref[er_ref[0, 0, 4 * tk + k]] + er_ref[0, 0, 4 * tk + 2 + k]
            _row_copy(h_hbm, base + tk, xin_hbm, dst, sem).start()
        return carry

    lax.fori_loop(0, tm, issue, 0)

    def drain(tk, carry):
        for k in range(2):
            _row_copy(h_hbm, 0, xin_hbm, 0, sem).wait()
        return carry

    lax.fori_loop(0, tm, drain, 0)


def _dispatch(start, er, h, xin0, tm):
    t = h.shape[0]
    return pl.pallas_call(
        functools.partial(_dispatch_kernel, tm=tm),
        grid_spec=pltpu.PrefetchScalarGridSpec(
            num_scalar_prefetch=1,
            grid=(t // tm,),
            in_specs=[pl.BlockSpec((1, 1, 4 * tm), lambda i, s: (i, 0, 0), memory_space=pltpu.SMEM),
                      pl.BlockSpec(memory_space=pl.ANY), pl.BlockSpec(memory_space=pl.ANY)],
            out_specs=pl.BlockSpec(memory_space=pl.ANY),
            scratch_shapes=[pltpu.SemaphoreType.DMA]),
        out_shape=jax.ShapeDtypeStruct(xin0.shape, xin0.dtype),
        input_output_aliases={3: 0},
        compiler_params=_params(("arbitrary",)),
        name="moe_dispatch",
    )(start, er, h, xin0)


def _experts_kernel(be_ref, nused_ref, x_ref, w1_ref, w3_ref, w2_ref, o_ref):
    i = pl.program_id(0)

    @pl.when(i < nused_ref[0])
    def _():
        x = x_ref[...].astype(BF16)
        a = _silu(_dot(x, w1_ref[0])) * _dot(x, w3_ref[0])
        o_ref[...] = _dot(a.astype(BF16), w2_ref[0])

    @pl.when(i >= nused_ref[0])
    def _():
        o_ref[...] = jnp.zeros_like(o_ref)


def _experts(block_e, nused, xin, w1, w3, w2):
    nb = xin.shape[0] // MOE_BLOCK
    row = lambda i, be, nu: (i, 0)
    wsel = lambda i, be, nu: (be[i], 0, 0)
    return pl.pallas_call(
        _experts_kernel,
        grid_spec=pltpu.PrefetchScalarGridSpec(
            num_scalar_prefetch=2,
            grid=(nb,),
            in_specs=[pl.BlockSpec((MOE_BLOCK, D_MODEL), row),
                      pl.BlockSpec((1, D_MODEL, D_EXPERT), wsel), pl.BlockSpec((1, D_MODEL, D_EXPERT), wsel),
                      pl.BlockSpec((1, D_EXPERT, D_MODEL), wsel)],
            out_specs=pl.BlockSpec((MOE_BLOCK, D_MODEL), row)),
        out_shape=jax.ShapeDtypeStruct(xin.shape, F32),
        compiler_params=_params(("parallel",)),
        name="moe_experts",
    )(block_e, nused, xin, w1, w3, w2)


def _combine_ple_kernel(start_ref, er_ref, x_ref, meta_ref, p_ref, g_ref, wg_ref, wp_ref, y_hbm,
                        out_ref, ya_ref, yb_ref, sem, *, tm):
    def issue(tk, carry):
        for k, buf in enumerate((ya_ref, yb_ref)):
            src = start_ref[er_ref[0, 0, 4 * tk + k]] + er_ref[0, 0, 4 * tk + 2 + k]
            _row_copy(y_hbm, src, buf, tk, sem).start()
        return carry

    lax.fori_loop(0, tm, issue, 0)

    def drain(tk, carry):
        for buf in (ya_ref, yb_ref):
            _row_copy(y_hbm, 0, buf, 0, sem).wait()
        return carry

    lax.fori_loop(0, tm, drain, 0)

    meta = meta_ref[...]
    x = x_ref[...] + (meta[:, 4:5] * ya_ref[...] + meta[:, 5:6] * yb_ref[...])
    hn = _rms_rows(x, g_ref[...]).astype(BF16)
    gate = 1.0 / (1.0 + jnp.exp(-_dot(hn, wg_ref[...])))
    out_ref[...] = x + _dot(p_ref[...].astype(BF16), wp_ref[...]) * gate


def _combine_ple(start, er, x2, meta, p2, ln, wg, wp, yout, tm):
    t = x2.shape[0]
    row = lambda i, s: (i, 0)
    full = lambda shape: pl.BlockSpec(shape, lambda i, s: (0,) * len(shape))
    return pl.pallas_call(
        functools.partial(_combine_ple_kernel, tm=tm),
        grid_spec=pltpu.PrefetchScalarGridSpec(
            num_scalar_prefetch=1,
            grid=(t // tm,),
            in_specs=[pl.BlockSpec((1, 1, 4 * tm), lambda i, s: (i, 0, 0), memory_space=pltpu.SMEM),
                      pl.BlockSpec((tm, D_MODEL), row), pl.BlockSpec((tm, LANES), row),
                      pl.BlockSpec((tm, PLE_DIM), row), full((1, D_MODEL)), full((D_MODEL, D_MODEL)),
                      full((PLE_DIM, D_MODEL)), pl.BlockSpec(memory_space=pl.ANY)],
            out_specs=pl.BlockSpec((tm, D_MODEL), row),
            scratch_shapes=[pltpu.VMEM((tm, D_MODEL), F32), pltpu.VMEM((tm, D_MODEL), F32),
                            pltpu.SemaphoreType.DMA]),
        out_shape=jax.ShapeDtypeStruct((t, D_MODEL), F32),
        compiler_params=_params(("arbitrary",)),
        name="moe_combine_ple",
    )(start, er, x2, meta, p2, ln, wg, wp, yout)


MOE_TM = 256


def _moe_ple(x2, p2, ln_ffn, w_rg, b_rg, w_re, b_re, w1, w3, w2, ln_ple, w_gate, w_proj):
    t = x2.shape[0]
    tm = MOE_TM
    wr = jnp.concatenate([w_rg, w_re, jnp.zeros((D_MODEL, LANES - ROUTE_LANES), w_rg.dtype)], axis=1).astype(F32)
    wr_hi = wr.astype(BF16)
    wr_lo = (wr - wr_hi.astype(F32)).astype(BF16)
    br = jnp.concatenate([b_rg, b_re, jnp.zeros((LANES - ROUTE_LANES,), b_rg.dtype)]).astype(F32)[None, :]
    ii = jnp.arange(tm, dtype=jnp.int32)
    lstrict = jnp.where(ii[:, None] > ii[None, :], 1.0, 0.0).astype(BF16)
    h, meta, cnt = _router(x2, ln_ffn[None, :], jnp.stack([wr_hi, wr_lo]), br, lstrict, tm)

    counts = cnt[0, :N_EXPERTS].astype(jnp.int32)
    padded = (counts + MOE_BLOCK - 1) // MOE_BLOCK * MOE_BLOCK
    pad_end = jnp.cumsum(padded)
    start = (pad_end - padded).astype(jnp.int32)
    nb = (t * 2) // MOE_BLOCK + N_EXPERTS
    block_e = jnp.minimum(jnp.searchsorted(pad_end, jnp.arange(nb, dtype=jnp.int32) * MOE_BLOCK, side="right"),
                          N_EXPERTS - 1).astype(jnp.int32)
    nused = (pad_end[-1:] // MOE_BLOCK).astype(jnp.int32)
    er = meta[:, :4].astype(jnp.int32).reshape(t // tm, 1, 4 * tm)

    xin = _dispatch(start, er, h, jnp.zeros((nb * MOE_BLOCK, D_MODEL), F32), tm)
    yout = _experts(block_e, nused, xin, w1.astype(BF16), w3.astype(BF16), w2.astype(BF16))
    return _combine_ple(start, er, x2, meta, p2, ln_ple[None, :], w_gate.astype(BF16), w_proj.astype(BF16),
                        yout, tm)


def _trunk(x, p, prm):
    bsz, seq, _ = x.shape
    x2 = x.reshape(bsz * seq, D_MODEL)
    for i in range(DEPTH):
        j = i // 2
        if i % 2 == 0:
            x2 = _even_mixer(x2, bsz, seq, prm["ln_mix_e"][j], prm["w_in_e"][j], prm["q_gain"][j],
                             prm["k_gain"][j], prm["sink"][j], prm["w_out_e"][j])
        else:
            x2 = _odd_mixer(x2, bsz, seq, prm["ln_mix_o"][j], prm["w_in_o"][j], prm["conv_w"][j],
                            prm["conv_b"][j], prm["dt_bias"][j], prm["a_log"][j], prm["d_skip"][j],
                            prm["ssm_gain"][j], prm["w_out_o"][j])
        x2 = _moe_ple(x2, p[i].reshape(bsz * seq, PLE_DIM), prm["ln_ffn"][i], prm["w_router_g"][i],
                      prm["b_router_g"][i], prm["w_router_e"][i], prm["b_router_e"][i], prm["w1"][i],
                      prm["w3"][i], prm["w2"][i], prm["ln_ple"][i], prm["w_ple_gate"][i], prm["w_ple_proj"][i])
    return x2.reshape(bsz, seq, D_MODEL)


def kernel(x_prompt, x_sample, p_prompt, p_sample, ln_mix_e, w_in_e, q_gain, k_gain, sink, w_out_e, ln_mix_o, w_in_o, conv_w, conv_b, dt_bias, a_log, d_skip, ssm_gain, w_out_o, ln_ffn, w_router_g, b_router_g, w_router_e, b_router_e, w1, w3, w2, ln_ple, w_ple_gate, w_ple_proj):
    prm = dict(ln_mix_e=ln_mix_e, w_in_e=w_in_e, q_gain=q_gain, k_gain=k_gain, sink=sink,
               w_out_e=w_out_e, ln_mix_o=ln_mix_o, w_in_o=w_in_o, conv_w=conv_w, conv_b=conv_b,
               dt_bias=dt_bias, a_log=a_log, d_skip=d_skip, ssm_gain=ssm_gain, w_out_o=w_out_o,
               ln_ffn=ln_ffn, w_router_g=w_router_g, b_router_g=b_router_g,
               w_router_e=w_router_e, b_router_e=b_router_e, w1=w1, w3=w3, w2=w2,
               ln_ple=ln_ple, w_ple_gate=w_ple_gate, w_ple_proj=w_ple_proj)
    return (_trunk(x_prompt, p_prompt, prm), _trunk(x_sample, p_sample, prm))
```

```python
import functools
import math

import jax
import jax.numpy as jnp
from jax import lax
from jax.experimental import pallas as pl
from jax.experimental.pallas import tpu as pltpu

D_MODEL = 1024
DEPTH = 4
RMS_EPS = 1e-6
N_FOURIER_GROUPS = 4
FOURIER_WIDTH = 512
FOURIER_GROUP = 128
FFT_N2 = 128
HEAD_DIM = 64
N_Q_HEADS = 8
N_KV_HEADS = 2
ATTN_WIDTH = 512
KV_WIDTH = 128
ATTN_BLOCK = 128
ROPE_THETA = 10000.0
D_INNER = 2048
SSM_HEAD_DIM = 64
SSM_HEADS = 32
SSM_STATE = 128
SSM_GROUPS = 4
HEADS_PER_GROUP = 8
CONV_K = 5
CHUNK = 128
GN = SSM_GROUPS * SSM_STATE
CONV_DIM = D_INNER + 2 * GN
N_EXPERT_GROUPS = 4
EXPERTS_PER_GROUP = 8
N_EXPERTS = 32
D_EXPERT = 512
MOE_BLOCK = 256
PLE_DIM = 256

LANES = 128
NEG_BIG = -1e30
VMEM_LIMIT = 52 * 1024 * 1024

F32 = jnp.float32
BF16 = jnp.bfloat16


def _params(sem, vmem=VMEM_LIMIT):
    return pltpu.CompilerParams(dimension_semantics=sem, vmem_limit_bytes=vmem)


def _dot(a, b):
    return jnp.dot(a, b, preferred_element_type=F32)


def _dot_nt(a, b):
    return lax.dot_general(a, b, (((1,), (1,)), ((), ())), preferred_element_type=F32)


def _split3(v):
    hi = v.astype(BF16)
    r1 = v - hi.astype(F32)
    mid = r1.astype(BF16)
    lo = (r1 - mid.astype(F32)).astype(BF16)
    return hi, mid, lo


def _dot3(v, m_bf16):
    hi, mid, lo = _split3(v)
    return _dot(hi, m_bf16) + _dot(mid, m_bf16) + _dot(lo, m_bf16)


def _dot3_left(m_bf16, v):
    hi, mid, lo = _split3(v)
    return _dot(m_bf16, hi) + _dot(m_bf16, mid) + _dot(m_bf16, lo)


def _rms_rows(x, g):
    return x * lax.rsqrt(jnp.mean(x * x, axis=-1, keepdims=True) + RMS_EPS) * g


def _silu(x):
    return x * (1.0 / (1.0 + jnp.exp(-x)))


def _full(shape):
    return pl.BlockSpec(shape, lambda *_: (0,) * len(shape))


def _even_in_kernel(x_ref, g_ref, w_ref, dft_ref, bd_ref, qg_ref, kg_ref, cos_ref, sin_ref,
                    gout_ref, q_ref, k_ref, v_ref):
    h = _rms_rows(x_ref[...], g_ref[...]).astype(BF16)
    u = _dot(h, w_ref[...])
    a = u[:, :FOURIER_WIDTH].astype(BF16)
    for g in range(N_FOURIER_GROUPS):
        r = _dot(a[:, g * LANES:(g + 1) * LANES], dft_ref[...])
        gout_ref[:, g * LANES:(g + 1) * LANES] = r[:, :LANES].astype(BF16)
        gout_ref[:, FOURIER_WIDTH + g * LANES:FOURIER_WIDTH + (g + 1) * LANES] = r[:, LANES:].astype(BF16)

    cos = cos_ref[...]
    sin = sin_ref[...]

    def norm_rope(t, gain, width):
        n = width // LANES
        bd = bd_ref[:width, :width]
        t2 = t * t
        hi = t2.astype(BF16)
        lo = (t2 - hi.astype(F32)).astype(BF16)
        ms = _dot(hi, bd) + _dot(lo, bd)
        tn = t * lax.rsqrt(ms + RMS_EPS) * gain
        lane = lax.broadcasted_iota(jnp.int32, tn.shape, 1)
        first_half = (lane % HEAD_DIM) < (HEAD_DIM // 2)
        rot = jnp.where(first_half, pltpu.roll(tn, width - HEAD_DIM // 2, 1), pltpu.roll(tn, HEAD_DIM // 2, 1))
        c = jnp.concatenate([cos] * n, axis=1)
        s = jnp.concatenate([sin] * n, axis=1)
        return tn * c + rot * s

    q = norm_rope(u[:, 512:1024], qg_ref[...], ATTN_WIDTH) * (HEAD_DIM ** -0.5)
    k = norm_rope(u[:, 1024:1280], kg_ref[...], 2 * KV_WIDTH)
    q_ref[...] = q.astype(BF16)
    k_ref[...] = k.astype(BF16)
    v_ref[...] = u[:, 1280:1536].astype(BF16)


def _even_in(x2, seq, ln, w, dft, bd, qg, kg, cos_t, sin_t, tm=512):
    t = x2.shape[0]
    nseq = seq // tm
    row = lambda i: (i, 0)
    pos = lambda i: (i % nseq, 0)
    return pl.pallas_call(
        _even_in_kernel,
        grid=(t // tm,),
        in_specs=[pl.BlockSpec((tm, D_MODEL), row), _full((1, D_MODEL)), _full((D_MODEL, 1536)),
                  _full((LANES, 2 * LANES)), _full((512, 512)), _full((1, 512)), _full((1, 256)),
                  pl.BlockSpec((tm, LANES), pos), pl.BlockSpec((tm, LANES), pos)],
        out_specs=[pl.BlockSpec((tm, 1024), row), pl.BlockSpec((tm, 512), row),
                   pl.BlockSpec((tm, 256), row), pl.BlockSpec((tm, 256), row)],
        out_shape=[jax.ShapeDtypeStruct((t, 1024), BF16), jax.ShapeDtypeStruct((t, 512), BF16),
                   jax.ShapeDtypeStruct((t, 256), BF16), jax.ShapeDtypeStruct((t, 256), BF16)],
        compiler_params=_params(("parallel",)),
        name="even_in",
    )(x2, ln, w, dft, bd, qg, kg, cos_t, sin_t)


def _fft1_kernel(g_ref, c_ref, s_ref, y_ref):
    g = g_ref[0]
    p = _dot(c_ref[...], g)
    q = _dot(s_ref[...], g)
    for j in range(g.shape[1] // 1024):
        o = j * 1024
        y_ref[0, :, o:o + 512] = (p[:, o:o + 512] + q[:, o + 512:o + 1024]).astype(BF16)
        y_ref[0, :, o + 512:o + 1024] = (p[:, o + 512:o + 1024] - q[:, o:o + 512]).astype(BF16)


def _fft1(g3, c1, s1, cb=8192):
    b, n1, width = g3.shape
    blk = lambda i, j: (i, 0, j)
    return pl.pallas_call(
        _fft1_kernel,
        grid=(b, width // cb),
        in_specs=[pl.BlockSpec((1, n1, cb), blk), _full((n1, n1)), _full((n1, n1))],
        out_specs=pl.BlockSpec((1, n1, cb), blk),
        out_shape=jax.ShapeDtypeStruct(g3.shape, BF16),
        compiler_params=_params(("parallel", "parallel")),
        name="fft_stage1",
    )(g3, c1, s1)


def _fft2_kernel(y_ref, mc_ref, ms_ref, o_ref, *, scale):
    for j in range(y_ref.shape[1]):
        yr = y_ref[0, j, :, :512]
        yi = y_ref[0, j, :, 512:]
        r = _dot(mc_ref[j], yr) + _dot(ms_ref[j], yi)
        o_ref[0, :, j * 512:(j + 1) * 512] = (r * scale).astype(BF16)


def _fft2(y4, mc, ms, scale, kb=8):
    b, n1, n2, _ = y4.shape
    return pl.pallas_call(
        functools.partial(_fft2_kernel, scale=scale),
        grid=(n1 // kb, b),
        in_specs=[pl.BlockSpec((1, kb, n2, 1024), lambda k, i: (i, k, 0, 0)),
                  pl.BlockSpec((kb, n2, n2), lambda k, i: (k, 0, 0)),
                  pl.BlockSpec((kb, n2, n2), lambda k, i: (k, 0, 0))],
        out_specs=pl.BlockSpec((1, n2, kb * 512), lambda k, i: (i, 0, k)),
        out_shape=jax.ShapeDtypeStruct((b, n2, n1 * 512), BF16),
        compiler_params=_params(("parallel", "parallel")),
        name="fft_stage2",
    )(y4, mc, ms)


def _dft_tables(seq):
    n1, n2 = seq // FFT_N2, FFT_N2
    i1 = jnp.arange(n1, dtype=jnp.int32)
    ang1 = ((i1[:, None] * i1[None, :]) % n1).astype(F32) * (2.0 * math.pi / n1)
    c1, s1 = jnp.cos(ang1).astype(BF16), jnp.sin(ang1).astype(BF16)
    i2 = jnp.arange(n2, dtype=jnp.int32)
    ph = (i2[None, None, :] * i1[:, None, None] + n1 * i2[None, :, None] * i2[None, None, :]) % seq
    ang2 = ph.astype(F32) * (2.0 * math.pi / seq)
    mc, ms = jnp.cos(ang2).astype(BF16), jnp.sin(ang2).astype(BF16)
    ic = jnp.arange(FOURIER_GROUP, dtype=jnp.int32)
    angc = ((ic[:, None] * ic[None, :]) % FOURIER_GROUP).astype(F32) * (2.0 * math.pi / FOURIER_GROUP)
    dftc = jnp.concatenate([jnp.cos(angc), -jnp.sin(angc)], axis=1).astype(BF16)
    return c1, s1, mc, ms, dftc


def _attn_kernel(sink_ref, q_ref, kp_ref, ko_ref, kn_ref, vp_ref, vo_ref, vn_ref, o_ref, kc_ref, vc_ref):
    i = pl.program_id(1)
    last = pl.num_programs(1) - 1
    tq = q_ref.shape[0]
    nb = tq // ATTN_BLOCK
    kc_ref[0:ATTN_BLOCK] = kp_ref[...]
    kc_ref[ATTN_BLOCK:ATTN_BLOCK + tq] = ko_ref[...]
    kc_ref[ATTN_BLOCK + tq:] = kn_ref[...]
    vc_ref[0:ATTN_BLOCK] = vp_ref[...]
    vc_ref[ATTN_BLOCK:ATTN_BLOCK + tq] = vo_ref[...]
    vc_ref[ATTN_BLOCK + tq:] = vn_ref[...]

    w3 = 3 * ATTN_BLOCK
    qpos = lax.broadcasted_iota(jnp.int32, (ATTN_BLOCK, w3), 0) + ATTN_BLOCK
    kpos = lax.broadcasted_iota(jnp.int32, (ATTN_BLOCK, w3), 1)
    band = jnp.abs(kpos - qpos) <= ATTN_BLOCK
    lane = lax.broadcasted_iota(jnp.int32, (w3, LANES), 1)
    lo_half = lane < HEAD_DIM
    out_lane = lax.broadcasted_iota(jnp.int32, (ATTN_BLOCK, LANES), 1) < HEAD_DIM

    for jb in range(nb):
        ok = band
        if jb == 0:
            ok = ok & ((kpos >= ATTN_BLOCK) | (i > 0))
        if jb == nb - 1:
            ok = ok & ((kpos < 2 * ATTN_BLOCK) | (i < last))
        bias = jnp.where(ok, 0.0, NEG_BIG).astype(F32)
        qb = q_ref[jb * ATTN_BLOCK:(jb + 1) * ATTN_BLOCK, :]
        kw = kc_ref[jb * ATTN_BLOCK:jb * ATTN_BLOCK + w3, :]
        vw = vc_ref[jb * ATTN_BLOCK:jb * ATTN_BLOCK + w3, :]
        for g in range(N_KV_HEADS):
            kg = kw[:, g * LANES:(g + 1) * LANES]
            vg = vw[:, g * LANES:(g + 1) * LANES]
            zero = jnp.zeros_like(kg)
            kbd = jnp.concatenate([jnp.where(lo_half, kg, zero), jnp.where(lo_half, zero, kg)], axis=0)
            vbd = jnp.concatenate([jnp.where(lo_half, vg, zero), jnp.where(lo_half, zero, vg)], axis=0)
            for pr in range(2):
                col = (2 * g + pr) * LANES
                s = _dot_nt(qb[:, col:col + LANES], kbd)
                es, rs = [], []
                for half in range(2):
                    sk = sink_ref[4 * g + 2 * pr + half]
                    sh = s[:, half * w3:(half + 1) * w3] + bias
                    m = jnp.maximum(jnp.max(sh, axis=-1, keepdims=True), sk)
                    e = jnp.exp(sh - m)
                    den = jnp.sum(e, axis=-1, keepdims=True) + jnp.exp(sk - m)
                    es.append(e.astype(BF16))
                    rs.append(1.0 / den)
                o = _dot(jnp.concatenate(es, axis=1), vbd)
                o = o * jnp.where(out_lane, rs[0], rs[1])
                o_ref[jb * ATTN_BLOCK:(jb + 1) * ATTN_BLOCK, col:col + LANES] = o.astype(BF16)


def _attention(q, k, v, sink, bsz, seq, tq=512):
    t = q.shape[0]
    nq = seq // tq
    r = tq // ATTN_BLOCK
    nblk = t // ATTN_BLOCK
    own = lambda b, i, s: (b * nq + i, 0)
    prev = lambda b, i, s: (jnp.maximum((b * nq + i) * r - 1, 0), 0)
    nxt = lambda b, i, s: (jnp.minimum((b * nq + i + 1) * r, nblk - 1), 0)
    kv = lambda m: pl.BlockSpec((ATTN_BLOCK if m is not own else tq, 256), m)
    return pl.pallas_call(
        _attn_kernel,
        grid_spec=pltpu.PrefetchScalarGridSpec(
            num_scalar_prefetch=1,
            grid=(bsz, nq),
            in_specs=[pl.BlockSpec((tq, 512), own), kv(prev), kv(own), kv(nxt), kv(prev), kv(own), kv(nxt)],
            out_specs=pl.BlockSpec((tq, 512), own),
            scratch_shapes=[pltpu.VMEM((tq + 2 * ATTN_BLOCK, 256), BF16),
                            pltpu.VMEM((tq + 2 * ATTN_BLOCK, 256), BF16)]),
        out_shape=jax.ShapeDtypeStruct((t, 512), BF16),
        compiler_params=_params(("parallel", "parallel")),
        name="window_attention",
    )(sink, q, k, k, k, v, v, v)


def _even_out_kernel(x_ref, f_ref, o_ref, w_ref, out_ref):
    out_ref[...] = (x_ref[...] + _dot(f_ref[...], w_ref[:512, :]) + _dot(o_ref[...], w_ref[512:, :]))


def _even_out(x2, f, o, w, tm=512):
    t = x2.shape[0]
    row = lambda i: (i, 0)
    return pl.pallas_call(
        _even_out_kernel,
        grid=(t // tm,),
        in_specs=[pl.BlockSpec((tm, D_MODEL), row), pl.BlockSpec((tm, 512), row),
                  pl.BlockSpec((tm, 512), row), _full((1024, D_MODEL))],
        out_specs=pl.BlockSpec((tm, D_MODEL), row),
        out_shape=jax.ShapeDtypeStruct((t, D_MODEL), F32),
        compiler_params=_params(("parallel",)),
        name="even_out",
    )(x2, f, o, w)


def _even_mixer(x2, bsz, seq, ln, w_in, q_gain, k_gain, sink, w_out):
    c1, s1, mc, ms, dftc = _dft_tables(seq)
    n1 = seq // FFT_N2
    kcols = w_in[:, 1024:1152]
    vcols = w_in[:, 1152:1280]
    dup = lambda c: jnp.concatenate([c[:, :64], c[:, :64], c[:, 64:], c[:, 64:]], axis=1)
    w = jnp.concatenate([w_in[:, :1024], dup(kcols), dup(vcols)], axis=1).astype(BF16)
    hid = jnp.arange(512, dtype=jnp.int32) // HEAD_DIM
    bd = jnp.where(hid[:, None] == hid[None, :], 1.0 / HEAD_DIM, 0.0).astype(BF16)
    inv = 1.0 / (ROPE_THETA ** (jnp.arange(0, HEAD_DIM, 2, dtype=F32) / HEAD_DIM))
    ang = jnp.arange(seq, dtype=F32)[:, None] * inv[None, :]
    cos, sin = jnp.cos(ang), jnp.sin(ang)
    cos_t = jnp.concatenate([cos, cos, cos, cos], axis=1)
    sin_t = jnp.concatenate([-sin, sin, -sin, sin], axis=1)
    qg = jnp.tile(q_gain.astype(F32), N_Q_HEADS)[None, :]
    kg = jnp.tile(k_gain.astype(F32), 2 * N_KV_HEADS)[None, :]
    g, q, k, v = _even_in(x2, seq, ln[None, :], w, dftc, bd, qg, kg, cos_t, sin_t)
    y = _fft1(g.reshape(bsz, n1, FFT_N2 * 1024), c1, s1)
    scale = 1.0 / math.sqrt(seq * FOURIER_GROUP)
    f = _fft2(y.reshape(bsz, n1, FFT_N2, 1024), mc, ms, scale).reshape(bsz * seq, 512)
    o = _attention(q, k, v, sink.astype(F32), bsz, seq)
    return _even_out(x2, f, o, w_out.astype(BF16))


def _odd_in_kernel(x_ref, g_ref, w_ref, z_ref, xbc_ref, dt_ref):
    h = _rms_rows(x_ref[...], g_ref[...]).astype(BF16)
    z_ref[...] = _dot(h, w_ref[:, :D_INNER]).astype(BF16)
    xbc_ref[...] = _dot(h, w_ref[:, D_INNER:D_INNER + CONV_DIM]).astype(BF16)
    dt_ref[...] = _dot(h, w_ref[:, D_INNER + CONV_DIM:])


def _odd_in(x2, ln, w, tm=256):
    t = x2.shape[0]
    row = lambda i: (i, 0)
    wcols = D_INNER + CONV_DIM + LANES
    return pl.pallas_call(
        _odd_in_kernel,
        grid=(t // tm,),
        in_specs=[pl.BlockSpec((tm, D_MODEL), row), _full((1, D_MODEL)), _full((D_MODEL, wcols))],
        out_specs=[pl.BlockSpec((tm, D_INNER), row), pl.BlockSpec((tm, CONV_DIM), row),
                   pl.BlockSpec((tm, LANES), row)],
        out_shape=[jax.ShapeDtypeStruct((t, D_INNER), BF16), jax.ShapeDtypeStruct((t, CONV_DIM), BF16),
                   jax.ShapeDtypeStruct((t, LANES), F32)],
        compiler_params=_params(("parallel",)),
        name="odd_in",
    )(x2, ln, w)


CONV_HALO = 16


def _conv_kernel(prev_ref, main_ref, next_ref, w_ref, b_ref, o_ref, win_ref, *, tiles_per_seq):
    i = pl.program_id(0) % tiles_per_seq
    tc = main_ref.shape[0]
    keep_prev = (i > 0).astype(F32)
    keep_next = (i < tiles_per_seq - 1).astype(F32)
    win_ref[0:CONV_HALO, :] = prev_ref[...].astype(F32) * keep_prev
    win_ref[CONV_HALO:CONV_HALO + tc, :] = main_ref[...].astype(F32)
    win_ref[CONV_HALO + tc:, :] = next_ref[...].astype(F32) * keep_next
    cb = 512
    for c in range(CONV_DIM // cb):
        cols = slice(c * cb, (c + 1) * cb)
        acc = jnp.broadcast_to(b_ref[:, cols], (tc, cb))
        for k in range(CONV_K):
            start = CONV_HALO - CONV_K // 2 + k
            acc = acc + win_ref[start:start + tc, cols] * w_ref[k:k + 1, cols]
        o_ref[:, cols] = _silu(acc).astype(BF16)


def _conv(xbc, seq, w, b, tc=256):
    t = xbc.shape[0]
    r = tc // CONV_HALO
    nh = t // CONV_HALO
    return pl.pallas_call(
        functools.partial(_conv_kernel, tiles_per_seq=seq // tc),
        grid=(t // tc,),
        in_specs=[pl.BlockSpec((CONV_HALO, CONV_DIM), lambda i: (jnp.maximum(i * r - 1, 0), 0)),
                  pl.BlockSpec((tc, CONV_DIM), lambda i: (i, 0)),
                  pl.BlockSpec((CONV_HALO, CONV_DIM), lambda i: (jnp.minimum((i + 1) * r, nh - 1), 0)),
                  _full((CONV_K, CONV_DIM)), _full((1, CONV_DIM))],
        out_specs=pl.BlockSpec((tc, CONV_DIM), lambda i: (i, 0)),
        out_shape=jax.ShapeDtypeStruct((t, CONV_DIM), BF16),
        scratch_shapes=[pltpu.VMEM((tc + 2 * CONV_HALO, CONV_DIM), F32)],
        compiler_params=_params(("parallel",)),
        name="ssd_conv",
    )(xbc, xbc, xbc, w, b)


N_PAIRS = SSM_HEADS // 2
PAIRS_PER_GROUP = HEADS_PER_GROUP // 2


def _ssd_kernel(xbc_ref, dt_ref, bias_ref, alog_ref, exp_ref, y_ref, state_ref, *, reverse):
    c = pl.program_id(1)

    @pl.when(c == 0)
    def _():
        state_ref[...] = jnp.zeros_like(state_ref)

    off = SSM_HEADS if reverse else 0
    r_io = lax.broadcasted_iota(jnp.int32, (CHUNK, CHUNK), 0)
    c_io = lax.broadcasted_iota(jnp.int32, (CHUNK, CHUNK), 1)
    ge = r_io >= c_io
    le = r_io <= c_io
    mask = le if reverse else ge
    tri_col = jnp.where(mask, 1.0, 0.0).astype(BF16)
    tri_row = jnp.where(ge if reverse else le, 1.0, 0.0).astype(BF16)

    dt_lh = dt_ref[...] + bias_ref[...]
    dt_lh = jnp.maximum(dt_lh, 0.0) + jnp.log(1.0 + jnp.exp(-jnp.abs(dt_lh)))
    la_lh = dt_lh * (-jnp.exp(alog_ref[...]))
    acs_lh = _dot3_left(tri_col, la_lh)
    dt_hl = dt_lh.T
    la_hl = la_lh.T
    acs_hl = _dot3(la_hl, tri_row)
    edge = 0 if reverse else CHUNK - 1
    tot_hl = acs_hl[:, edge:edge + 1]
    wrow_hl = jnp.exp(tot_hl - acs_hl) * dt_hl
    tot_row = acs_lh[edge:edge + 1, :]
    cd_lanes = jnp.exp(_dot3(jnp.broadcast_to(tot_row, (8, LANES)), exp_ref[...])[0:1, :])

    lane = lax.broadcasted_iota(jnp.int32, (CHUNK, LANES), 1)
    lo_half = lane < SSM_HEAD_DIM

    for g in range(SSM_GROUPS):
        b_g = xbc_ref[:, D_INNER + g * SSM_STATE:D_INNER + (g + 1) * SSM_STATE]
        c_g = xbc_ref[:, D_INNER + GN + g * SSM_STATE:D_INNER + GN + (g + 1) * SSM_STATE]
        cb = _dot_nt(c_g, b_g)
        c_f = c_g.astype(F32)
        bt = b_g.astype(F32).T
        for pp in range(PAIRS_PER_GROUP):
            p = g * PAIRS_PER_GROUP + pp
            ms, cs, bw = [], [], []
            for half in range(2):
                hd = off + 2 * p + half
                colb = jnp.broadcast_to(acs_lh[:, hd:hd + 1], (CHUNK, CHUNK))
                rowb = acs_hl[hd:hd + 1, :]
                seg = jnp.where(mask, colb - rowb, NEG_BIG)
                ms.append((cb * jnp.exp(seg) * dt_hl[hd:hd + 1, :]).astype(BF16))
                cs.append((c_f * jnp.exp(colb)).astype(BF16))
                bw.append((bt * wrow_hl[hd:hd + 1, :]).astype(BF16))
            xp = xbc_ref[:, p * LANES:(p + 1) * LANES]
            zx = jnp.zeros_like(xp)
            xbd = jnp.concatenate([jnp.where(lo_half, xp, zx), jnp.where(lo_half, zx, xp)], axis=0)
            st = state_ref[:, p * LANES:(p + 1) * LANES]
            stb = st.astype(BF16)
            zs = jnp.zeros_like(stb)
            hbd = jnp.concatenate([jnp.where(lo_half, stb, zs), jnp.where(lo_half, zs, stb)], axis=0)
            lhs_y = jnp.concatenate(ms + cs, axis=1)
            y = _dot(lhs_y, jnp.concatenate([xbd, hbd], axis=0))
            y_ref[:, p * LANES:(p + 1) * LANES] = y.astype(BF16)
            snew = _dot(jnp.concatenate(bw, axis=1), xbd)
            state_ref[:, p * LANES:(p + 1) * LANES] = st * cd_lanes[:, p * LANES:(p + 1) * LANES] + snew


def _ssd(xbc_c, dt_raw, bias_row, alog_row, expand, bsz, seq, reverse):
    t = xbc_c.shape[0]
    nc = seq // CHUNK
    if reverse:
        blk = lambda b, c: (b * nc + nc - 1 - c, 0)
    else:
        blk = lambda b, c: (b * nc + c, 0)
    return pl.pallas_call(
        functools.partial(_ssd_kernel, reverse=reverse),
        grid=(bsz, nc),
        in_specs=[pl.BlockSpec((CHUNK, CONV_DIM), blk), pl.BlockSpec((CHUNK, LANES), blk),
                  _full((1, LANES)), _full((1, LANES)), _full((LANES, D_INNER))],
        out_specs=pl.BlockSpec((CHUNK, D_INNER), blk),
        out_shape=jax.ShapeDtypeStruct((t, D_INNER), BF16),
        scratch_shapes=[pltpu.VMEM((SSM_STATE, D_INNER), F32)],
        compiler_params=_params(("parallel", "arbitrary")),
        name="ssd_scan_bwd" if reverse else "ssd_scan_fwd",
    )(xbc_c, dt_raw, bias_row, alog_row, expand)


def _odd_out_kernel(x_ref, yf_ref, yb_ref, xs_ref, z_ref, d_ref, g_ref, w_ref, out_ref):
    y = yf_ref[...].astype(F32) + yb_ref[...].astype(F32) + d_ref[...] * xs_ref[...].astype(F32)
    y = y * _silu(z_ref[...].astype(F32))
    gw = D_INNER // SSM_GROUPS
    parts = []
    for g in range(SSM_GROUPS):
        yg = y[:, g * gw:(g + 1) * gw]
        parts.append(yg * lax.rsqrt(jnp.mean(yg * yg, axis=-1, keepdims=True) + RMS_EPS))
    yn = (jnp.concatenate(parts, axis=1) * g_ref[...]).astype(BF16)
    out_ref[...] = x_ref[...] + _dot(yn, w_ref[...])


def _odd_out(x2, yf, yb, xbc_c, z, dskip, gain, w, tm=256):
    t = x2.shape[0]
    row = lambda i: (i, 0)
    return pl.pallas_call(
        _odd_out_kernel,
        grid=(t // tm,),
        in_specs=[pl.BlockSpec((tm, D_MODEL), row), pl.BlockSpec((tm, D_INNER), row),
                  pl.BlockSpec((tm, D_INNER), row), pl.BlockSpec((tm, D_INNER), row),
                  pl.BlockSpec((tm, D_INNER), row), _full((1, D_INNER)), _full((1, D_INNER)),
                  _full((D_INNER, D_MODEL))],
        out_specs=pl.BlockSpec((tm, D_MODEL), row),
        out_shape=jax.ShapeDtypeStruct((t, D_MODEL), F32),
        compiler_params=_params(("parallel",)),
        name="odd_out",
    )(x2, yf, yb, xbc_c, z, dskip, gain, w)


def _odd_mixer(x2, bsz, seq, ln, w_in, conv_w, conv_b, dt_bias, a_log, d_skip, norm_g, w_out):
    pad = jnp.zeros((D_MODEL, LANES - 2 * SSM_HEADS), w_in.dtype)
    w = jnp.concatenate([w_in, pad], axis=1).astype(BF16)
    z, xbc, dt_raw = _odd_in(x2, ln[None, :], w)
    xbc_c = _conv(xbc, seq, conv_w.astype(F32), conv_b.astype(F32)[None, :])
    lane_pad = jnp.zeros((LANES - 2 * SSM_HEADS,), F32)
    bias_row = jnp.concatenate([dt_bias.astype(F32).reshape(-1), lane_pad])[None, :]
    alog_row = jnp.concatenate([a_log.astype(F32).reshape(-1), lane_pad])[None, :]
    ch = jnp.arange(D_INNER, dtype=jnp.int32) // SSM_HEAD_DIM
    hd = jnp.arange(LANES, dtype=jnp.int32)
    ys = []
    for reverse in (False, True):
        off = SSM_HEADS if reverse else 0
        expand = jnp.where(hd[:, None] == off + ch[None, :], 1.0, 0.0).astype(BF16)
        ys.append(_ssd(xbc_c, dt_raw, bias_row, alog_row, expand, bsz, seq, reverse))
    dskip = jnp.repeat(d_skip.astype(F32), SSM_HEAD_DIM)[None, :]
    return _odd_out(x2, ys[0], ys[1], xbc_c, z, dskip, norm_g.astype(F32)[None, :], w_out.astype(BF16))


ROUTE_LANES = N_EXPERT_GROUPS + N_EXPERTS


def _router_kernel(x_ref, g_ref, w_ref, b_ref, ls_ref, h_ref, meta_ref, cnt_ref, run_ref):
    i = pl.program_id(0)

    @pl.when(i == 0)
    def _():
        run_ref[...] = jnp.zeros_like(run_ref)

    h = _rms_rows(x_ref[...], g_ref[...])
    h_ref[...] = h
    hh = h.astype(BF16)
    hl = (h - hh.astype(F32)).astype(BF16)
    w_hi = w_ref[0]
    w_lo = w_ref[1]
    logits = _dot(hh, w_hi) + _dot(hl, w_hi) + _dot(hh, w_lo) + b_ref[...]
    tm = logits.shape[0]
    lane = lax.broadcasted_iota(jnp.int32, (tm, LANES), 1).astype(F32)

    def top1(v):
        m = jnp.max(v, axis=-1, keepdims=True)
        idx = jnp.min(jnp.where(v == m, lane, float(LANES)), axis=-1, keepdims=True)
        return m, idx

    gl = jnp.where(lane < N_EXPERT_GROUPS, logits, NEG_BIG)
    gmax, gidx = top1(gl)
    g_p = 1.0 / jnp.sum(jnp.exp(gl - gmax), axis=-1, keepdims=True)
    lo = N_EXPERT_GROUPS + EXPERTS_PER_GROUP * gidx
    e1 = jnp.where((lane >= lo) & (lane < lo + EXPERTS_PER_GROUP), logits, NEG_BIG)
    m1, i1 = top1(e1)
    e2 = jnp.where(lane == i1, NEG_BIG, e1)
    m2, i2 = top1(e2)
    r = jnp.exp(m2 - m1)
    w_a = g_p / (1.0 + r)
    w_b = g_p * r / (1.0 + r)
    e_a = i1 - N_EXPERT_GROUPS
    e_b = i2 - N_EXPERT_GROUPS

    onehot = jnp.where((lane == e_a) | (lane == e_b), 1.0, 0.0)
    before = _dot(ls_ref[...], onehot.astype(BF16)) + run_ref[...]
    rank_a = jnp.sum(jnp.where(lane == e_a, before, 0.0), axis=-1, keepdims=True)
    rank_b = jnp.sum(jnp.where(lane == e_b, before, 0.0), axis=-1, keepdims=True)
    run_ref[...] = run_ref[...] + jnp.sum(onehot, axis=0, keepdims=True)
    cnt_ref[...] = jnp.broadcast_to(run_ref[...], cnt_ref.shape)

    meta = jnp.where(lane == 0, e_a, 0.0)
    meta = jnp.where(lane == 1, e_b, meta)
    meta = jnp.where(lane == 2, rank_a, meta)
    meta = jnp.where(lane == 3, rank_b, meta)
    meta = jnp.where(lane == 4, w_a, meta)
    meta = jnp.where(lane == 5, w_b, meta)
    meta_ref[...] = meta


def _router(x2, ln, w2, b, lstrict, tm):
    t = x2.shape[0]
    row = lambda i: (i, 0)
    return pl.pallas_call(
        _router_kernel,
        grid=(t // tm,),
        in_specs=[pl.BlockSpec((tm, D_MODEL), row), _full((1, D_MODEL)), _full((2, D_MODEL, LANES)),
                  _full((1, LANES)), _full((tm, tm))],
        out_specs=[pl.BlockSpec((tm, D_MODEL), row), pl.BlockSpec((tm, LANES), row), _full((8, LANES))],
        out_shape=[jax.ShapeDtypeStruct((t, D_MODEL), F32), jax.ShapeDtypeStruct((t, LANES), F32),
                   jax.ShapeDtypeStruct((8, LANES), F32)],
        scratch_shapes=[pltpu.VMEM((1, LANES), F32)],
        compiler_params=_params(("arbitrary",)),
        name="moe_router",
    )(x2, ln, w2, b, lstrict)


def _row_copy(src_hbm, src_row, dst, dst_row, sem):
    return pltpu.make_async_copy(src_hbm.at[pl.ds(src_row, 1), :], dst.at[pl.ds(dst_row, 1), :], sem)


def _dispatch_kernel(start_ref, er_ref, h_ref, xin_in, xin_hbm, sem, *, tm):
    del xin_in

    def issue(tk, carry):
        for k in range(2):
            dst = start_ref[er_ref[0, 0, 4 * tk + k]] + er_ref[0, 0, 4 * tk + 2 + k]
            _row_copy(h_ref, tk, xin_hbm, dst, sem).start()
        return carry

    lax.fori_loop(0, tm, issue, 0, unroll=8)

    def drain(tk, carry):
        for k in range(2):
            _row_copy(h_ref, 0, xin_hbm, 0, sem).wait()
        return carry

    lax.fori_loop(0, tm, drain, 0, unroll=8)


def _dispatch(start, er, h, xin0, tm):
    t = h.shape[0]
    return pl.pallas_call(
        functools.partial(_dispatch_kernel, tm=tm),
        grid_spec=pltpu.PrefetchScalarGridSpec(
            num_scalar_prefetch=1,
            grid=(t // tm,),
            in_specs=[pl.BlockSpec((1, 1, 4 * tm), lambda i, s: (i, 0, 0), memory_space=pltpu.SMEM),
                      pl.BlockSpec((tm, D_MODEL), lambda i, s: (i, 0)), pl.BlockSpec(memory_space=pl.ANY)],
            out_specs=pl.BlockSpec(memory_space=pl.ANY),
            scratch_shapes=[pltpu.SemaphoreType.DMA]),
        out_shape=jax.ShapeDtypeStruct(xin0.shape, xin0.dtype),
        input_output_aliases={3: 0},
        compiler_params=_params(("arbitrary",)),
        name="moe_dispatch",
    )(start, er, h, xin0)


def _experts_kernel(be_ref, nused_ref, x_ref, w1_ref, w3_ref, w2_ref, o_ref):
    i = pl.program_id(0)

    @pl.when(i < nused_ref[0])
    def _():
        x = x_ref[...].astype(BF16)
        a = _silu(_dot(x, w1_ref[0])) * _dot(x, w3_ref[0])
        o_ref[...] = _dot(a.astype(BF16), w2_ref[0])

    @pl.when(i >= nused_ref[0])
    def _():
        o_ref[...] = jnp.zeros_like(o_ref)


def _experts(block_e, nused, xin, w1, w3, w2):
    nb = xin.shape[0] // MOE_BLOCK
    row = lambda i, be, nu: (i, 0)
    wsel = lambda i, be, nu: (be[i], 0, 0)
    return pl.pallas_call(
        _experts_kernel,
        grid_spec=pltpu.PrefetchScalarGridSpec(
            num_scalar_prefetch=2,
            grid=(nb,),
            in_specs=[pl.BlockSpec((MOE_BLOCK, D_MODEL), row),
                      pl.BlockSpec((1, D_MODEL, D_EXPERT), wsel), pl.BlockSpec((1, D_MODEL, D_EXPERT), wsel),
                      pl.BlockSpec((1, D_EXPERT, D_MODEL), wsel)],
            out_specs=pl.BlockSpec((MOE_BLOCK, D_MODEL), row)),
        out_shape=jax.ShapeDtypeStruct(xin.shape, F32),
        compiler_params=_params(("parallel",)),
        name="moe_experts",
    )(block_e, nused, xin, w1, w3, w2)


def _combine_ple_kernel(start_ref, er_ref, x_ref, meta_ref, p_ref, g_ref, wg_ref, wp_ref, y_hbm,
                        out_ref, ya_ref, yb_ref, sem, *, tm):
    def issue(tk, carry):
        for k, buf in enumerate((ya_ref, yb_ref)):
            src = start_ref[er_ref[0, 0, 4 * tk + k]] + er_ref[0, 0, 4 * tk + 2 + k]
            _row_copy(y_hbm, src, buf, tk, sem).start()
        return carry

    lax.fori_loop(0, tm, issue, 0, unroll=8)

    def drain(tk, carry):
        for buf in (ya_ref, yb_ref):
            _row_copy(y_hbm, 0, buf, 0, sem).wait()
        return carry

    lax.fori_loop(0, tm, drain, 0, unroll=8)

    meta = meta_ref[...]
    x = x_ref[...] + (meta[:, 4:5] * ya_ref[...] + meta[:, 5:6] * yb_ref[...])
    hn = _rms_rows(x, g_ref[...]).astype(BF16)
    gate = 1.0 / (1.0 + jnp.exp(-_dot(hn, wg_ref[...])))
    out_ref[...] = x + _dot(p_ref[...].astype(BF16), wp_ref[...]) * gate


def _combine_ple(start, er, x2, meta, p2, ln, wg, wp, yout, tm):
    t = x2.shape[0]
    row = lambda i, s: (i, 0)
    full = lambda shape: pl.BlockSpec(shape, lambda i, s: (0,) * len(shape))
    return pl.pallas_call(
        functools.partial(_combine_ple_kernel, tm=tm),
        grid_spec=pltpu.PrefetchScalarGridSpec(
            num_scalar_prefetch=1,
            grid=(t // tm,),
            in_specs=[pl.BlockSpec((1, 1, 4 * tm), lambda i, s: (i, 0, 0), memory_space=pltpu.SMEM),
                      pl.BlockSpec((tm, D_MODEL), row), pl.BlockSpec((tm, LANES), row),
                      pl.BlockSpec((tm, PLE_DIM), row), full((1, D_MODEL)), full((D_MODEL, D_MODEL)),
                      full((PLE_DIM, D_MODEL)), pl.BlockSpec(memory_space=pl.ANY)],
            out_specs=pl.BlockSpec((tm, D_MODEL), row),
            scratch_shapes=[pltpu.VMEM((tm, D_MODEL), F32), pltpu.VMEM((tm, D_MODEL), F32),
                            pltpu.SemaphoreType.DMA]),
        out_shape=jax.ShapeDtypeStruct((t, D_MODEL), F32),
        compiler_params=_params(("arbitrary",)),
        name="moe_combine_ple",
    )(start, er, x2, meta, p2, ln, wg, wp, yout)


MOE_TM = 256


def _moe_ple(x2, p2, ln_ffn, w_rg, b_rg, w_re, b_re, w1, w3, w2, ln_ple, w_gate, w_proj):
    t = x2.shape[0]
    tm = MOE_TM
    wr = jnp.concatenate([w_rg, w_re, jnp.zeros((D_MODEL, LANES - ROUTE_LANES), w_rg.dtype)], axis=1).astype(F32)
    wr_hi = wr.astype(BF16)
    wr_lo = (wr - wr_hi.astype(F32)).astype(BF16)
    br = jnp.concatenate([b_rg, b_re, jnp.zeros((LANES - ROUTE_LANES,), b_rg.dtype)]).astype(F32)[None, :]
    ii = jnp.arange(tm, dtype=jnp.int32)
    lstrict = jnp.where(ii[:, None] > ii[None, :], 1.0, 0.0).astype(BF16)
    h, meta, cnt = _router(x2, ln_ffn[None, :], jnp.stack([wr_hi, wr_lo]), br, lstrict, tm)

    counts = cnt[0, :N_EXPERTS].astype(jnp.int32)
    padded = (counts + MOE_BLOCK - 1) // MOE_BLOCK * MOE_BLOCK
    pad_end = jnp.cumsum(padded)
    start = (pad_end - padded).astype(jnp.int32)
    nb = (t * 2) // MOE_BLOCK + N_EXPERTS
    first_row = jnp.arange(nb, dtype=jnp.int32) * MOE_BLOCK
    block_e = jnp.minimum(jnp.sum((pad_end[None, :] <= first_row[:, None]).astype(jnp.int32), axis=1),
                          N_EXPERTS - 1).astype(jnp.int32)
    nused = (pad_end[-1:] // MOE_BLOCK).astype(jnp.int32)
    er = meta[:, :4].astype(jnp.int32).reshape(t // tm, 1, 4 * tm)

    xin = _dispatch(start, er, h, jnp.zeros((nb * MOE_BLOCK, D_MODEL), F32), tm)
    yout = _experts(block_e, nused, xin, w1.astype(BF16), w3.astype(BF16), w2.astype(BF16))
    return _combine_ple(start, er, x2, meta, p2, ln_ple[None, :], w_gate.astype(BF16), w_proj.astype(BF16),
                        yout, tm)


def _trunk(x, p, prm):
    bsz, seq, _ = x.shape
    x2 = x.reshape(bsz * seq, D_MODEL)
    for i in range(DEPTH):
        j = i // 2
        if i % 2 == 0:
            x2 = _even_mixer(x2, bsz, seq, prm["ln_mix_e"][j], prm["w_in_e"][j], prm["q_gain"][j],
                             prm["k_gain"][j], prm["sink"][j], prm["w_out_e"][j])
        else:
            x2 = _odd_mixer(x2, bsz, seq, prm["ln_mix_o"][j], prm["w_in_o"][j], prm["conv_w"][j],
                            prm["conv_b"][j], prm["dt_bias"][j], prm["a_log"][j], prm["d_skip"][j],
                            prm["ssm_gain"][j], prm["w_out_o"][j])
        x2 = _moe_ple(x2, p[i].reshape(bsz * seq, PLE_DIM), prm["ln_ffn"][i], prm["w_router_g"][i],
                      prm["b_router_g"][i], prm["w_router_e"][i], prm["b_router_e"][i], prm["w1"][i],
                      prm["w3"][i], prm["w2"][i], prm["ln_ple"][i], prm["w_ple_gate"][i], prm["w_ple_proj"][i])
    return x2.reshape(bsz, seq, D_MODEL)


def kernel(x_prompt, x_sample, p_prompt, p_sample, ln_mix_e, w_in_e, q_gain, k_gain, sink, w_out_e, ln_mix_o, w_in_o, conv_w, conv_b, dt_bias, a_log, d_skip, ssm_gain, w_out_o, ln_ffn, w_router_g, b_router_g, w_router_e, b_router_e, w1, w3, w2, ln_ple, w_ple_gate, w_ple_proj):
    prm = dict(ln_mix_e=ln_mix_e, w_in_e=w_in_e, q_gain=q_gain, k_gain=k_gain, sink=sink,
               w_out_e=w_out_e, ln_mix_o=ln_mix_o, w_in_o=w_in_o, conv_w=conv_w, conv_b=conv_b,
               dt_bias=dt_bias, a_log=a_log, d_skip=d_skip, ssm_gain=ssm_gain, w_out_o=w_out_o,
               ln_ffn=ln_ffn, w_router_g=w_router_g, b_router_g=b_router_g,
               w_router_e=w_router_e, b_router_e=b_router_e, w1=w1, w3=w3, w2=w2,
               ln_ple=ln_ple, w_ple_gate=w_ple_gate, w_ple_proj=w_ple_proj)
    return (_trunk(x_prompt, p_prompt, prm), _trunk(x_sample, p_sample, prm))
```

```python
import functools
import math

import jax
import jax.numpy as jnp
from jax import lax
from jax.experimental import pallas as pl
from jax.experimental.pallas import tpu as pltpu

D_MODEL = 1024
DEPTH = 4
RMS_EPS = 1e-6
N_FOURIER_GROUPS = 4
FOURIER_WIDTH = 512
FOURIER_GROUP = 128
FFT_N2 = 128
HEAD_DIM = 64
N_Q_HEADS = 8
N_KV_HEADS = 2
ATTN_WIDTH = 512
KV_WIDTH = 128
ATTN_BLOCK = 128
ROPE_THETA = 10000.0
D_INNER = 2048
SSM_HEAD_DIM = 64
SSM_HEADS = 32
SSM_STATE = 128
SSM_GROUPS = 4
HEADS_PER_GROUP = 8
CONV_K = 5
CHUNK = 128
GN = SSM_GROUPS * SSM_STATE
CONV_DIM = D_INNER + 2 * GN
N_EXPERT_GROUPS = 4
EXPERTS_PER_GROUP = 8
N_EXPERTS = 32
D_EXPERT = 512
MOE_BLOCK = 256
PLE_DIM = 256

LANES = 128
NEG_BIG = -1e30
VMEM_LIMIT = 52 * 1024 * 1024

F32 = jnp.float32
BF16 = jnp.bfloat16


def _params(sem, vmem=VMEM_LIMIT):
    return pltpu.CompilerParams(dimension_semantics=sem, vmem_limit_bytes=vmem)


def _dot(a, b):
    return jnp.dot(a, b, preferred_element_type=F32)


def _dot_nt(a, b):
    return lax.dot_general(a, b, (((1,), (1,)), ((), ())), preferred_element_type=F32)


def _split3(v):
    hi = v.astype(BF16)
    r1 = v - hi.astype(F32)
    mid = r1.astype(BF16)
    lo = (r1 - mid.astype(F32)).astype(BF16)
    return hi, mid, lo


def _dot3(v, m_bf16):
    hi, mid, lo = _split3(v)
    return _dot(hi, m_bf16) + _dot(mid, m_bf16) + _dot(lo, m_bf16)


def _dot3_left(m_bf16, v):
    hi, mid, lo = _split3(v)
    return _dot(m_bf16, hi) + _dot(m_bf16, mid) + _dot(m_bf16, lo)


def _rms_rows(x, g):
    return x * lax.rsqrt(jnp.mean(x * x, axis=-1, keepdims=True) + RMS_EPS) * g


def _silu(x):
    return x * (1.0 / (1.0 + jnp.exp(-x)))


def _full(shape):
    return pl.BlockSpec(shape, lambda *_: (0,) * len(shape))


def _even_in_kernel(x_ref, g_ref, w_ref, dft_ref, bd_ref, qg_ref, kg_ref, cos_ref, sin_ref,
                    gout_ref, q_ref, k_ref, v_ref):
    h = _rms_rows(x_ref[...], g_ref[...]).astype(BF16)
    u = _dot(h, w_ref[...])
    a = u[:, :FOURIER_WIDTH].astype(BF16)
    for g in range(N_FOURIER_GROUPS):
        r = _dot(a[:, g * LANES:(g + 1) * LANES], dft_ref[...])
        gout_ref[:, g * LANES:(g + 1) * LANES] = r[:, :LANES].astype(BF16)
        gout_ref[:, FOURIER_WIDTH + g * LANES:FOURIER_WIDTH + (g + 1) * LANES] = r[:, LANES:].astype(BF16)

    cos = cos_ref[...]
    sin = sin_ref[...]

    def norm_rope(t, gain, width):
        n = width // LANES
        bd = bd_ref[:width, :width]
        t2 = t * t
        hi = t2.astype(BF16)
        lo = (t2 - hi.astype(F32)).astype(BF16)
        ms = _dot(hi, bd) + _dot(lo, bd)
        tn = t * lax.rsqrt(ms + RMS_EPS) * gain
        lane = lax.broadcasted_iota(jnp.int32, tn.shape, 1)
        first_half = (lane % HEAD_DIM) < (HEAD_DIM // 2)
        rot = jnp.where(first_half, pltpu.roll(tn, width - HEAD_DIM // 2, 1), pltpu.roll(tn, HEAD_DIM // 2, 1))
        c = jnp.concatenate([cos] * n, axis=1)
        s = jnp.concatenate([sin] * n, axis=1)
        return tn * c + rot * s

    q = norm_rope(u[:, 512:1024], qg_ref[...], ATTN_WIDTH) * (HEAD_DIM ** -0.5)
    k = norm_rope(u[:, 1024:1280], kg_ref[...], 2 * KV_WIDTH)
    q_ref[...] = q.astype(BF16)
    k_ref[...] = k.astype(BF16)
    v_ref[...] = u[:, 1280:1536].astype(BF16)


def _even_in(x2, seq, ln, w, dft, bd, qg, kg, cos_t, sin_t, tm=512):
    t = x2.shape[0]
    nseq = seq // tm
    row = lambda i: (i, 0)
    pos = lambda i: (i % nseq, 0)
    return pl.pallas_call(
        _even_in_kernel,
        grid=(t // tm,),
        in_specs=[pl.BlockSpec((tm, D_MODEL), row), _full((1, D_MODEL)), _full((D_MODEL, 1536)),
                  _full((LANES, 2 * LANES)), _full((512, 512)), _full((1, 512)), _full((1, 256)),
                  pl.BlockSpec((tm, LANES), pos), pl.BlockSpec((tm, LANES), pos)],
        out_specs=[pl.BlockSpec((tm, 1024), row), pl.BlockSpec((tm, 512), row),
                   pl.BlockSpec((tm, 256), row), pl.BlockSpec((tm, 256), row)],
        out_shape=[jax.ShapeDtypeStruct((t, 1024), BF16), jax.ShapeDtypeStruct((t, 512), BF16),
                   jax.ShapeDtypeStruct((t, 256), BF16), jax.ShapeDtypeStruct((t, 256), BF16)],
        compiler_params=_params(("parallel",)),
        name="even_in",
    )(x2, ln, w, dft, bd, qg, kg, cos_t, sin_t)


def _fft1_kernel(g_ref, c_ref, s_ref, y_ref):
    g = g_ref[0]
    p = _dot(c_ref[...], g)
    q = _dot(s_ref[...], g)
    for j in range(g.shape[1] // 1024):
        o = j * 1024
        y_ref[0, :, o:o + 512] = (p[:, o:o + 512] + q[:, o + 512:o + 1024]).astype(BF16)
        y_ref[0, :, o + 512:o + 1024] = (p[:, o + 512:o + 1024] - q[:, o:o + 512]).astype(BF16)


def _fft1(g3, c1, s1, cb=8192):
    b, n1, width = g3.shape
    blk = lambda i, j: (i, 0, j)
    return pl.pallas_call(
        _fft1_kernel,
        grid=(b, width // cb),
        in_specs=[pl.BlockSpec((1, n1, cb), blk), _full((n1, n1)), _full((n1, n1))],
        out_specs=pl.BlockSpec((1, n1, cb), blk),
        out_shape=jax.ShapeDtypeStruct(g3.shape, BF16),
        compiler_params=_params(("parallel", "parallel")),
        name="fft_stage1",
    )(g3, c1, s1)


def _fft2_kernel(y_ref, mc_ref, ms_ref, o_ref, *, scale):
    for j in range(y_ref.shape[1]):
        yr = y_ref[0, j, :, :512]
        yi = y_ref[0, j, :, 512:]
        r = _dot(mc_ref[j], yr) + _dot(ms_ref[j], yi)
        o_ref[0, :, j * 512:(j + 1) * 512] = (r * scale).astype(BF16)


def _fft2(y4, mc, ms, scale, kb=8):
    b, n1, n2, _ = y4.shape
    return pl.pallas_call(
        functools.partial(_fft2_kernel, scale=scale),
        grid=(n1 // kb, b),
        in_specs=[pl.BlockSpec((1, kb, n2, 1024), lambda k, i: (i, k, 0, 0)),
                  pl.BlockSpec((kb, n2, n2), lambda k, i: (k, 0, 0)),
                  pl.BlockSpec((kb, n2, n2), lambda k, i: (k, 0, 0))],
        out_specs=pl.BlockSpec((1, n2, kb * 512), lambda k, i: (i, 0, k)),
        out_shape=jax.ShapeDtypeStruct((b, n2, n1 * 512), BF16),
        compiler_params=_params(("parallel", "parallel")),
        name="fft_stage2",
    )(y4, mc, ms)


def _dft_tables(seq):
    n1, n2 = seq // FFT_N2, FFT_N2
    i1 = jnp.arange(n1, dtype=jnp.int32)
    ang1 = ((i1[:, None] * i1[None, :]) % n1).astype(F32) * (2.0 * math.pi / n1)
    c1, s1 = jnp.cos(ang1).astype(BF16), jnp.sin(ang1).astype(BF16)
    i2 = jnp.arange(n2, dtype=jnp.int32)
    ph = (i2[None, None, :] * i1[:, None, None] + n1 * i2[None, :, None] * i2[None, None, :]) % seq
    ang2 = ph.astype(F32) * (2.0 * math.pi / seq)
    mc, ms = jnp.cos(ang2).astype(BF16), jnp.sin(ang2).astype(BF16)
    ic = jnp.arange(FOURIER_GROUP, dtype=jnp.int32)
    angc = ((ic[:, None] * ic[None, :]) % FOURIER_GROUP).astype(F32) * (2.0 * math.pi / FOURIER_GROUP)
    dftc = jnp.concatenate([jnp.cos(angc), -jnp.sin(angc)], axis=1).astype(BF16)
    return c1, s1, mc, ms, dftc


def _attn_kernel(sink_ref, q_ref, kp_ref, ko_ref, kn_ref, vp_ref, vo_ref, vn_ref, o_ref, kc_ref, vc_ref):
    i = pl.program_id(1)
    last = pl.num_programs(1) - 1
    tq = q_ref.shape[0]
    nb = tq // ATTN_BLOCK
    kc_ref[0:ATTN_BLOCK] = kp_ref[...]
    kc_ref[ATTN_BLOCK:ATTN_BLOCK + tq] = ko_ref[...]
    kc_ref[ATTN_BLOCK + tq:] = kn_ref[...]
    vc_ref[0:ATTN_BLOCK] = vp_ref[...]
    vc_ref[ATTN_BLOCK:ATTN_BLOCK + tq] = vo_ref[...]
    vc_ref[ATTN_BLOCK + tq:] = vn_ref[...]

    w3 = 3 * ATTN_BLOCK
    qpos = lax.broadcasted_iota(jnp.int32, (ATTN_BLOCK, w3), 0) + ATTN_BLOCK
    kpos = lax.broadcasted_iota(jnp.int32, (ATTN_BLOCK, w3), 1)
    band = jnp.abs(kpos - qpos) <= ATTN_BLOCK
    lane = lax.broadcasted_iota(jnp.int32, (w3, LANES), 1)
    lo_half = lane < HEAD_DIM
    out_lane = lax.broadcasted_iota(jnp.int32, (ATTN_BLOCK, LANES), 1) < HEAD_DIM

    for jb in range(nb):
        ok = band
        if jb == 0:
            ok = ok & ((kpos >= ATTN_BLOCK) | (i > 0))
        if jb == nb - 1:
            ok = ok & ((kpos < 2 * ATTN_BLOCK) | (i < last))
        bias = jnp.where(ok, 0.0, NEG_BIG).astype(F32)
        qb = q_ref[jb * ATTN_BLOCK:(jb + 1) * ATTN_BLOCK, :]
        kw = kc_ref[jb * ATTN_BLOCK:jb * ATTN_BLOCK + w3, :]
        vw = vc_ref[jb * ATTN_BLOCK:jb * ATTN_BLOCK + w3, :]
        for g in range(N_KV_HEADS):
            kg = kw[:, g * LANES:(g + 1) * LANES]
            vg = vw[:, g * LANES:(g + 1) * LANES]
            zero = jnp.zeros_like(kg)
            kbd = jnp.concatenate([jnp.where(lo_half, kg, zero), jnp.where(lo_half, zero, kg)], axis=0)
            vbd = jnp.concatenate([jnp.where(lo_half, vg, zero), jnp.where(lo_half, zero, vg)], axis=0)
            for pr in range(2):
                col = (2 * g + pr) * LANES
                s = _dot_nt(qb[:, col:col + LANES], kbd)
                es, rs = [], []
                for half in range(2):
                    sk = sink_ref[4 * g + 2 * pr + half]
                    sh = s[:, half * w3:(half + 1) * w3] + bias
                    m = jnp.maximum(jnp.max(sh, axis=-1, keepdims=True), sk)
                    e = jnp.exp(sh - m)
                    den = jnp.sum(e, axis=-1, keepdims=True) + jnp.exp(sk - m)
                    es.append(e.astype(BF16))
                    rs.append(1.0 / den)
                o = _dot(jnp.concatenate(es, axis=1), vbd)
                o = o * jnp.where(out_lane, rs[0], rs[1])
                o_ref[jb * ATTN_BLOCK:(jb + 1) * ATTN_BLOCK, col:col + LANES] = o.astype(BF16)


def _attention(q, k, v, sink, bsz, seq, tq=512):
    t = q.shape[0]
    nq = seq // tq
    r = tq // ATTN_BLOCK
    nblk = t // ATTN_BLOCK
    own = lambda b, i, s: (b * nq + i, 0)
    prev = lambda b, i, s: (jnp.maximum((b * nq + i) * r - 1, 0), 0)
    nxt = lambda b, i, s: (jnp.minimum((b * nq + i + 1) * r, nblk - 1), 0)
    kv = lambda m: pl.BlockSpec((ATTN_BLOCK if m is not own else tq, 256), m)
    return pl.pallas_call(
        _attn_kernel,
        grid_spec=pltpu.PrefetchScalarGridSpec(
            num_scalar_prefetch=1,
            grid=(bsz, nq),
            in_specs=[pl.BlockSpec((tq, 512), own), kv(prev), kv(own), kv(nxt), kv(prev), kv(own), kv(nxt)],
            out_specs=pl.BlockSpec((tq, 512), own),
            scratch_shapes=[pltpu.VMEM((tq + 2 * ATTN_BLOCK, 256), BF16),
                            pltpu.VMEM((tq + 2 * ATTN_BLOCK, 256), BF16)]),
        out_shape=jax.ShapeDtypeStruct((t, 512), BF16),
        compiler_params=_params(("parallel", "parallel")),
        name="window_attention",
    )(sink, q, k, k, k, v, v, v)


def _even_out_kernel(x_ref, f_ref, o_ref, w_ref, out_ref):
    out_ref[...] = (x_ref[...] + _dot(f_ref[...], w_ref[:512, :]) + _dot(o_ref[...], w_ref[512:, :]))


def _even_out(x2, f, o, w, tm=512):
    t = x2.shape[0]
    row = lambda i: (i, 0)
    return pl.pallas_call(
        _even_out_kernel,
        grid=(t // tm,),
        in_specs=[pl.BlockSpec((tm, D_MODEL), row), pl.BlockSpec((tm, 512), row),
                  pl.BlockSpec((tm, 512), row), _full((1024, D_MODEL))],
        out_specs=pl.BlockSpec((tm, D_MODEL), row),
        out_shape=jax.ShapeDtypeStruct((t, D_MODEL), F32),
        compiler_params=_params(("parallel",)),
        name="even_out",
    )(x2, f, o, w)


def _even_mixer(x2, bsz, seq, ln, w_in, q_gain, k_gain, sink, w_out):
    c1, s1, mc, ms, dftc = _dft_tables(seq)
    n1 = seq // FFT_N2
    kcols = w_in[:, 1024:1152]
    vcols = w_in[:, 1152:1280]
    dup = lambda c: jnp.concatenate([c[:, :64], c[:, :64], c[:, 64:], c[:, 64:]], axis=1)
    w = jnp.concatenate([w_in[:, :1024], dup(kcols), dup(vcols)], axis=1).astype(BF16)
    hid = jnp.arange(512, dtype=jnp.int32) // HEAD_DIM
    bd = jnp.where(hid[:, None] == hid[None, :], 1.0 / HEAD_DIM, 0.0).astype(BF16)
    inv = 1.0 / (ROPE_THETA ** (jnp.arange(0, HEAD_DIM, 2, dtype=F32) / HEAD_DIM))
    ang = jnp.arange(seq, dtype=F32)[:, None] * inv[None, :]
    cos, sin = jnp.cos(ang), jnp.sin(ang)
    cos_t = jnp.concatenate([cos, cos, cos, cos], axis=1)
    sin_t = jnp.concatenate([-sin, sin, -sin, sin], axis=1)
    qg = jnp.tile(q_gain.astype(F32), N_Q_HEADS)[None, :]
    kg = jnp.tile(k_gain.astype(F32), 2 * N_KV_HEADS)[None, :]
    g, q, k, v = _even_in(x2, seq, ln[None, :], w, dftc, bd, qg, kg, cos_t, sin_t)
    y = _fft1(g.reshape(bsz, n1, FFT_N2 * 1024), c1, s1)
    scale = 1.0 / math.sqrt(seq * FOURIER_GROUP)
    f = _fft2(y.reshape(bsz, n1, FFT_N2, 1024), mc, ms, scale).reshape(bsz * seq, 512)
    o = _attention(q, k, v, sink.astype(F32), bsz, seq)
    return _even_out(x2, f, o, w_out.astype(BF16))


def _odd_in_kernel(x_ref, g_ref, w_ref, z_ref, xbc_ref, dt_ref):
    h = _rms_rows(x_ref[...], g_ref[...]).astype(BF16)
    z_ref[...] = _dot(h, w_ref[:, :D_INNER]).astype(BF16)
    xbc_ref[...] = _dot(h, w_ref[:, D_INNER:D_INNER + CONV_DIM]).astype(BF16)
    dt_ref[...] = _dot(h, w_ref[:, D_INNER + CONV_DIM:])


def _odd_in(x2, ln, w, tm=256):
    t = x2.shape[0]
    row = lambda i: (i, 0)
    wcols = D_INNER + CONV_DIM + LANES
    return pl.pallas_call(
        _odd_in_kernel,
        grid=(t // tm,),
        in_specs=[pl.BlockSpec((tm, D_MODEL), row), _full((1, D_MODEL)), _full((D_MODEL, wcols))],
        out_specs=[pl.BlockSpec((tm, D_INNER), row), pl.BlockSpec((tm, CONV_DIM), row),
                   pl.BlockSpec((tm, LANES), row)],
        out_shape=[jax.ShapeDtypeStruct((t, D_INNER), BF16), jax.ShapeDtypeStruct((t, CONV_DIM), BF16),
                   jax.ShapeDtypeStruct((t, LANES), F32)],
        compiler_params=_params(("parallel",)),
        name="odd_in",
    )(x2, ln, w)


CONV_HALO = 16


def _conv_kernel(prev_ref, main_ref, next_ref, w_ref, b_ref, o_ref, win_ref, *, tiles_per_seq):
    i = pl.program_id(0) % tiles_per_seq
    tc = main_ref.shape[0]
    keep_prev = (i > 0).astype(F32)
    keep_next = (i < tiles_per_seq - 1).astype(F32)
    win_ref[0:CONV_HALO, :] = prev_ref[...].astype(F32) * keep_prev
    win_ref[CONV_HALO:CONV_HALO + tc, :] = main_ref[...].astype(F32)
    win_ref[CONV_HALO + tc:, :] = next_ref[...].astype(F32) * keep_next
    cb = 512
    for c in range(CONV_DIM // cb):
        cols = slice(c * cb, (c + 1) * cb)
        acc = jnp.broadcast_to(b_ref[:, cols], (tc, cb))
        for k in range(CONV_K):
            start = CONV_HALO - CONV_K // 2 + k
            acc = acc + win_ref[start:start + tc, cols] * w_ref[k:k + 1, cols]
        o_ref[:, cols] = _silu(acc).astype(BF16)


def _conv(xbc, seq, w, b, tc=512):
    t = xbc.shape[0]
    r = tc // CONV_HALO
    nh = t // CONV_HALO
    return pl.pallas_call(
        functools.partial(_conv_kernel, tiles_per_seq=seq // tc),
        grid=(t // tc,),
        in_specs=[pl.BlockSpec((CONV_HALO, CONV_DIM), lambda i: (jnp.maximum(i * r - 1, 0), 0)),
                  pl.BlockSpec((tc, CONV_DIM), lambda i: (i, 0)),
                  pl.BlockSpec((CONV_HALO, CONV_DIM), lambda i: (jnp.minimum((i + 1) * r, nh - 1), 0)),
                  _full((CONV_K, CONV_DIM)), _full((1, CONV_DIM))],
        out_specs=pl.BlockSpec((tc, CONV_DIM), lambda i: (i, 0)),
        out_shape=jax.ShapeDtypeStruct((t, CONV_DIM), BF16),
        scratch_shapes=[pltpu.VMEM((tc + 2 * CONV_HALO, CONV_DIM), F32)],
        compiler_params=_params(("parallel",)),
        name="ssd_conv",
    )(xbc, xbc, xbc, w, b)


N_PAIRS = SSM_HEADS // 2
PAIRS_PER_GROUP = HEADS_PER_GROUP // 2


LOG2E = math.log2(math.e)
SSD_STEP = 2 * CHUNK


def _ssd_chunk(xbc_ref, dt_ref, bias, a_row, expand, y_ref, state_ref, r0, reverse, mask, tri_col, lo_half):
    off = SSM_HEADS if reverse else 0
    rows = slice(r0, r0 + CHUNK)
    dt_lh = dt_ref[rows, :] + bias
    dt_lh = jnp.maximum(dt_lh, 0.0) + jnp.log(1.0 + jnp.exp(-jnp.abs(dt_lh)))
    acs_lh = _dot3_left(tri_col, dt_lh * a_row)
    edge = 0 if reverse else CHUNK - 1
    acs2_lh = acs_lh * LOG2E
    tot2_row = acs2_lh[edge:edge + 1, :]
    r_hl = acs2_lh.T
    dt_hi = dt_lh.astype(BF16)
    dt_mid = (dt_lh - dt_hi.astype(F32)).astype(BF16)
    dt_exp = _dot(dt_hi, expand) + _dot(dt_mid, expand)
    lo_row = lo_half[0:1, :]

    for g in range(SSM_GROUPS):
        b_g = xbc_ref[rows, D_INNER + g * SSM_STATE:D_INNER + (g + 1) * SSM_STATE]
        c_g = xbc_ref[rows, D_INNER + GN + g * SSM_STATE:D_INNER + GN + (g + 1) * SSM_STATE]
        cb = _dot_nt(c_g, b_g)
        bt = b_g.astype(F32).T.astype(BF16)
        gcols = slice(g * PAIRS_PER_GROUP * LANES, (g + 1) * PAIRS_PER_GROUP * LANES)
        y_off = _dot(c_g, state_ref[:, gcols].astype(BF16))
        for pp in range(PAIRS_PER_GROUP):
            p = g * PAIRS_PER_GROUP + pp
            hd0, hd1 = off + 2 * p, off + 2 * p + 1
            colb = [jnp.broadcast_to(acs2_lh[:, hd:hd + 1], (CHUNK, CHUNK)) for hd in (hd0, hd1)]
            ms = [(cb * jnp.exp2(jnp.where(mask, cb_l - r_hl[hd:hd + 1, :], NEG_BIG))).astype(BF16)
                  for cb_l, hd in zip(colb, (hd0, hd1))]
            pcols = slice(p * LANES, (p + 1) * LANES)
            xdt = xbc_ref[rows, pcols].astype(F32) * dt_exp[:, pcols]
            xdt_b = xdt.astype(BF16)
            zx = jnp.zeros_like(xdt_b)
            xbd = jnp.concatenate([jnp.where(lo_half, xdt_b, zx), jnp.where(lo_half, zx, xdt_b)], axis=0)
            colsel = jnp.where(lo_half, colb[0], colb[1])
            y = _dot(jnp.concatenate(ms, axis=1), xbd)
            y = y + jnp.exp2(colsel) * y_off[:, pp * LANES:(pp + 1) * LANES]
            y_ref[rows, pcols] = y.astype(BF16)
            tot2 = jnp.where(lo_row, tot2_row[:, hd0:hd0 + 1], tot2_row[:, hd1:hd1 + 1])
            snew = _dot(bt, (xdt * jnp.exp2(tot2 - colsel)).astype(BF16))
            state_ref[:, pcols] = state_ref[:, pcols] * jnp.exp2(tot2) + snew


def _ssd_kernel(xbc_ref, dt_ref, bias_ref, alog_ref, exp_ref, y_ref, state_ref, *, reverse):
    @pl.when(pl.program_id(1) == 0)
    def _():
        state_ref[...] = jnp.zeros_like(state_ref)

    r_io = lax.broadcasted_iota(jnp.int32, (CHUNK, CHUNK), 0)
    c_io = lax.broadcasted_iota(jnp.int32, (CHUNK, CHUNK), 1)
    mask = (r_io <= c_io) if reverse else (r_io >= c_io)
    tri_col = jnp.where(mask, 1.0, 0.0).astype(BF16)
    lo_half = lax.broadcasted_iota(jnp.int32, (CHUNK, LANES), 1) < SSM_HEAD_DIM
    bias = bias_ref[...]
    a_row = -jnp.exp(alog_ref[...])
    n_sub = SSD_STEP // CHUNK
    order = range(n_sub - 1, -1, -1) if reverse else range(n_sub)
    for sub in order:
        _ssd_chunk(xbc_ref, dt_ref, bias, a_row, exp_ref[...], y_ref, state_ref, sub * CHUNK, reverse,
                   mask, tri_col, lo_half)


def _ssd(xbc_c, dt_raw, bias_row, alog_row, expand, bsz, seq, reverse):
    t = xbc_c.shape[0]
    ns = seq // SSD_STEP
    if reverse:
        blk = lambda b, c: (b * ns + ns - 1 - c, 0)
    else:
        blk = lambda b, c: (b * ns + c, 0)
    return pl.pallas_call(
        functools.partial(_ssd_kernel, reverse=reverse),
        grid=(bsz, ns),
        in_specs=[pl.BlockSpec((SSD_STEP, CONV_DIM), blk), pl.BlockSpec((SSD_STEP, LANES), blk),
                  _full((1, LANES)), _full((1, LANES)), _full((LANES, D_INNER))],
        out_specs=pl.BlockSpec((SSD_STEP, D_INNER), blk),
        out_shape=jax.ShapeDtypeStruct((t, D_INNER), BF16),
        scratch_shapes=[pltpu.VMEM((SSM_STATE, D_INNER), F32)],
        compiler_params=_params(("parallel", "arbitrary")),
        name="ssd_scan_bwd" if reverse else "ssd_scan_fwd",
    )(xbc_c, dt_raw, bias_row, alog_row, expand)


def _odd_out_kernel(x_ref, yf_ref, yb_ref, xs_ref, z_ref, d_ref, g_ref, w_ref, out_ref):
    y = yf_ref[...].astype(F32) + yb_ref[...].astype(F32) + d_ref[...] * xs_ref[...].astype(F32)
    y = y * _silu(z_ref[...].astype(F32))
    gw = D_INNER // SSM_GROUPS
    parts = []
    for g in range(SSM_GROUPS):
        yg = y[:, g * gw:(g + 1) * gw]
        parts.append(yg * lax.rsqrt(jnp.mean(yg * yg, axis=-1, keepdims=True) + RMS_EPS))
    yn = (jnp.concatenate(parts, axis=1) * g_ref[...]).astype(BF16)
    out_ref[...] = x_ref[...] + _dot(yn, w_ref[...])


def _odd_out(x2, yf, yb, xbc_c, z, dskip, gain, w, tm=512):
    t = x2.shape[0]
    row = lambda i: (i, 0)
    return pl.pallas_call(
        _odd_out_kernel,
        grid=(t // tm,),
        in_specs=[pl.BlockSpec((tm, D_MODEL), row), pl.BlockSpec((tm, D_INNER), row),
                  pl.BlockSpec((tm, D_INNER), row), pl.BlockSpec((tm, D_INNER), row),
                  pl.BlockSpec((tm, D_INNER), row), _full((1, D_INNER)), _full((1, D_INNER)),
                  _full((D_INNER, D_MODEL))],
        out_specs=pl.BlockSpec((tm, D_MODEL), row),
        out_shape=jax.ShapeDtypeStruct((t, D_MODEL), F32),
        compiler_params=_params(("parallel",)),
        name="odd_out",
    )(x2, yf, yb, xbc_c, z, dskip, gain, w)


def _odd_mixer(x2, bsz, seq, ln, w_in, conv_w, conv_b, dt_bias, a_log, d_skip, norm_g, w_out):
    pad = jnp.zeros((D_MODEL, LANES - 2 * SSM_HEADS), w_in.dtype)
    w = jnp.concatenate([w_in, pad], axis=1).astype(BF16)
    z, xbc, dt_raw = _odd_in(x2, ln[None, :], w)
    xbc_c = _conv(xbc, seq, conv_w.astype(F32), conv_b.astype(F32)[None, :])
    lane_pad = jnp.zeros((LANES - 2 * SSM_HEADS,), F32)
    bias_row = jnp.concatenate([dt_bias.astype(F32).reshape(-1), lane_pad])[None, :]
    alog_row = jnp.concatenate([a_log.astype(F32).reshape(-1), lane_pad])[None, :]
    ch = jnp.arange(D_INNER, dtype=jnp.int32) // SSM_HEAD_DIM
    hd = jnp.arange(LANES, dtype=jnp.int32)
    ys = []
    for reverse in (False, True):
        off = SSM_HEADS if reverse else 0
        expand = jnp.where(hd[:, None] == off + ch[None, :], 1.0, 0.0).astype(BF16)
        ys.append(_ssd(xbc_c, dt_raw, bias_row, alog_row, expand, bsz, seq, reverse))
    dskip = jnp.repeat(d_skip.astype(F32), SSM_HEAD_DIM)[None, :]
    return _odd_out(x2, ys[0], ys[1], xbc_c, z, dskip, norm_g.astype(F32)[None, :], w_out.astype(BF16))


ROUTE_LANES = N_EXPERT_GROUPS + N_EXPERTS


def _router_kernel(x_ref, g_ref, w_ref, b_ref, ls_ref, meta_ref, cnt_ref, run_ref):
    i = pl.program_id(0)

    @pl.when(i == 0)
    def _():
        run_ref[...] = jnp.zeros_like(run_ref)

    h = _rms_rows(x_ref[...], g_ref[...])
    hh = h.astype(BF16)
    hl = (h - hh.astype(F32)).astype(BF16)
    w_hi = w_ref[0]
    w_lo = w_ref[1]
    logits = _dot(hh, w_hi) + _dot(hl, w_hi) + _dot(hh, w_lo) + b_ref[...]
    tm = logits.shape[0]
    lane = lax.broadcasted_iota(jnp.int32, (tm, LANES), 1).astype(F32)

    def top1(v):
        m = jnp.max(v, axis=-1, keepdims=True)
        idx = jnp.min(jnp.where(v == m, lane, float(LANES)), axis=-1, keepdims=True)
        return m, idx

    gl = jnp.where(lane < N_EXPERT_GROUPS, logits, NEG_BIG)
    gmax, gidx = top1(gl)
    g_p = 1.0 / jnp.sum(jnp.exp(gl - gmax), axis=-1, keepdims=True)
    lo = N_EXPERT_GROUPS + EXPERTS_PER_GROUP * gidx
    e1 = jnp.where((lane >= lo) & (lane < lo + EXPERTS_PER_GROUP), logits, NEG_BIG)
    m1, i1 = top1(e1)
    e2 = jnp.where(lane == i1, NEG_BIG, e1)
    m2, i2 = top1(e2)
    r = jnp.exp(m2 - m1)
    w_a = g_p / (1.0 + r)
    w_b = g_p * r / (1.0 + r)
    e_a = i1 - N_EXPERT_GROUPS
    e_b = i2 - N_EXPERT_GROUPS

    onehot = jnp.where((lane == e_a) | (lane == e_b), 1.0, 0.0)
    before = _dot(ls_ref[...], onehot.astype(BF16)) + run_ref[...]
    rank_a = jnp.sum(jnp.where(lane == e_a, before, 0.0), axis=-1, keepdims=True)
    rank_b = jnp.sum(jnp.where(lane == e_b, before, 0.0), axis=-1, keepdims=True)
    run_ref[...] = run_ref[...] + jnp.sum(onehot, axis=0, keepdims=True)
    cnt_ref[...] = jnp.broadcast_to(run_ref[...], cnt_ref.shape)

    meta = jnp.where(lane == 0, e_a, 0.0)
    meta = jnp.where(lane == 1, e_b, meta)
    meta = jnp.where(lane == 2, rank_a, meta)
    meta = jnp.where(lane == 3, rank_b, meta)
    meta = jnp.where(lane == 4, w_a, meta)
    meta = jnp.where(lane == 5, w_b, meta)
    meta_ref[...] = meta


def _router(x2, ln, w2, b, lstrict, tm):
    t = x2.shape[0]
    row = lambda i: (i, 0)
    return pl.pallas_call(
        _router_kernel,
        grid=(t // tm,),
        in_specs=[pl.BlockSpec((tm, D_MODEL), row), _full((1, D_MODEL)), _full((2, D_MODEL, LANES)),
                  _full((1, LANES)), _full((tm, tm))],
        out_specs=[pl.BlockSpec((tm, LANES), row), _full((8, LANES))],
        out_shape=[jax.ShapeDtypeStruct((t, LANES), F32), jax.ShapeDtypeStruct((8, LANES), F32)],
        scratch_shapes=[pltpu.VMEM((1, LANES), F32)],
        compiler_params=_params(("arbitrary",)),
        name="moe_router",
    )(x2, ln, w2, b, lstrict)


def _row_copy(src_hbm, src_row, dst, dst_row, sem):
    return pltpu.make_async_copy(src_hbm.at[pl.ds(src_row, 1), :], dst.at[pl.ds(dst_row, 1), :], sem)


def _dispatch_kernel(start_ref, er_ref, x_ref, g_ref, xin_in, xin_hbm, h_ref, sem, *, tm):
    del xin_in
    h_ref[...] = _rms_rows(x_ref[...], g_ref[...])

    def issue(tk, carry):
        for k in range(2):
            dst = start_ref[er_ref[0, 0, 4 * tk + k]] + er_ref[0, 0, 4 * tk + 2 + k]
            _row_copy(h_ref, tk, xin_hbm, dst, sem).start()
        return carry

    lax.fori_loop(0, tm, issue, 0, unroll=8)

    def drain(tk, carry):
        for k in range(2):
            _row_copy(h_ref, 0, xin_hbm, 0, sem).wait()
        return carry

    lax.fori_loop(0, tm, drain, 0, unroll=8)


def _dispatch(start, er, x2, ln, xin0, tm):
    t = x2.shape[0]
    return pl.pallas_call(
        functools.partial(_dispatch_kernel, tm=tm),
        grid_spec=pltpu.PrefetchScalarGridSpec(
            num_scalar_prefetch=1,
            grid=(t // tm,),
            in_specs=[pl.BlockSpec((1, 1, 4 * tm), lambda i, s: (i, 0, 0), memory_space=pltpu.SMEM),
                      pl.BlockSpec((tm, D_MODEL), lambda i, s: (i, 0)),
                      pl.BlockSpec((1, D_MODEL), lambda i, s: (0, 0)), pl.BlockSpec(memory_space=pl.ANY)],
            out_specs=pl.BlockSpec(memory_space=pl.ANY),
            scratch_shapes=[pltpu.VMEM((tm, D_MODEL), F32), pltpu.SemaphoreType.DMA]),
        out_shape=jax.ShapeDtypeStruct(xin0.shape, xin0.dtype),
        input_output_aliases={4: 0},
        compiler_params=_params(("arbitrary",)),
        name="moe_dispatch",
    )(start, er, x2, ln, xin0)


def _experts_kernel(be_ref, nused_ref, x_ref, w1_ref, w3_ref, w2_ref, o_ref):
    i = pl.program_id(0)

    @pl.when(i < nused_ref[0])
    def _():
        x = x_ref[...].astype(BF16)
        a = _silu(_dot(x, w1_ref[0])) * _dot(x, w3_ref[0])
        o_ref[...] = _dot(a.astype(BF16), w2_ref[0])

    @pl.when(i >= nused_ref[0])
    def _():
        o_ref[...] = jnp.zeros_like(o_ref)


def _experts(block_e, nused, xin, w1, w3, w2):
    nb = xin.shape[0] // MOE_BLOCK
    row = lambda i, be, nu: (i, 0)
    wsel = lambda i, be, nu: (be[i], 0, 0)
    return pl.pallas_call(
        _experts_kernel,
        grid_spec=pltpu.PrefetchScalarGridSpec(
            num_scalar_prefetch=2,
            grid=(nb,),
            in_specs=[pl.BlockSpec((MOE_BLOCK, D_MODEL), row),
                      pl.BlockSpec((1, D_MODEL, D_EXPERT), wsel), pl.BlockSpec((1, D_MODEL, D_EXPERT), wsel),
                      pl.BlockSpec((1, D_EXPERT, D_MODEL), wsel)],
            out_specs=pl.BlockSpec((MOE_BLOCK, D_MODEL), row)),
        out_shape=jax.ShapeDtypeStruct(xin.shape, F32),
        compiler_params=_params(("parallel",)),
        name="moe_experts",
    )(block_e, nused, xin, w1, w3, w2)


def _combine_ple_kernel(start_ref, er_ref, er_next_ref, x_ref, meta_ref, p_ref, g_ref, wg_ref, wp_ref, y_hbm,
                        out_ref, ya_ref, yb_ref, sem, *, tm):
    i = pl.program_id(0)
    slot = i % 2

    def row_gather(src_row, buf, s, dst_row):
        return pltpu.make_async_copy(y_hbm.at[pl.ds(src_row, 1), :], buf.at[s, pl.ds(dst_row, 1), :], sem.at[s])

    def gather_tile(idx_ref, s):
        def issue(tk, carry):
            for k, buf in enumerate((ya_ref, yb_ref)):
                src = start_ref[idx_ref[0, 0, 4 * tk + k]] + idx_ref[0, 0, 4 * tk + 2 + k]
                row_gather(src, buf, s, tk).start()
            return carry

        lax.fori_loop(0, tm, issue, 0, unroll=8)

    @pl.when(i == 0)
    def _():
        gather_tile(er_ref, 0)

    @pl.when(i + 1 < pl.num_programs(0))
    def _():
        gather_tile(er_next_ref, 1 - slot)

    def drain(tk, carry):
        for buf in (ya_ref, yb_ref):
            row_gather(0, buf, slot, 0).wait()
        return carry

    lax.fori_loop(0, tm, drain, 0, unroll=8)

    meta = meta_ref[...]
    x = x_ref[...] + (meta[:, 4:5] * ya_ref[slot] + meta[:, 5:6] * yb_ref[slot])
    hn = _rms_rows(x, g_ref[...]).astype(BF16)
    gate = 1.0 / (1.0 + jnp.exp(-_dot(hn, wg_ref[...])))
    out_ref[...] = x + _dot(p_ref[...].astype(BF16), wp_ref[...]) * gate


def _combine_ple(start, er, x2, meta, p2, ln, wg, wp, yout, tm):
    t = x2.shape[0]
    nt = t // tm
    row = lambda i, s: (i, 0)
    full = lambda shape: pl.BlockSpec(shape, lambda i, s: (0,) * len(shape))
    return pl.pallas_call(
        functools.partial(_combine_ple_kernel, tm=tm),
        grid_spec=pltpu.PrefetchScalarGridSpec(
            num_scalar_prefetch=1,
            grid=(nt,),
            in_specs=[pl.BlockSpec((1, 1, 4 * tm), lambda i, s: (i, 0, 0), memory_space=pltpu.SMEM),
                      pl.BlockSpec((1, 1, 4 * tm), lambda i, s: (jnp.minimum(i + 1, nt - 1), 0, 0),
                                   memory_space=pltpu.SMEM),
                      pl.BlockSpec((tm, D_MODEL), row), pl.BlockSpec((tm, LANES), row),
                      pl.BlockSpec((tm, PLE_DIM), row), full((1, D_MODEL)), full((D_MODEL, D_MODEL)),
                      full((PLE_DIM, D_MODEL)), pl.BlockSpec(memory_space=pl.ANY)],
            out_specs=pl.BlockSpec((tm, D_MODEL), row),
            scratch_shapes=[pltpu.VMEM((2, tm, D_MODEL), F32), pltpu.VMEM((2, tm, D_MODEL), F32),
                            pltpu.SemaphoreType.DMA((2,))]),
        out_shape=jax.ShapeDtypeStruct((t, D_MODEL), F32),
        compiler_params=_params(("arbitrary",)),
        name="moe_combine_ple",
    )(start, er, er, x2, meta, p2, ln, wg, wp, yout)


MOE_TM = 512


def _moe_ple(x2, p2, ln_ffn, w_rg, b_rg, w_re, b_re, w1, w3, w2, ln_ple, w_gate, w_proj):
    t = x2.shape[0]
    tm = MOE_TM
    wr = jnp.concatenate([w_rg, w_re, jnp.zeros((D_MODEL, LANES - ROUTE_LANES), w_rg.dtype)], axis=1).astype(F32)
    wr_hi = wr.astype(BF16)
    wr_lo = (wr - wr_hi.astype(F32)).astype(BF16)
    br = jnp.concatenate([b_rg, b_re, jnp.zeros((LANES - ROUTE_LANES,), b_rg.dtype)]).astype(F32)[None, :]
    ii = jnp.arange(tm, dtype=jnp.int32)
    lstrict = jnp.where(ii[:, None] > ii[None, :], 1.0, 0.0).astype(BF16)
    meta, cnt = _router(x2, ln_ffn[None, :], jnp.stack([wr_hi, wr_lo]), br, lstrict, tm)

    counts = cnt[0, :N_EXPERTS].astype(jnp.int32)
    padded = (counts + MOE_BLOCK - 1) // MOE_BLOCK * MOE_BLOCK
    pad_end = jnp.cumsum(padded)
    start = (pad_end - padded).astype(jnp.int32)
    nb = (t * 2) // MOE_BLOCK + N_EXPERTS
    first_row = jnp.arange(nb, dtype=jnp.int32) * MOE_BLOCK
    block_e = jnp.minimum(jnp.sum((pad_end[None, :] <= first_row[:, None]).astype(jnp.int32), axis=1),
                          N_EXPERTS - 1).astype(jnp.int32)
    nused = (pad_end[-1:] // MOE_BLOCK).astype(jnp.int32)
    er = meta[:, :4].astype(jnp.int32).reshape(t // tm, 1, 4 * tm)

    xin = _dispatch(start, er, x2, ln_ffn[None, :], jnp.zeros((nb * MOE_BLOCK, D_MODEL), F32), tm)
    yout = _experts(block_e, nused, xin, w1.astype(BF16), w3.astype(BF16), w2.astype(BF16))
    return _combine_ple(start, er, x2, meta, p2, ln_ple[None, :], w_gate.astype(BF16), w_proj.astype(BF16),
                        yout, tm)


def _trunk(x, p, prm):
    bsz, seq, _ = x.shape
    x2 = x.reshape(bsz * seq, D_MODEL)
    for i in range(DEPTH):
        j = i // 2
        if i % 2 == 0:
            x2 = _even_mixer(x2, bsz, seq, prm["ln_mix_e"][j], prm["w_in_e"][j], prm["q_gain"][j],
                             prm["k_gain"][j], prm["sink"][j], prm["w_out_e"][j])
        else:
            x2 = _odd_mixer(x2, bsz, seq, prm["ln_mix_o"][j], prm["w_in_o"][j], prm["conv_w"][j],
                            prm["conv_b"][j], prm["dt_bias"][j], prm["a_log"][j], prm["d_skip"][j],
                            prm["ssm_gain"][j], prm["w_out_o"][j])
        x2 = _moe_ple(x2, p[i].reshape(bsz * seq, PLE_DIM), prm["ln_ffn"][i], prm["w_router_g"][i],
                      prm["b_router_g"][i], prm["w_router_e"][i], prm["b_router_e"][i], prm["w1"][i],
                      prm["w3"][i], prm["w2"][i], prm["ln_ple"][i], prm["w_ple_gate"][i], prm["w_ple_proj"][i])
    return x2.reshape(bsz, seq, D_MODEL)


def kernel(x_prompt, x_sample, p_prompt, p_sample, ln_mix_e, w_in_e, q_gain, k_gain, sink, w_out_e, ln_mix_o, w_in_o, conv_w, conv_b, dt_bias, a_log, d_skip, ssm_gain, w_out_o, ln_ffn, w_router_g, b_router_g, w_router_e, b_router_e, w1, w3, w2, ln_ple, w_ple_gate, w_ple_proj):
    prm = dict(ln_mix_e=ln_mix_e, w_in_e=w_in_e, q_gain=q_gain, k_gain=k_gain, sink=sink,
               w_out_e=w_out_e, ln_mix_o=ln_mix_o, w_in_o=w_in_o, conv_w=conv_w, conv_b=conv_b,
               dt_bias=dt_bias, a_log=a_log, d_skip=d_skip, ssm_gain=ssm_gain, w_out_o=w_out_o,
               ln_ffn=ln_ffn, w_router_g=w_router_g, b_router_g=b_router_g,
               w_router_e=w_router_e, b_router_e=b_router_e, w1=w1, w3=w3, w2=w2,
               ln_ple=ln_ple, w_ple_gate=w_ple_gate, w_ple_proj=w_ple_proj)
    return (_trunk(x_prompt, p_prompt, prm), _trunk(x_sample, p_sample, prm))
```

```python
import functools
import math

import jax
import jax.numpy as jnp
from jax import lax
from jax.experimental import pallas as pl
from jax.experimental.pallas import tpu as pltpu

D_MODEL = 1024
DEPTH = 4
RMS_EPS = 1e-6
N_FOURIER_GROUPS = 4
FOURIER_WIDTH = 512
FOURIER_GROUP = 128
FFT_N2 = 128
HEAD_DIM = 64
N_Q_HEADS = 8
N_KV_HEADS = 2
ATTN_WIDTH = 512
KV_WIDTH = 128
ATTN_BLOCK = 128
ROPE_THETA = 10000.0
D_INNER = 2048
SSM_HEAD_DIM = 64
SSM_HEADS = 32
SSM_STATE = 128
SSM_GROUPS = 4
HEADS_PER_GROUP = 8
CONV_K = 5
CHUNK = 128
GN = SSM_GROUPS * SSM_STATE
CONV_DIM = D_INNER + 2 * GN
N_EXPERT_GROUPS = 4
EXPERTS_PER_GROUP = 8
N_EXPERTS = 32
D_EXPERT = 512
MOE_BLOCK = 256
PLE_DIM = 256

LANES = 128
NEG_BIG = -1e30
VMEM_LIMIT = 52 * 1024 * 1024

F32 = jnp.float32
BF16 = jnp.bfloat16


def _params(sem, vmem=VMEM_LIMIT):
    return pltpu.CompilerParams(dimension_semantics=sem, vmem_limit_bytes=vmem)


def _dot(a, b):
    return jnp.dot(a, b, preferred_element_type=F32)


def _dot_nt(a, b):
    return lax.dot_general(a, b, (((1,), (1,)), ((), ())), preferred_element_type=F32)


def _split3(v):
    hi = v.astype(BF16)
    r1 = v - hi.astype(F32)
    mid = r1.astype(BF16)
    lo = (r1 - mid.astype(F32)).astype(BF16)
    return hi, mid, lo


def _dot3(v, m_bf16):
    hi, mid, lo = _split3(v)
    return _dot(hi, m_bf16) + _dot(mid, m_bf16) + _dot(lo, m_bf16)


def _dot3_left(m_bf16, v):
    hi, mid, lo = _split3(v)
    return _dot(m_bf16, hi) + _dot(m_bf16, mid) + _dot(m_bf16, lo)


def _rms_rows(x, g):
    return x * lax.rsqrt(jnp.mean(x * x, axis=-1, keepdims=True) + RMS_EPS) * g


def _silu(x):
    return x * (1.0 / (1.0 + jnp.exp(-x)))


def _full(shape):
    return pl.BlockSpec(shape, lambda *_: (0,) * len(shape))


def _even_in_kernel(x_ref, g_ref, w_ref, dft_ref, bd_ref, qg_ref, kg_ref, cos_ref, sin_ref,
                    gout_ref, q_ref, k_ref, v_ref):
    h = _rms_rows(x_ref[...], g_ref[...]).astype(BF16)
    u = _dot(h, w_ref[...])
    a = u[:, :FOURIER_WIDTH].astype(BF16)
    for g in range(N_FOURIER_GROUPS):
        r = _dot(a[:, g * LANES:(g + 1) * LANES], dft_ref[...])
        gout_ref[:, g * LANES:(g + 1) * LANES] = r[:, :LANES].astype(BF16)
        gout_ref[:, FOURIER_WIDTH + g * LANES:FOURIER_WIDTH + (g + 1) * LANES] = r[:, LANES:].astype(BF16)

    cos = cos_ref[...]
    sin = sin_ref[...]

    def norm_rope(t, gain, width):
        n = width // LANES
        bd = bd_ref[:width, :width]
        t2 = t * t
        hi = t2.astype(BF16)
        lo = (t2 - hi.astype(F32)).astype(BF16)
        ms = _dot(hi, bd) + _dot(lo, bd)
        tn = t * lax.rsqrt(ms + RMS_EPS) * gain
        lane = lax.broadcasted_iota(jnp.int32, tn.shape, 1)
        first_half = (lane % HEAD_DIM) < (HEAD_DIM // 2)
        rot = jnp.where(first_half, pltpu.roll(tn, width - HEAD_DIM // 2, 1), pltpu.roll(tn, HEAD_DIM // 2, 1))
        c = jnp.concatenate([cos] * n, axis=1)
        s = jnp.concatenate([sin] * n, axis=1)
        return tn * c + rot * s

    q = norm_rope(u[:, 512:1024], qg_ref[...], ATTN_WIDTH) * (HEAD_DIM ** -0.5)
    k = norm_rope(u[:, 1024:1280], kg_ref[...], 2 * KV_WIDTH)
    q_ref[...] = q.astype(BF16)
    k_ref[...] = k.astype(BF16)
    v_ref[...] = u[:, 1280:1536].astype(BF16)


def _even_in(x2, seq, ln, w, dft, bd, qg, kg, cos_t, sin_t, tm=512):
    t = x2.shape[0]
    nseq = seq // tm
    row = lambda i: (i, 0)
    pos = lambda i: (i % nseq, 0)
    return pl.pallas_call(
        _even_in_kernel,
        grid=(t // tm,),
        in_specs=[pl.BlockSpec((tm, D_MODEL), row), _full((1, D_MODEL)), _full((D_MODEL, 1536)),
                  _full((LANES, 2 * LANES)), _full((512, 512)), _full((1, 512)), _full((1, 256)),
                  pl.BlockSpec((tm, LANES), pos), pl.BlockSpec((tm, LANES), pos)],
        out_specs=[pl.BlockSpec((tm, 1024), row), pl.BlockSpec((tm, 512), row),
                   pl.BlockSpec((tm, 256), row), pl.BlockSpec((tm, 256), row)],
        out_shape=[jax.ShapeDtypeStruct((t, 1024), BF16), jax.ShapeDtypeStruct((t, 512), BF16),
                   jax.ShapeDtypeStruct((t, 256), BF16), jax.ShapeDtypeStruct((t, 256), BF16)],
        compiler_params=_params(("parallel",)),
        name="even_in",
    )(x2, ln, w, dft, bd, qg, kg, cos_t, sin_t)


def _fft1_kernel(g_ref, c_ref, s_ref, y_ref):
    g = g_ref[0]
    p = _dot(c_ref[...], g)
    q = _dot(s_ref[...], g)
    for j in range(g.shape[1] // 1024):
        o = j * 1024
        y_ref[0, :, o:o + 512] = (p[:, o:o + 512] + q[:, o + 512:o + 1024]).astype(BF16)
        y_ref[0, :, o + 512:o + 1024] = (p[:, o + 512:o + 1024] - q[:, o:o + 512]).astype(BF16)


def _fft1(g3, c1, s1, cb=8192):
    b, n1, width = g3.shape
    blk = lambda i, j: (i, 0, j)
    return pl.pallas_call(
        _fft1_kernel,
        grid=(b, width // cb),
        in_specs=[pl.BlockSpec((1, n1, cb), blk), _full((n1, n1)), _full((n1, n1))],
        out_specs=pl.BlockSpec((1, n1, cb), blk),
        out_shape=jax.ShapeDtypeStruct(g3.shape, BF16),
        compiler_params=_params(("parallel", "parallel")),
        name="fft_stage1",
    )(g3, c1, s1)


def _fft2_kernel(y_ref, mc_ref, ms_ref, o_ref, *, scale):
    for j in range(y_ref.shape[1]):
        yr = y_ref[0, j, :, :512]
        yi = y_ref[0, j, :, 512:]
        r = _dot(mc_ref[j], yr) + _dot(ms_ref[j], yi)
        o_ref[0, :, j * 512:(j + 1) * 512] = (r * scale).astype(BF16)


def _fft2(y4, mc, ms, scale, kb=8):
    b, n1, n2, _ = y4.shape
    return pl.pallas_call(
        functools.partial(_fft2_kernel, scale=scale),
        grid=(n1 // kb, b),
        in_specs=[pl.BlockSpec((1, kb, n2, 1024), lambda k, i: (i, k, 0, 0)),
                  pl.BlockSpec((kb, n2, n2), lambda k, i: (k, 0, 0)),
                  pl.BlockSpec((kb, n2, n2), lambda k, i: (k, 0, 0))],
        out_specs=pl.BlockSpec((1, n2, kb * 512), lambda k, i: (i, 0, k)),
        out_shape=jax.ShapeDtypeStruct((b, n2, n1 * 512), BF16),
        compiler_params=_params(("parallel", "parallel")),
        name="fft_stage2",
    )(y4, mc, ms)


def _dft_tables(seq):
    n1, n2 = seq // FFT_N2, FFT_N2
    i1 = jnp.arange(n1, dtype=jnp.int32)
    ang1 = ((i1[:, None] * i1[None, :]) % n1).astype(F32) * (2.0 * math.pi / n1)
    c1, s1 = jnp.cos(ang1).astype(BF16), jnp.sin(ang1).astype(BF16)
    i2 = jnp.arange(n2, dtype=jnp.int32)
    ph = (i2[None, None, :] * i1[:, None, None] + n1 * i2[None, :, None] * i2[None, None, :]) % seq
    ang2 = ph.astype(F32) * (2.0 * math.pi / seq)
    mc, ms = jnp.cos(ang2).astype(BF16), jnp.sin(ang2).astype(BF16)
    ic = jnp.arange(FOURIER_GROUP, dtype=jnp.int32)
    angc = ((ic[:, None] * ic[None, :]) % FOURIER_GROUP).astype(F32) * (2.0 * math.pi / FOURIER_GROUP)
    dftc = jnp.concatenate([jnp.cos(angc), -jnp.sin(angc)], axis=1).astype(BF16)
    return c1, s1, mc, ms, dftc


def _attn_kernel(sink_ref, q_ref, kp_ref, ko_ref, kn_ref, vp_ref, vo_ref, vn_ref, o_ref, kc_ref, vc_ref):
    i = pl.program_id(1)
    last = pl.num_programs(1) - 1
    tq = q_ref.shape[0]
    nb = tq // ATTN_BLOCK
    kc_ref[0:ATTN_BLOCK] = kp_ref[...]
    kc_ref[ATTN_BLOCK:ATTN_BLOCK + tq] = ko_ref[...]
    kc_ref[ATTN_BLOCK + tq:] = kn_ref[...]
    vc_ref[0:ATTN_BLOCK] = vp_ref[...]
    vc_ref[ATTN_BLOCK:ATTN_BLOCK + tq] = vo_ref[...]
    vc_ref[ATTN_BLOCK + tq:] = vn_ref[...]

    w3 = 3 * ATTN_BLOCK
    qpos = lax.broadcasted_iota(jnp.int32, (ATTN_BLOCK, w3), 0) + ATTN_BLOCK
    kpos = lax.broadcasted_iota(jnp.int32, (ATTN_BLOCK, w3), 1)
    band = jnp.abs(kpos - qpos) <= ATTN_BLOCK
    lane = lax.broadcasted_iota(jnp.int32, (w3, LANES), 1)
    lo_half = lane < HEAD_DIM
    out_lane = lax.broadcasted_iota(jnp.int32, (ATTN_BLOCK, LANES), 1) < HEAD_DIM

    for jb in range(nb):
        ok = band
        if jb == 0:
            ok = ok & ((kpos >= ATTN_BLOCK) | (i > 0))
        if jb == nb - 1:
            ok = ok & ((kpos < 2 * ATTN_BLOCK) | (i < last))
        bias = jnp.where(ok, 0.0, NEG_BIG).astype(F32)
        qb = q_ref[jb * ATTN_BLOCK:(jb + 1) * ATTN_BLOCK, :]
        kw = kc_ref[jb * ATTN_BLOCK:jb * ATTN_BLOCK + w3, :]
        vw = vc_ref[jb * ATTN_BLOCK:jb * ATTN_BLOCK + w3, :]
        for g in range(N_KV_HEADS):
            kg = kw[:, g * LANES:(g + 1) * LANES]
            vg = vw[:, g * LANES:(g + 1) * LANES]
            zero = jnp.zeros_like(kg)
            kbd = jnp.concatenate([jnp.where(lo_half, kg, zero), jnp.where(lo_half, zero, kg)], axis=0)
            vbd = jnp.concatenate([jnp.where(lo_half, vg, zero), jnp.where(lo_half, zero, vg)], axis=0)
            for pr in range(2):
                col = (2 * g + pr) * LANES
                s = _dot_nt(qb[:, col:col + LANES], kbd)
                es, rs = [], []
                for half in range(2):
                    sk = sink_ref[4 * g + 2 * pr + half]
                    sh = s[:, half * w3:(half + 1) * w3] + bias
                    m = jnp.maximum(jnp.max(sh, axis=-1, keepdims=True), sk)
                    e = jnp.exp(sh - m)
                    den = jnp.sum(e, axis=-1, keepdims=True) + jnp.exp(sk - m)
                    es.append(e.astype(BF16))
                    rs.append(1.0 / den)
                o = _dot(jnp.concatenate(es, axis=1), vbd)
                o = o * jnp.where(out_lane, rs[0], rs[1])
                o_ref[jb * ATTN_BLOCK:(jb + 1) * ATTN_BLOCK, col:col + LANES] = o.astype(BF16)


def _attention(q, k, v, sink, bsz, seq, tq=512):
    t = q.shape[0]
    nq = seq // tq
    r = tq // ATTN_BLOCK
    nblk = t // ATTN_BLOCK
    own = lambda b, i, s: (b * nq + i, 0)
    prev = lambda b, i, s: (jnp.maximum((b * nq + i) * r - 1, 0), 0)
    nxt = lambda b, i, s: (jnp.minimum((b * nq + i + 1) * r, nblk - 1), 0)
    kv = lambda m: pl.BlockSpec((ATTN_BLOCK if m is not own else tq, 256), m)
    return pl.pallas_call(
        _attn_kernel,
        grid_spec=pltpu.PrefetchScalarGridSpec(
            num_scalar_prefetch=1,
            grid=(bsz, nq),
            in_specs=[pl.BlockSpec((tq, 512), own), kv(prev), kv(own), kv(nxt), kv(prev), kv(own), kv(nxt)],
            out_specs=pl.BlockSpec((tq, 512), own),
            scratch_shapes=[pltpu.VMEM((tq + 2 * ATTN_BLOCK, 256), BF16),
                            pltpu.VMEM((tq + 2 * ATTN_BLOCK, 256), BF16)]),
        out_shape=jax.ShapeDtypeStruct((t, 512), BF16),
        compiler_params=_params(("parallel", "parallel")),
        name="window_attention",
    )(sink, q, k, k, k, v, v, v)


def _even_out_kernel(x_ref, f_ref, o_ref, w_ref, out_ref):
    out_ref[...] = (x_ref[...] + _dot(f_ref[...], w_ref[:512, :]) + _dot(o_ref[...], w_ref[512:, :]))


def _even_out(x2, f, o, w, tm=512):
    t = x2.shape[0]
    row = lambda i: (i, 0)
    return pl.pallas_call(
        _even_out_kernel,
        grid=(t // tm,),
        in_specs=[pl.BlockSpec((tm, D_MODEL), row), pl.BlockSpec((tm, 512), row),
                  pl.BlockSpec((tm, 512), row), _full((1024, D_MODEL))],
        out_specs=pl.BlockSpec((tm, D_MODEL), row),
        out_shape=jax.ShapeDtypeStruct((t, D_MODEL), F32),
        compiler_params=_params(("parallel",)),
        name="even_out",
    )(x2, f, o, w)


def _even_mixer(x2, bsz, seq, ln, w_in, q_gain, k_gain, sink, w_out):
    c1, s1, mc, ms, dftc = _dft_tables(seq)
    n1 = seq // FFT_N2
    kcols = w_in[:, 1024:1152]
    vcols = w_in[:, 1152:1280]
    dup = lambda c: jnp.concatenate([c[:, :64], c[:, :64], c[:, 64:], c[:, 64:]], axis=1)
    w = jnp.concatenate([w_in[:, :1024], dup(kcols), dup(vcols)], axis=1).astype(BF16)
    hid = jnp.arange(512, dtype=jnp.int32) // HEAD_DIM
    bd = jnp.where(hid[:, None] == hid[None, :], 1.0 / HEAD_DIM, 0.0).astype(BF16)
    inv = 1.0 / (ROPE_THETA ** (jnp.arange(0, HEAD_DIM, 2, dtype=F32) / HEAD_DIM))
    ang = jnp.arange(seq, dtype=F32)[:, None] * inv[None, :]
    cos, sin = jnp.cos(ang), jnp.sin(ang)
    cos_t = jnp.concatenate([cos, cos, cos, cos], axis=1)
    sin_t = jnp.concatenate([-sin, sin, -sin, sin], axis=1)
    qg = jnp.tile(q_gain.astype(F32), N_Q_HEADS)[None, :]
    kg = jnp.tile(k_gain.astype(F32), 2 * N_KV_HEADS)[None, :]
    g, q, k, v = _even_in(x2, seq, ln[None, :], w, dftc, bd, qg, kg, cos_t, sin_t)
    y = _fft1(g.reshape(bsz, n1, FFT_N2 * 1024), c1, s1)
    scale = 1.0 / math.sqrt(seq * FOURIER_GROUP)
    f = _fft2(y.reshape(bsz, n1, FFT_N2, 1024), mc, ms, scale).reshape(bsz * seq, 512)
    o = _attention(q, k, v, sink.astype(F32), bsz, seq)
    return _even_out(x2, f, o, w_out.astype(BF16))


def _odd_in_kernel(x_ref, g_ref, w_ref, z_ref, xbc_ref, dt_ref):
    h = _rms_rows(x_ref[...], g_ref[...]).astype(BF16)
    z_ref[...] = _dot(h, w_ref[:, :D_INNER]).astype(BF16)
    xbc_ref[...] = _dot(h, w_ref[:, D_INNER:D_INNER + CONV_DIM]).astype(BF16)
    dt_ref[...] = _dot(h, w_ref[:, D_INNER + CONV_DIM:])


def _odd_in(x2, ln, w, tm=256):
    t = x2.shape[0]
    row = lambda i: (i, 0)
    wcols = D_INNER + CONV_DIM + LANES
    return pl.pallas_call(
        _odd_in_kernel,
        grid=(t // tm,),
        in_specs=[pl.BlockSpec((tm, D_MODEL), row), _full((1, D_MODEL)), _full((D_MODEL, wcols))],
        out_specs=[pl.BlockSpec((tm, D_INNER), row), pl.BlockSpec((tm, CONV_DIM), row),
                   pl.BlockSpec((tm, LANES), row)],
        out_shape=[jax.ShapeDtypeStruct((t, D_INNER), BF16), jax.ShapeDtypeStruct((t, CONV_DIM), BF16),
                   jax.ShapeDtypeStruct((t, LANES), F32)],
        compiler_params=_params(("parallel",)),
        name="odd_in",
    )(x2, ln, w)


CONV_HALO = 16


def _conv_kernel(prev_ref, main_ref, next_ref, w_ref, b_ref, o_ref, win_ref, *, tiles_per_seq):
    i = pl.program_id(0) % tiles_per_seq
    tc = main_ref.shape[0]
    keep_prev = (i > 0).astype(F32)
    keep_next = (i < tiles_per_seq - 1).astype(F32)
    win_ref[0:CONV_HALO, :] = prev_ref[...].astype(F32) * keep_prev
    win_ref[CONV_HALO:CONV_HALO + tc, :] = main_ref[...].astype(F32)
    win_ref[CONV_HALO + tc:, :] = next_ref[...].astype(F32) * keep_next
    cb = 512
    for c in range(CONV_DIM // cb):
        cols = slice(c * cb, (c + 1) * cb)
        acc = jnp.broadcast_to(b_ref[:, cols], (tc, cb))
        for k in range(CONV_K):
            start = CONV_HALO - CONV_K // 2 + k
            acc = acc + win_ref[start:start + tc, cols] * w_ref[k:k + 1, cols]
        o_ref[:, cols] = _silu(acc).astype(BF16)


def _conv(xbc, seq, w, b, tc=512):
    t = xbc.shape[0]
    r = tc // CONV_HALO
    nh = t // CONV_HALO
    return pl.pallas_call(
        functools.partial(_conv_kernel, tiles_per_seq=seq // tc),
        grid=(t // tc,),
        in_specs=[pl.BlockSpec((CONV_HALO, CONV_DIM), lambda i: (jnp.maximum(i * r - 1, 0), 0)),
                  pl.BlockSpec((tc, CONV_DIM), lambda i: (i, 0)),
                  pl.BlockSpec((CONV_HALO, CONV_DIM), lambda i: (jnp.minimum((i + 1) * r, nh - 1), 0)),
                  _full((CONV_K, CONV_DIM)), _full((1, CONV_DIM))],
        out_specs=pl.BlockSpec((tc, CONV_DIM), lambda i: (i, 0)),
        out_shape=jax.ShapeDtypeStruct((t, CONV_DIM), BF16),
        scratch_shapes=[pltpu.VMEM((tc + 2 * CONV_HALO, CONV_DIM), F32)],
        compiler_params=_params(("parallel",)),
        name="ssd_conv",
    )(xbc, xbc, xbc, w, b)


N_PAIRS = SSM_HEADS // 2
PAIRS_PER_GROUP = HEADS_PER_GROUP // 2


LOG2E = math.log2(math.e)
SSD_STEP = 2 * CHUNK


def _ssd_chunk(xbc_ref, dt_ref, bias, a_row, expand, y_ref, state_ref, r0, reverse, mask, tri_col, lo_half):
    off = SSM_HEADS if reverse else 0
    rows = slice(r0, r0 + CHUNK)
    dt_lh = dt_ref[rows, :] + bias
    dt_lh = jnp.maximum(dt_lh, 0.0) + jnp.log(1.0 + jnp.exp(-jnp.abs(dt_lh)))
    acs_lh = _dot3_left(tri_col, dt_lh * a_row)
    edge = 0 if reverse else CHUNK - 1
    acs2_lh = acs_lh * LOG2E
    tot2_row = acs2_lh[edge:edge + 1, :]
    r_hl = acs2_lh.T
    dt_hi = dt_lh.astype(BF16)
    dt_mid = (dt_lh - dt_hi.astype(F32)).astype(BF16)
    dt_exp = _dot(dt_hi, expand) + _dot(dt_mid, expand)
    lo_row = lo_half[0:1, :]

    for g in range(SSM_GROUPS):
        b_g = xbc_ref[rows, D_INNER + g * SSM_STATE:D_INNER + (g + 1) * SSM_STATE]
        c_g = xbc_ref[rows, D_INNER + GN + g * SSM_STATE:D_INNER + GN + (g + 1) * SSM_STATE]
        cb = _dot_nt(c_g, b_g)
        bt = b_g.astype(F32).T.astype(BF16)
        gcols = slice(g * PAIRS_PER_GROUP * LANES, (g + 1) * PAIRS_PER_GROUP * LANES)
        y_off = _dot(c_g, state_ref[:, gcols].astype(BF16))
        for pp in range(PAIRS_PER_GROUP):
            p = g * PAIRS_PER_GROUP + pp
            hd0, hd1 = off + 2 * p, off + 2 * p + 1
            colb = [jnp.broadcast_to(acs2_lh[:, hd:hd + 1], (CHUNK, CHUNK)) for hd in (hd0, hd1)]
            ms = [(cb * jnp.exp2(jnp.where(mask, cb_l - r_hl[hd:hd + 1, :], NEG_BIG))).astype(BF16)
                  for cb_l, hd in zip(colb, (hd0, hd1))]
            pcols = slice(p * LANES, (p + 1) * LANES)
            xdt = xbc_ref[rows, pcols].astype(F32) * dt_exp[:, pcols]
            xdt_b = xdt.astype(BF16)
            zx = jnp.zeros_like(xdt_b)
            xbd = jnp.concatenate([jnp.where(lo_half, xdt_b, zx), jnp.where(lo_half, zx, xdt_b)], axis=0)
            colsel = jnp.where(lo_half, colb[0], colb[1])
            y = _dot(jnp.concatenate(ms, axis=1), xbd)
            y = y + jnp.exp2(colsel) * y_off[:, pp * LANES:(pp + 1) * LANES]
            y_ref[rows, pcols] = y.astype(BF16)
            tot2 = jnp.where(lo_row, tot2_row[:, hd0:hd0 + 1], tot2_row[:, hd1:hd1 + 1])
            snew = _dot(bt, (xdt * jnp.exp2(tot2 - colsel)).astype(BF16))
            state_ref[:, pcols] = state_ref[:, pcols] * jnp.exp2(tot2) + snew


def _ssd_kernel(xbc_ref, dt_ref, bias_ref, alog_ref, exp_ref, y_ref, state_ref, *, reverse):
    @pl.when(pl.program_id(1) == 0)
    def _():
        state_ref[...] = jnp.zeros_like(state_ref)

    r_io = lax.broadcasted_iota(jnp.int32, (CHUNK, CHUNK), 0)
    c_io = lax.broadcasted_iota(jnp.int32, (CHUNK, CHUNK), 1)
    mask = (r_io <= c_io) if reverse else (r_io >= c_io)
    tri_col = jnp.where(mask, 1.0, 0.0).astype(BF16)
    lo_half = lax.broadcasted_iota(jnp.int32, (CHUNK, LANES), 1) < SSM_HEAD_DIM
    bias = bias_ref[...]
    a_row = -jnp.exp(alog_ref[...])
    n_sub = SSD_STEP // CHUNK
    order = range(n_sub - 1, -1, -1) if reverse else range(n_sub)
    for sub in order:
        _ssd_chunk(xbc_ref, dt_ref, bias, a_row, exp_ref[...], y_ref, state_ref, sub * CHUNK, reverse,
                   mask, tri_col, lo_half)


def _ssd(xbc_c, dt_raw, bias_row, alog_row, expand, bsz, seq, reverse):
    t = xbc_c.shape[0]
    ns = seq // SSD_STEP
    if reverse:
        blk = lambda b, c: (b * ns + ns - 1 - c, 0)
    else:
        blk = lambda b, c: (b * ns + c, 0)
    return pl.pallas_call(
        functools.partial(_ssd_kernel, reverse=reverse),
        grid=(bsz, ns),
        in_specs=[pl.BlockSpec((SSD_STEP, CONV_DIM), blk), pl.BlockSpec((SSD_STEP, LANES), blk),
                  _full((1, LANES)), _full((1, LANES)), _full((LANES, D_INNER))],
        out_specs=pl.BlockSpec((SSD_STEP, D_INNER), blk),
        out_shape=jax.ShapeDtypeStruct((t, D_INNER), BF16),
        scratch_shapes=[pltpu.VMEM((SSM_STATE, D_INNER), F32)],
        compiler_params=_params(("parallel", "arbitrary")),
        name="ssd_scan_bwd" if reverse else "ssd_scan_fwd",
    )(xbc_c, dt_raw, bias_row, alog_row, expand)


def _odd_out_kernel(x_ref, yf_ref, yb_ref, xs_ref, z_ref, d_ref, g_ref, w_ref, out_ref):
    y = yf_ref[...].astype(F32) + yb_ref[...].astype(F32) + d_ref[...] * xs_ref[...].astype(F32)
    y = y * _silu(z_ref[...].astype(F32))
    gw = D_INNER // SSM_GROUPS
    parts = []
    for g in range(SSM_GROUPS):
        yg = y[:, g * gw:(g + 1) * gw]
        parts.append(yg * lax.rsqrt(jnp.mean(yg * yg, axis=-1, keepdims=True) + RMS_EPS))
    yn = (jnp.concatenate(parts, axis=1) * g_ref[...]).astype(BF16)
    out_ref[...] = x_ref[...] + _dot(yn, w_ref[...])


def _odd_out(x2, yf, yb, xbc_c, z, dskip, gain, w, tm=512):
    t = x2.shape[0]
    row = lambda i: (i, 0)
    return pl.pallas_call(
        _odd_out_kernel,
        grid=(t // tm,),
        in_specs=[pl.BlockSpec((tm, D_MODEL), row), pl.BlockSpec((tm, D_INNER), row),
                  pl.BlockSpec((tm, D_INNER), row), pl.BlockSpec((tm, D_INNER), row),
                  pl.BlockSpec((tm, D_INNER), row), _full((1, D_INNER)), _full((1, D_INNER)),
                  _full((D_INNER, D_MODEL))],
        out_specs=pl.BlockSpec((tm, D_MODEL), row),
        out_shape=jax.ShapeDtypeStruct((t, D_MODEL), F32),
        compiler_params=_params(("parallel",)),
        name="odd_out",
    )(x2, yf, yb, xbc_c, z, dskip, gain, w)


def _odd_mixer(x2, bsz, seq, ln, w_in, conv_w, conv_b, dt_bias, a_log, d_skip, norm_g, w_out):
    pad = jnp.zeros((D_MODEL, LANES - 2 * SSM_HEADS), w_in.dtype)
    w = jnp.concatenate([w_in, pad], axis=1).astype(BF16)
    z, xbc, dt_raw = _odd_in(x2, ln[None, :], w)
    xbc_c = _conv(xbc, seq, conv_w.astype(F32), conv_b.astype(F32)[None, :])
    lane_pad = jnp.zeros((LANES - 2 * SSM_HEADS,), F32)
    bias_row = jnp.concatenate([dt_bias.astype(F32).reshape(-1), lane_pad])[None, :]
    alog_row = jnp.concatenate([a_log.astype(F32).reshape(-1), lane_pad])[None, :]
    ch = jnp.arange(D_INNER, dtype=jnp.int32) // SSM_HEAD_DIM
    hd = jnp.arange(LANES, dtype=jnp.int32)
    ys = []
    for reverse in (False, True):
        off = SSM_HEADS if reverse else 0
        expand = jnp.where(hd[:, None] == off + ch[None, :], 1.0, 0.0).astype(BF16)
        ys.append(_ssd(xbc_c, dt_raw, bias_row, alog_row, expand, bsz, seq, reverse))
    dskip = jnp.repeat(d_skip.astype(F32), SSM_HEAD_DIM)[None, :]
    return _odd_out(x2, ys[0], ys[1], xbc_c, z, dskip, norm_g.astype(F32)[None, :], w_out.astype(BF16))


ROUTE_LANES = N_EXPERT_GROUPS + N_EXPERTS


def _router_kernel(x_ref, g_ref, w_ref, b_ref, ls_ref, meta_ref, cnt_ref, run_ref):
    i = pl.program_id(0)

    @pl.when(i == 0)
    def _():
        run_ref[...] = jnp.zeros_like(run_ref)

    h = _rms_rows(x_ref[...], g_ref[...])
    hh = h.astype(BF16)
    hl = (h - hh.astype(F32)).astype(BF16)
    w_hi = w_ref[0]
    w_lo = w_ref[1]
    logits = _dot(hh, w_hi) + _dot(hl, w_hi) + _dot(hh, w_lo) + b_ref[...]
    tm = logits.shape[0]
    lane = lax.broadcasted_iota(jnp.int32, (tm, LANES), 1).astype(F32)

    def top1(v):
        m = jnp.max(v, axis=-1, keepdims=True)
        idx = jnp.min(jnp.where(v == m, lane, float(LANES)), axis=-1, keepdims=True)
        return m, idx

    gl = jnp.where(lane < N_EXPERT_GROUPS, logits, NEG_BIG)
    gmax, gidx = top1(gl)
    g_p = 1.0 / jnp.sum(jnp.exp(gl - gmax), axis=-1, keepdims=True)
    lo = N_EXPERT_GROUPS + EXPERTS_PER_GROUP * gidx
    e1 = jnp.where((lane >= lo) & (lane < lo + EXPERTS_PER_GROUP), logits, NEG_BIG)
    m1, i1 = top1(e1)
    e2 = jnp.where(lane == i1, NEG_BIG, e1)
    m2, i2 = top1(e2)
    r = jnp.exp(m2 - m1)
    w_a = g_p / (1.0 + r)
    w_b = g_p * r / (1.0 + r)
    e_a = i1 - N_EXPERT_GROUPS
    e_b = i2 - N_EXPERT_GROUPS

    onehot = jnp.where((lane == e_a) | (lane == e_b), 1.0, 0.0)
    before = _dot(ls_ref[...], onehot.astype(BF16)) + run_ref[...]
    rank_a = jnp.sum(jnp.where(lane == e_a, before, 0.0), axis=-1, keepdims=True)
    rank_b = jnp.sum(jnp.where(lane == e_b, before, 0.0), axis=-1, keepdims=True)
    run_ref[...] = run_ref[...] + jnp.sum(onehot, axis=0, keepdims=True)
    cnt_ref[...] = jnp.broadcast_to(run_ref[...], cnt_ref.shape)

    meta = jnp.where(lane == 0, e_a, 0.0)
    meta = jnp.where(lane == 1, e_b, meta)
    meta = jnp.where(lane == 2, rank_a, meta)
    meta = jnp.where(lane == 3, rank_b, meta)
    meta = jnp.where(lane == 4, w_a, meta)
    meta = jnp.where(lane == 5, w_b, meta)
    meta_ref[...] = meta


def _router(x2, ln, w2, b, lstrict, tm):
    t = x2.shape[0]
    row = lambda i: (i, 0)
    return pl.pallas_call(
        _router_kernel,
        grid=(t // tm,),
        in_specs=[pl.BlockSpec((tm, D_MODEL), row), _full((1, D_MODEL)), _full((2, D_MODEL, LANES)),
                  _full((1, LANES)), _full((tm, tm))],
        out_specs=[pl.BlockSpec((tm, LANES), row), _full((8, LANES))],
        out_shape=[jax.ShapeDtypeStruct((t, LANES), F32), jax.ShapeDtypeStruct((8, LANES), F32)],
        scratch_shapes=[pltpu.VMEM((1, LANES), F32)],
        compiler_params=_params(("arbitrary",)),
        name="moe_router",
    )(x2, ln, w2, b, lstrict)


SUBLANES = 8


def _dispatch_kernel(pad_end_ref, nused_ref, dest_ref, x_ref, g_ref, xin_hbm, h_ref, zero_ref, sem, zsem,
                     *, tm, nb):
    def zero_block(first_row):
        return pltpu.make_async_copy(zero_ref, xin_hbm.at[pl.ds(pl.multiple_of(first_row, MOE_BLOCK), MOE_BLOCK), :],
                                     zsem)

    def seg_zero(action):
        for e in range(N_EXPERTS):
            end = pad_end_ref[e]
            prev = pad_end_ref[e - 1] if e else 0

            @pl.when(end > prev)
            def _():
                action(zero_block(end - MOE_BLOCK))

    def tail_zero(action):
        def body(b, carry):
            action(zero_block(b * MOE_BLOCK))
            return carry

        lax.fori_loop(nused_ref[0], nb, body, 0)

    @pl.when(pl.program_id(0) == 0)
    def _():
        zero_ref[...] = jnp.zeros_like(zero_ref)
        seg_zero(lambda c: c.start())
        tail_zero(lambda c: c.start())
        seg_zero(lambda c: c.wait())
        tail_zero(lambda c: c.wait())

    h_ref[...] = _rms_rows(x_ref[...], g_ref[...]).reshape(h_ref.shape)

    def row_copy(grp, c, dst_row):
        return pltpu.make_async_copy(h_ref.at[grp, pl.ds(c, 1), :], xin_hbm.at[pl.ds(dst_row, 1), :], sem)

    def issue(grp, carry):
        for c in range(SUBLANES):
            for k in range(2):
                row_copy(grp, c, dest_ref[0, 0, 2 * SUBLANES * grp + 2 * c + k]).start()
        return carry

    lax.fori_loop(0, tm // SUBLANES, issue, 0)

    def drain(grp, carry):
        for _ in range(2 * SUBLANES):
            row_copy(0, 0, 0).wait()
        return carry

    lax.fori_loop(0, tm // SUBLANES, drain, 0)


def _dispatch(pad_end, nused, dest, x2, ln, nb, tm):
    t = x2.shape[0]
    return pl.pallas_call(
        functools.partial(_dispatch_kernel, tm=tm, nb=nb),
        grid_spec=pltpu.PrefetchScalarGridSpec(
            num_scalar_prefetch=2,
            grid=(t // tm,),
            in_specs=[pl.BlockSpec((1, 1, 2 * tm), lambda i, pe, nu: (i, 0, 0), memory_space=pltpu.SMEM),
                      pl.BlockSpec((tm, D_MODEL), lambda i, pe, nu: (i, 0)),
                      pl.BlockSpec((1, D_MODEL), lambda i, pe, nu: (0, 0))],
            out_specs=pl.BlockSpec(memory_space=pl.ANY),
            scratch_shapes=[pltpu.VMEM((tm // SUBLANES, SUBLANES, D_MODEL), F32),
                            pltpu.VMEM((MOE_BLOCK, D_MODEL), F32),
                            pltpu.SemaphoreType.DMA, pltpu.SemaphoreType.DMA]),
        out_shape=jax.ShapeDtypeStruct((nb * MOE_BLOCK, D_MODEL), F32),
        compiler_params=_params(("arbitrary",)),
        name="moe_dispatch",
    )(pad_end, nused, dest, x2, ln)


EXPERT_BLOCKS_PER_STEP = 2


def _experts_kernel(be_ref, nused_ref, x_ref, w1a_ref, w3a_ref, w2a_ref, w1b_ref, w3b_ref, w2b_ref, o_ref):
    i = pl.program_id(0)
    for half, (w1_ref, w3_ref, w2_ref) in enumerate(((w1a_ref, w3a_ref, w2a_ref), (w1b_ref, w3b_ref, w2b_ref))):
        rows = slice(half * MOE_BLOCK, (half + 1) * MOE_BLOCK)
        blk = EXPERT_BLOCKS_PER_STEP * i + half

        @pl.when(blk < nused_ref[0])
        def _():
            x = x_ref[rows, :].astype(BF16)
            a = _silu(_dot(x, w1_ref[0])) * _dot(x, w3_ref[0])
            o_ref[rows, :] = _dot(a.astype(BF16), w2_ref[0])

        @pl.when(blk >= nused_ref[0])
        def _():
            o_ref[rows, :] = jnp.zeros((MOE_BLOCK, D_MODEL), F32)


def _experts(block_e, nused, xin, w1, w3, w2):
    n = EXPERT_BLOCKS_PER_STEP
    nb = xin.shape[0] // MOE_BLOCK
    assert nb % n == 0
    row = lambda i, be, nu: (i, 0)
    specs = []
    for half in range(n):
        wsel = functools.partial(lambda i, be, nu, half: (be[n * i + half], 0, 0), half=half)
        specs += [pl.BlockSpec((1, D_MODEL, D_EXPERT), wsel), pl.BlockSpec((1, D_MODEL, D_EXPERT), wsel),
                  pl.BlockSpec((1, D_EXPERT, D_MODEL), wsel)]
    return pl.pallas_call(
        _experts_kernel,
        grid_spec=pltpu.PrefetchScalarGridSpec(
            num_scalar_prefetch=2,
            grid=(nb // n,),
            in_specs=[pl.BlockSpec((n * MOE_BLOCK, D_MODEL), row)] + specs,
            out_specs=pl.BlockSpec((n * MOE_BLOCK, D_MODEL), row)),
        out_shape=jax.ShapeDtypeStruct(xin.shape, F32),
        compiler_params=_params(("parallel",)),
        name="moe_experts",
    )(block_e, nused, xin, *([w1, w3, w2] * n))


def _combine_ple_kernel(dest_ref, dest_next_ref, x_ref, meta_ref, p_ref, g_ref, wg_ref, wp_ref, y_hbm,
                        out_ref, ya_ref, yb_ref, sem, *, tm):
    i = pl.program_id(0)
    slot = i % 2

    def row_gather(src_row, buf, s, grp, c):
        return pltpu.make_async_copy(y_hbm.at[pl.ds(src_row, 1), :], buf.at[s, grp, pl.ds(c, 1), :], sem.at[s])

    def gather_tile(idx_ref, s):
        def issue(grp, carry):
            for c in range(SUBLANES):
                for k, buf in enumerate((ya_ref, yb_ref)):
                    row_gather(idx_ref[0, 0, 2 * SUBLANES * grp + 2 * c + k], buf, s, grp, c).start()
            return carry

        lax.fori_loop(0, tm // SUBLANES, issue, 0)

    @pl.when(i == 0)
    def _():
        gather_tile(dest_ref, 0)

    @pl.when(i + 1 < pl.num_programs(0))
    def _():
        gather_tile(dest_next_ref, 1 - slot)

    def drain(grp, carry):
        for _ in range(SUBLANES):
            for buf in (ya_ref, yb_ref):
                row_gather(0, buf, slot, 0, 0).wait()
        return carry

    lax.fori_loop(0, tm // SUBLANES, drain, 0)

    meta = meta_ref[...]
    ya = ya_ref[slot].reshape(tm, D_MODEL)
    yb = yb_ref[slot].reshape(tm, D_MODEL)
    x = x_ref[...] + (meta[:, 4:5] * ya + meta[:, 5:6] * yb)
    hn = _rms_rows(x, g_ref[...]).astype(BF16)
    gate = 1.0 / (1.0 + jnp.exp(-_dot(hn, wg_ref[...])))
    out_ref[...] = x + _dot(p_ref[...].astype(BF16), wp_ref[...]) * gate


def _combine_ple(dest, x2, meta, p2, ln, wg, wp, yout, tm):
    t = x2.shape[0]
    nt = t // tm
    row = lambda i: (i, 0)
    full = lambda shape: pl.BlockSpec(shape, lambda i: (0,) * len(shape))
    return pl.pallas_call(
        functools.partial(_combine_ple_kernel, tm=tm),
        grid_spec=pltpu.PrefetchScalarGridSpec(
            num_scalar_prefetch=0,
            grid=(nt,),
            in_specs=[pl.BlockSpec((1, 1, 2 * tm), lambda i: (i, 0, 0), memory_space=pltpu.SMEM),
                      pl.BlockSpec((1, 1, 2 * tm), lambda i: (jnp.minimum(i + 1, nt - 1), 0, 0),
                                   memory_space=pltpu.SMEM),
                      pl.BlockSpec((tm, D_MODEL), row), pl.BlockSpec((tm, LANES), row),
                      pl.BlockSpec((tm, PLE_DIM), row), full((1, D_MODEL)), full((D_MODEL, D_MODEL)),
                      full((PLE_DIM, D_MODEL)), pl.BlockSpec(memory_space=pl.ANY)],
            out_specs=pl.BlockSpec((tm, D_MODEL), row),
            scratch_shapes=[pltpu.VMEM((2, tm // SUBLANES, SUBLANES, D_MODEL), F32),
                            pltpu.VMEM((2, tm // SUBLANES, SUBLANES, D_MODEL), F32),
                            pltpu.SemaphoreType.DMA((2,))]),
        out_shape=jax.ShapeDtypeStruct((t, D_MODEL), F32),
        compiler_params=_params(("arbitrary",)),
        name="moe_combine_ple",
    )(dest, dest, x2, meta, p2, ln, wg, wp, yout)


MOE_TM = 512


def _moe_ple(x2, p2, ln_ffn, w_rg, b_rg, w_re, b_re, w1, w3, w2, ln_ple, w_gate, w_proj):
    t = x2.shape[0]
    tm = MOE_TM
    wr = jnp.concatenate([w_rg, w_re, jnp.zeros((D_MODEL, LANES - ROUTE_LANES), w_rg.dtype)], axis=1).astype(F32)
    wr_hi = wr.astype(BF16)
    wr_lo = (wr - wr_hi.astype(F32)).astype(BF16)
    br = jnp.concatenate([b_rg, b_re, jnp.zeros((LANES - ROUTE_LANES,), b_rg.dtype)]).astype(F32)[None, :]
    ii = jnp.arange(tm, dtype=jnp.int32)
    lstrict = jnp.where(ii[:, None] > ii[None, :], 1.0, 0.0).astype(BF16)
    meta, cnt = _router(x2, ln_ffn[None, :], jnp.stack([wr_hi, wr_lo]), br, lstrict, tm)

    counts = cnt[0, :N_EXPERTS].astype(jnp.int32)
    padded = (counts + MOE_BLOCK - 1) // MOE_BLOCK * MOE_BLOCK
    pad_end = jnp.cumsum(padded)
    start = (pad_end - padded).astype(jnp.int32)
    nb = (t * 2) // MOE_BLOCK + N_EXPERTS
    first_row = jnp.arange(nb, dtype=jnp.int32) * MOE_BLOCK
    block_e = jnp.minimum(jnp.sum((pad_end[None, :] <= first_row[:, None]).astype(jnp.int32), axis=1),
                          N_EXPERTS - 1).astype(jnp.int32)
    nused = (pad_end[-1:] // MOE_BLOCK).astype(jnp.int32)
    er = meta[:, :4].astype(jnp.int32)
    eid = jnp.arange(N_EXPERTS, dtype=jnp.int32)
    seg_start = jnp.sum(jnp.where(er[:, :2, None] == eid, start, 0), axis=-1)
    dest = (seg_start + er[:, 2:4]).reshape(t // tm, 1, 2 * tm)

    xin = _dispatch(pad_end.astype(jnp.int32), nused, dest, x2, ln_ffn[None, :], nb, tm)
    yout = _experts(block_e, nused, xin, w1.astype(BF16), w3.astype(BF16), w2.astype(BF16))
    return _combine_ple(dest, x2, meta, p2, ln_ple[None, :], w_gate.astype(BF16), w_proj.astype(BF16), yout, tm)


def _trunk(x, p, prm):
    bsz, seq, _ = x.shape
    x2 = x.reshape(bsz * seq, D_MODEL)
    for i in range(DEPTH):
        j = i // 2
        if i % 2 == 0:
            x2 = _even_mixer(x2, bsz, seq, prm["ln_mix_e"][j], prm["w_in_e"][j], prm["q_gain"][j],
                             prm["k_gain"][j], prm["sink"][j], prm["w_out_e"][j])
        else:
            x2 = _odd_mixer(x2, bsz, seq, prm["ln_mix_o"][j], prm["w_in_o"][j], prm["conv_w"][j],
                            prm["conv_b"][j], prm["dt_bias"][j], prm["a_log"][j], prm["d_skip"][j],
                            prm["ssm_gain"][j], prm["w_out_o"][j])
        x2 = _moe_ple(x2, p[i].reshape(bsz * seq, PLE_DIM), prm["ln_ffn"][i], prm["w_router_g"][i],
                      prm["b_router_g"][i], prm["w_router_e"][i], prm["b_router_e"][i], prm["w1"][i],
                      prm["w3"][i], prm["w2"][i], prm["ln_ple"][i], prm["w_ple_gate"][i], prm["w_ple_proj"][i])
    return x2.reshape(bsz, seq, D_MODEL)


def kernel(x_prompt, x_sample, p_prompt, p_sample, ln_mix_e, w_in_e, q_gain, k_gain, sink, w_out_e, ln_mix_o, w_in_o, conv_w, conv_b, dt_bias, a_log, d_skip, ssm_gain, w_out_o, ln_ffn, w_router_g, b_router_g, w_router_e, b_router_e, w1, w3, w2, ln_ple, w_ple_gate, w_ple_proj):
    prm = dict(ln_mix_e=ln_mix_e, w_in_e=w_in_e, q_gain=q_gain, k_gain=k_gain, sink=sink,
               w_out_e=w_out_e, ln_mix_o=ln_mix_o, w_in_o=w_in_o, conv_w=conv_w, conv_b=conv_b,
               dt_bias=dt_bias, a_log=a_log, d_skip=d_skip, ssm_gain=ssm_gain, w_out_o=w_out_o,
               ln_ffn=ln_ffn, w_router_g=w_router_g, b_router_g=b_router_g,
               w_router_e=w_router_e, b_router_e=b_router_e, w1=w1, w3=w3, w2=w2,
               ln_ple=ln_ple, w_ple_gate=w_ple_gate, w_ple_proj=w_ple_proj)
    return (_trunk(x_prompt, p_prompt, prm), _trunk(x_sample, p_sample, prm))
```

```python
import functools
import math

import jax
import jax.numpy as jnp
from jax import lax
from jax.experimental import pallas as pl
from jax.experimental.pallas import tpu as pltpu

D_MODEL = 1024
DEPTH = 4
RMS_EPS = 1e-6
N_FOURIER_GROUPS = 4
FOURIER_WIDTH = 512
FOURIER_GROUP = 128
FFT_N2 = 128
HEAD_DIM = 64
N_Q_HEADS = 8
N_KV_HEADS = 2
ATTN_WIDTH = 512
KV_WIDTH = 128
ATTN_BLOCK = 128
ROPE_THETA = 10000.0
D_INNER = 2048
SSM_HEAD_DIM = 64
SSM_HEADS = 32
SSM_STATE = 128
SSM_GROUPS = 4
HEADS_PER_GROUP = 8
CONV_K = 5
CHUNK = 128
GN = SSM_GROUPS * SSM_STATE
CONV_DIM = D_INNER + 2 * GN
N_EXPERT_GROUPS = 4
EXPERTS_PER_GROUP = 8
N_EXPERTS = 32
D_EXPERT = 512
MOE_BLOCK = 256
PLE_DIM = 256

LANES = 128
NEG_BIG = -1e30
VMEM_LIMIT = 52 * 1024 * 1024

F32 = jnp.float32
BF16 = jnp.bfloat16


def _params(sem, vmem=VMEM_LIMIT):
    return pltpu.CompilerParams(dimension_semantics=sem, vmem_limit_bytes=vmem)


def _dot(a, b):
    return jnp.dot(a, b, preferred_element_type=F32)


def _dot_nt(a, b):
    return lax.dot_general(a, b, (((1,), (1,)), ((), ())), preferred_element_type=F32)


def _split3(v):
    hi = v.astype(BF16)
    r1 = v - hi.astype(F32)
    mid = r1.astype(BF16)
    lo = (r1 - mid.astype(F32)).astype(BF16)
    return hi, mid, lo


def _dot3(v, m_bf16):
    hi, mid, lo = _split3(v)
    return _dot(hi, m_bf16) + _dot(mid, m_bf16) + _dot(lo, m_bf16)


def _dot3_left(m_bf16, v):
    hi, mid, lo = _split3(v)
    return _dot(m_bf16, hi) + _dot(m_bf16, mid) + _dot(m_bf16, lo)


def _rms_rows(x, g):
    return x * lax.rsqrt(jnp.mean(x * x, axis=-1, keepdims=True) + RMS_EPS) * g


def _silu(x):
    return x * (1.0 / (1.0 + jnp.exp(-x)))


def _full(shape):
    return pl.BlockSpec(shape, lambda *_: (0,) * len(shape))


def _even_in_kernel(x_ref, g_ref, w_ref, dft_ref, bd_ref, qg_ref, kg_ref, cos_ref, sin_ref,
                    gout_ref, q_ref, k_ref, v_ref):
    h = _rms_rows(x_ref[...], g_ref[...]).astype(BF16)
    u = _dot(h, w_ref[...])
    a = u[:, :FOURIER_WIDTH].astype(BF16)
    for g in range(N_FOURIER_GROUPS):
        r = _dot(a[:, g * LANES:(g + 1) * LANES], dft_ref[...])
        gout_ref[:, g * LANES:(g + 1) * LANES] = r[:, :LANES].astype(BF16)
        gout_ref[:, FOURIER_WIDTH + g * LANES:FOURIER_WIDTH + (g + 1) * LANES] = r[:, LANES:].astype(BF16)

    cos = cos_ref[...]
    sin = sin_ref[...]

    def norm_rope(t, gain, width):
        n = width // LANES
        bd = bd_ref[:width, :width]
        t2 = t * t
        hi = t2.astype(BF16)
        lo = (t2 - hi.astype(F32)).astype(BF16)
        ms = _dot(hi, bd) + _dot(lo, bd)
        tn = t * lax.rsqrt(ms + RMS_EPS) * gain
        lane = lax.broadcasted_iota(jnp.int32, tn.shape, 1)
        first_half = (lane % HEAD_DIM) < (HEAD_DIM // 2)
        rot = jnp.where(first_half, pltpu.roll(tn, width - HEAD_DIM // 2, 1), pltpu.roll(tn, HEAD_DIM // 2, 1))
        c = jnp.concatenate([cos] * n, axis=1)
        s = jnp.concatenate([sin] * n, axis=1)
        return tn * c + rot * s

    q = norm_rope(u[:, 512:1024], qg_ref[...], ATTN_WIDTH) * (HEAD_DIM ** -0.5)
    k = norm_rope(u[:, 1024:1280], kg_ref[...], 2 * KV_WIDTH)
    q_ref[...] = q.astype(BF16)
    k_ref[...] = k.astype(BF16)
    v_ref[...] = u[:, 1280:1536].astype(BF16)


def _even_in(x2, seq, ln, w, dft, bd, qg, kg, cos_t, sin_t, tm=512):
    t = x2.shape[0]
    nseq = seq // tm
    row = lambda i: (i, 0)
    pos = lambda i: (i % nseq, 0)
    return pl.pallas_call(
        _even_in_kernel,
        grid=(t // tm,),
        in_specs=[pl.BlockSpec((tm, D_MODEL), row), _full((1, D_MODEL)), _full((D_MODEL, 1536)),
                  _full((LANES, 2 * LANES)), _full((512, 512)), _full((1, 512)), _full((1, 256)),
                  pl.BlockSpec((tm, LANES), pos), pl.BlockSpec((tm, LANES), pos)],
        out_specs=[pl.BlockSpec((tm, 1024), row), pl.BlockSpec((tm, 512), row),
                   pl.BlockSpec((tm, 256), row), pl.BlockSpec((tm, 256), row)],
        out_shape=[jax.ShapeDtypeStruct((t, 1024), BF16), jax.ShapeDtypeStruct((t, 512), BF16),
                   jax.ShapeDtypeStruct((t, 256), BF16), jax.ShapeDtypeStruct((t, 256), BF16)],
        compiler_params=_params(("parallel",)),
        name="even_in",
    )(x2, ln, w, dft, bd, qg, kg, cos_t, sin_t)


def _fft1_kernel(g_ref, c_ref, s_ref, y_ref):
    g = g_ref[0]
    p = _dot(c_ref[...], g)
    q = _dot(s_ref[...], g)
    for j in range(g.shape[1] // 1024):
        o = j * 1024
        y_ref[0, :, o:o + 512] = (p[:, o:o + 512] + q[:, o + 512:o + 1024]).astype(BF16)
        y_ref[0, :, o + 512:o + 1024] = (p[:, o + 512:o + 1024] - q[:, o:o + 512]).astype(BF16)


def _fft1(g3, c1, s1, cb=8192):
    b, n1, width = g3.shape
    blk = lambda i, j: (i, 0, j)
    return pl.pallas_call(
        _fft1_kernel,
        grid=(b, width // cb),
        in_specs=[pl.BlockSpec((1, n1, cb), blk), _full((n1, n1)), _full((n1, n1))],
        out_specs=pl.BlockSpec((1, n1, cb), blk),
        out_shape=jax.ShapeDtypeStruct(g3.shape, BF16),
        compiler_params=_params(("parallel", "parallel")),
        name="fft_stage1",
    )(g3, c1, s1)


def _fft2_kernel(y_ref, mc_ref, ms_ref, o_ref, *, scale):
    for j in range(y_ref.shape[1]):
        yr = y_ref[0, j, :, :512]
        yi = y_ref[0, j, :, 512:]
        r = _dot(mc_ref[j], yr) + _dot(ms_ref[j], yi)
        o_ref[0, :, j * 512:(j + 1) * 512] = (r * scale).astype(BF16)


def _fft2(y4, mc, ms, scale, kb=8):
    b, n1, n2, _ = y4.shape
    return pl.pallas_call(
        functools.partial(_fft2_kernel, scale=scale),
        grid=(n1 // kb, b),
        in_specs=[pl.BlockSpec((1, kb, n2, 1024), lambda k, i: (i, k, 0, 0)),
                  pl.BlockSpec((kb, n2, n2), lambda k, i: (k, 0, 0)),
                  pl.BlockSpec((kb, n2, n2), lambda k, i: (k, 0, 0))],
        out_specs=pl.BlockSpec((1, n2, kb * 512), lambda k, i: (i, 0, k)),
        out_shape=jax.ShapeDtypeStruct((b, n2, n1 * 512), BF16),
        compiler_params=_params(("parallel", "parallel")),
        name="fft_stage2",
    )(y4, mc, ms)


def _dft_tables(seq):
    n1, n2 = seq // FFT_N2, FFT_N2
    i1 = jnp.arange(n1, dtype=jnp.int32)
    ang1 = ((i1[:, None] * i1[None, :]) % n1).astype(F32) * (2.0 * math.pi / n1)
    c1, s1 = jnp.cos(ang1).astype(BF16), jnp.sin(ang1).astype(BF16)
    i2 = jnp.arange(n2, dtype=jnp.int32)
    ph = (i2[None, None, :] * i1[:, None, None] + n1 * i2[None, :, None] * i2[None, None, :]) % seq
    ang2 = ph.astype(F32) * (2.0 * math.pi / seq)
    mc, ms = jnp.cos(ang2).astype(BF16), jnp.sin(ang2).astype(BF16)
    ic = jnp.arange(FOURIER_GROUP, dtype=jnp.int32)
    angc = ((ic[:, None] * ic[None, :]) % FOURIER_GROUP).astype(F32) * (2.0 * math.pi / FOURIER_GROUP)
    dftc = jnp.concatenate([jnp.cos(angc), -jnp.sin(angc)], axis=1).astype(BF16)
    return c1, s1, mc, ms, dftc


def _attn_kernel(sink_ref, q_ref, kp_ref, ko_ref, kn_ref, vp_ref, vo_ref, vn_ref, o_ref, kc_ref, vc_ref):
    i = pl.program_id(1)
    last = pl.num_programs(1) - 1
    tq = q_ref.shape[0]
    nb = tq // ATTN_BLOCK
    kc_ref[0:ATTN_BLOCK] = kp_ref[...]
    kc_ref[ATTN_BLOCK:ATTN_BLOCK + tq] = ko_ref[...]
    kc_ref[ATTN_BLOCK + tq:] = kn_ref[...]
    vc_ref[0:ATTN_BLOCK] = vp_ref[...]
    vc_ref[ATTN_BLOCK:ATTN_BLOCK + tq] = vo_ref[...]
    vc_ref[ATTN_BLOCK + tq:] = vn_ref[...]

    w3 = 3 * ATTN_BLOCK
    qpos = lax.broadcasted_iota(jnp.int32, (ATTN_BLOCK, w3), 0) + ATTN_BLOCK
    kpos = lax.broadcasted_iota(jnp.int32, (ATTN_BLOCK, w3), 1)
    band = jnp.abs(kpos - qpos) <= ATTN_BLOCK
    lane = lax.broadcasted_iota(jnp.int32, (w3, LANES), 1)
    lo_half = lane < HEAD_DIM
    out_lane = lax.broadcasted_iota(jnp.int32, (ATTN_BLOCK, LANES), 1) < HEAD_DIM

    for jb in range(nb):
        ok = band
        if jb == 0:
            ok = ok & ((kpos >= ATTN_BLOCK) | (i > 0))
        if jb == nb - 1:
            ok = ok & ((kpos < 2 * ATTN_BLOCK) | (i < last))
        bias = jnp.where(ok, 0.0, NEG_BIG).astype(F32)
        qb = q_ref[jb * ATTN_BLOCK:(jb + 1) * ATTN_BLOCK, :]
        kw = kc_ref[jb * ATTN_BLOCK:jb * ATTN_BLOCK + w3, :]
        vw = vc_ref[jb * ATTN_BLOCK:jb * ATTN_BLOCK + w3, :]
        for g in range(N_KV_HEADS):
            kg = kw[:, g * LANES:(g + 1) * LANES]
            vg = vw[:, g * LANES:(g + 1) * LANES]
            zero = jnp.zeros_like(kg)
            kbd = jnp.concatenate([jnp.where(lo_half, kg, zero), jnp.where(lo_half, zero, kg)], axis=0)
            vbd = jnp.concatenate([jnp.where(lo_half, vg, zero), jnp.where(lo_half, zero, vg)], axis=0)
            for pr in range(2):
                col = (2 * g + pr) * LANES
                s = _dot_nt(qb[:, col:col + LANES], kbd)
                es, rs = [], []
                for half in range(2):
                    sk = sink_ref[4 * g + 2 * pr + half]
                    sh = s[:, half * w3:(half + 1) * w3] + bias
                    m = jnp.maximum(jnp.max(sh, axis=-1, keepdims=True), sk)
                    e = jnp.exp(sh - m)
                    den = jnp.sum(e, axis=-1, keepdims=True) + jnp.exp(sk - m)
                    es.append(e.astype(BF16))
                    rs.append(1.0 / den)
                o = _dot(jnp.concatenate(es, axis=1), vbd)
                o = o * jnp.where(out_lane, rs[0], rs[1])
                o_ref[jb * ATTN_BLOCK:(jb + 1) * ATTN_BLOCK, col:col + LANES] = o.astype(BF16)


def _attention(q, k, v, sink, bsz, seq, tq=512):
    t = q.shape[0]
    nq = seq // tq
    r = tq // ATTN_BLOCK
    nblk = t // ATTN_BLOCK
    own = lambda b, i, s: (b * nq + i, 0)
    prev = lambda b, i, s: (jnp.maximum((b * nq + i) * r - 1, 0), 0)
    nxt = lambda b, i, s: (jnp.minimum((b * nq + i + 1) * r, nblk - 1), 0)
    kv = lambda m: pl.BlockSpec((ATTN_BLOCK if m is not own else tq, 256), m)
    return pl.pallas_call(
        _attn_kernel,
        grid_spec=pltpu.PrefetchScalarGridSpec(
            num_scalar_prefetch=1,
            grid=(bsz, nq),
            in_specs=[pl.BlockSpec((tq, 512), own), kv(prev), kv(own), kv(nxt), kv(prev), kv(own), kv(nxt)],
            out_specs=pl.BlockSpec((tq, 512), own),
            scratch_shapes=[pltpu.VMEM((tq + 2 * ATTN_BLOCK, 256), BF16),
                            pltpu.VMEM((tq + 2 * ATTN_BLOCK, 256), BF16)]),
        out_shape=jax.ShapeDtypeStruct((t, 512), BF16),
        compiler_params=_params(("parallel", "parallel")),
        name="window_attention",
    )(sink, q, k, k, k, v, v, v)


def _even_out_kernel(x_ref, f_ref, o_ref, w_ref, out_ref):
    out_ref[...] = (x_ref[...] + _dot(f_ref[...], w_ref[:512, :]) + _dot(o_ref[...], w_ref[512:, :]))


def _even_out(x2, f, o, w, tm=512):
    t = x2.shape[0]
    row = lambda i: (i, 0)
    return pl.pallas_call(
        _even_out_kernel,
        grid=(t // tm,),
        in_specs=[pl.BlockSpec((tm, D_MODEL), row), pl.BlockSpec((tm, 512), row),
                  pl.BlockSpec((tm, 512), row), _full((1024, D_MODEL))],
        out_specs=pl.BlockSpec((tm, D_MODEL), row),
        out_shape=jax.ShapeDtypeStruct((t, D_MODEL), F32),
        compiler_params=_params(("parallel",)),
        name="even_out",
    )(x2, f, o, w)


def _even_mixer(x2, bsz, seq, ln, w_in, q_gain, k_gain, sink, w_out):
    c1, s1, mc, ms, dftc = _dft_tables(seq)
    n1 = seq // FFT_N2
    kcols = w_in[:, 1024:1152]
    vcols = w_in[:, 1152:1280]
    dup = lambda c: jnp.concatenate([c[:, :64], c[:, :64], c[:, 64:], c[:, 64:]], axis=1)
    w = jnp.concatenate([w_in[:, :1024], dup(kcols), dup(vcols)], axis=1).astype(BF16)
    hid = jnp.arange(512, dtype=jnp.int32) // HEAD_DIM
    bd = jnp.where(hid[:, None] == hid[None, :], 1.0 / HEAD_DIM, 0.0).astype(BF16)
    inv = 1.0 / (ROPE_THETA ** (jnp.arange(0, HEAD_DIM, 2, dtype=F32) / HEAD_DIM))
    ang = jnp.arange(seq, dtype=F32)[:, None] * inv[None, :]
    cos, sin = jnp.cos(ang), jnp.sin(ang)
    cos_t = jnp.concatenate([cos, cos, cos, cos], axis=1)
    sin_t = jnp.concatenate([-sin, sin, -sin, sin], axis=1)
    qg = jnp.tile(q_gain.astype(F32), N_Q_HEADS)[None, :]
    kg = jnp.tile(k_gain.astype(F32), 2 * N_KV_HEADS)[None, :]
    g, q, k, v = _even_in(x2, seq, ln[None, :], w, dftc, bd, qg, kg, cos_t, sin_t)
    y = _fft1(g.reshape(bsz, n1, FFT_N2 * 1024), c1, s1)
    scale = 1.0 / math.sqrt(seq * FOURIER_GROUP)
    f = _fft2(y.reshape(bsz, n1, FFT_N2, 1024), mc, ms, scale).reshape(bsz * seq, 512)
    o = _attention(q, k, v, sink.astype(F32), bsz, seq)
    return _even_out(x2, f, o, w_out.astype(BF16))


def _odd_in_kernel(x_ref, g_ref, w_ref, z_ref, xbc_ref, dt_ref):
    h = _rms_rows(x_ref[...], g_ref[...]).astype(BF16)
    z_ref[...] = _dot(h, w_ref[:, :D_INNER]).astype(BF16)
    xbc_ref[...] = _dot(h, w_ref[:, D_INNER:D_INNER + CONV_DIM]).astype(BF16)
    dt_ref[...] = _dot(h, w_ref[:, D_INNER + CONV_DIM:])


def _odd_in(x2, ln, w, tm=256):
    t = x2.shape[0]
    row = lambda i: (i, 0)
    wcols = D_INNER + CONV_DIM + LANES
    return pl.pallas_call(
        _odd_in_kernel,
        grid=(t // tm,),
        in_specs=[pl.BlockSpec((tm, D_MODEL), row), _full((1, D_MODEL)), _full((D_MODEL, wcols))],
        out_specs=[pl.BlockSpec((tm, D_INNER), row), pl.BlockSpec((tm, CONV_DIM), row),
                   pl.BlockSpec((tm, LANES), row)],
        out_shape=[jax.ShapeDtypeStruct((t, D_INNER), BF16), jax.ShapeDtypeStruct((t, CONV_DIM), BF16),
                   jax.ShapeDtypeStruct((t, LANES), F32)],
        compiler_params=_params(("parallel",)),
        name="odd_in",
    )(x2, ln, w)


CONV_HALO = 16


def _conv_kernel(prev_ref, main_ref, next_ref, w_ref, b_ref, o_ref, win_ref, *, tiles_per_seq):
    i = pl.program_id(0) % tiles_per_seq
    tc = main_ref.shape[0]
    keep_prev = (i > 0).astype(F32)
    keep_next = (i < tiles_per_seq - 1).astype(F32)
    win_ref[0:CONV_HALO, :] = prev_ref[...].astype(F32) * keep_prev
    win_ref[CONV_HALO:CONV_HALO + tc, :] = main_ref[...].astype(F32)
    win_ref[CONV_HALO + tc:, :] = next_ref[...].astype(F32) * keep_next
    cb = 512
    for c in range(CONV_DIM // cb):
        cols = slice(c * cb, (c + 1) * cb)
        win = win_ref[:, cols]
        acc = jnp.broadcast_to(b_ref[:, cols], (tc, cb))
        for k in range(CONV_K):
            shift = CONV_K // 2 - k
            rolled = win if shift == 0 else pltpu.roll(win, shift % win.shape[0], 0)
            acc = acc + rolled[CONV_HALO:CONV_HALO + tc, :] * w_ref[k:k + 1, cols]
        o_ref[:, cols] = _silu(acc).astype(BF16)


def _conv(xbc, seq, w, b, tc=512):
    t = xbc.shape[0]
    r = tc // CONV_HALO
    nh = t // CONV_HALO
    return pl.pallas_call(
        functools.partial(_conv_kernel, tiles_per_seq=seq // tc),
        grid=(t // tc,),
        in_specs=[pl.BlockSpec((CONV_HALO, CONV_DIM), lambda i: (jnp.maximum(i * r - 1, 0), 0)),
                  pl.BlockSpec((tc, CONV_DIM), lambda i: (i, 0)),
                  pl.BlockSpec((CONV_HALO, CONV_DIM), lambda i: (jnp.minimum((i + 1) * r, nh - 1), 0)),
                  _full((CONV_K, CONV_DIM)), _full((1, CONV_DIM))],
        out_specs=pl.BlockSpec((tc, CONV_DIM), lambda i: (i, 0)),
        out_shape=jax.ShapeDtypeStruct((t, CONV_DIM), BF16),
        scratch_shapes=[pltpu.VMEM((tc + 2 * CONV_HALO, CONV_DIM), F32)],
        compiler_params=_params(("parallel",)),
        name="ssd_conv",
    )(xbc, xbc, xbc, w, b)


N_PAIRS = SSM_HEADS // 2
PAIRS_PER_GROUP = HEADS_PER_GROUP // 2


LOG2E = math.log2(math.e)
SSD_STEP = 2 * CHUNK


def _ssd_chunk(xbc_ref, dt_ref, bias, a_row, expand, y_ref, state_ref, r0, reverse, mask, tri_col, lo_half):
    off = SSM_HEADS if reverse else 0
    rows = slice(r0, r0 + CHUNK)
    dt_lh = dt_ref[rows, :] + bias
    dt_lh = jnp.maximum(dt_lh, 0.0) + jnp.log(1.0 + jnp.exp(-jnp.abs(dt_lh)))
    acs_lh = _dot3_left(tri_col, dt_lh * a_row)
    edge = 0 if reverse else CHUNK - 1
    acs2_lh = acs_lh * LOG2E
    tot2_row = acs2_lh[edge:edge + 1, :]
    r_hl = acs2_lh.T
    dt_hi = dt_lh.astype(BF16)
    dt_mid = (dt_lh - dt_hi.astype(F32)).astype(BF16)
    dt_exp = _dot(dt_hi, expand) + _dot(dt_mid, expand)
    lo_row = lo_half[0:1, :]

    for g in range(SSM_GROUPS):
        b_g = xbc_ref[rows, D_INNER + g * SSM_STATE:D_INNER + (g + 1) * SSM_STATE]
        c_g = xbc_ref[rows, D_INNER + GN + g * SSM_STATE:D_INNER + GN + (g + 1) * SSM_STATE]
        cb = _dot_nt(c_g, b_g)
        bt = b_g.astype(F32).T.astype(BF16)
        gcols = slice(g * PAIRS_PER_GROUP * LANES, (g + 1) * PAIRS_PER_GROUP * LANES)
        y_off = _dot(c_g, state_ref[:, gcols].astype(BF16))
        for pp in range(PAIRS_PER_GROUP):
            p = g * PAIRS_PER_GROUP + pp
            hd0, hd1 = off + 2 * p, off + 2 * p + 1
            colb = [jnp.broadcast_to(acs2_lh[:, hd:hd + 1], (CHUNK, CHUNK)) for hd in (hd0, hd1)]
            ms = [(cb * jnp.exp2(jnp.where(mask, cb_l - r_hl[hd:hd + 1, :], NEG_BIG))).astype(BF16)
                  for cb_l, hd in zip(colb, (hd0, hd1))]
            pcols = slice(p * LANES, (p + 1) * LANES)
            xdt = xbc_ref[rows, pcols].astype(F32) * dt_exp[:, pcols]
            xdt_b = xdt.astype(BF16)
            zx = jnp.zeros_like(xdt_b)
            xbd = jnp.concatenate([jnp.where(lo_half, xdt_b, zx), jnp.where(lo_half, zx, xdt_b)], axis=0)
            colsel = jnp.where(lo_half, colb[0], colb[1])
            y = _dot(jnp.concatenate(ms, axis=1), xbd)
            y = y + jnp.exp2(colsel) * y_off[:, pp * LANES:(pp + 1) * LANES]
            y_ref[rows, pcols] = y.astype(BF16)
            tot2 = jnp.where(lo_row, tot2_row[:, hd0:hd0 + 1], tot2_row[:, hd1:hd1 + 1])
            snew = _dot(bt, (xdt * jnp.exp2(tot2 - colsel)).astype(BF16))
            state_ref[:, pcols] = state_ref[:, pcols] * jnp.exp2(tot2) + snew


def _ssd_kernel(xbc_ref, dt_ref, bias_ref, alog_ref, exp_ref, y_ref, state_ref, *, reverse):
    @pl.when(pl.program_id(1) == 0)
    def _():
        state_ref[...] = jnp.zeros_like(state_ref)

    r_io = lax.broadcasted_iota(jnp.int32, (CHUNK, CHUNK), 0)
    c_io = lax.broadcasted_iota(jnp.int32, (CHUNK, CHUNK), 1)
    mask = (r_io <= c_io) if reverse else (r_io >= c_io)
    tri_col = jnp.where(mask, 1.0, 0.0).astype(BF16)
    lo_half = lax.broadcasted_iota(jnp.int32, (CHUNK, LANES), 1) < SSM_HEAD_DIM
    bias = bias_ref[...]
    a_row = -jnp.exp(alog_ref[...])
    n_sub = SSD_STEP // CHUNK
    order = range(n_sub - 1, -1, -1) if reverse else range(n_sub)
    for sub in order:
        _ssd_chunk(xbc_ref, dt_ref, bias, a_row, exp_ref[...], y_ref, state_ref, sub * CHUNK, reverse,
                   mask, tri_col, lo_half)


def _ssd(xbc_c, dt_raw, bias_row, alog_row, expand, bsz, seq, reverse):
    t = xbc_c.shape[0]
    ns = seq // SSD_STEP
    if reverse:
        blk = lambda b, c: (b * ns + ns - 1 - c, 0)
    else:
        blk = lambda b, c: (b * ns + c, 0)
    return pl.pallas_call(
        functools.partial(_ssd_kernel, reverse=reverse),
        grid=(bsz, ns),
        in_specs=[pl.BlockSpec((SSD_STEP, CONV_DIM), blk), pl.BlockSpec((SSD_STEP, LANES), blk),
                  _full((1, LANES)), _full((1, LANES)), _full((LANES, D_INNER))],
        out_specs=pl.BlockSpec((SSD_STEP, D_INNER), blk),
        out_shape=jax.ShapeDtypeStruct((t, D_INNER), BF16),
        scratch_shapes=[pltpu.VMEM((SSM_STATE, D_INNER), F32)],
        compiler_params=_params(("parallel", "arbitrary")),
        name="ssd_scan_bwd" if reverse else "ssd_scan_fwd",
    )(xbc_c, dt_raw, bias_row, alog_row, expand)


def _odd_out_kernel(x_ref, yf_ref, yb_ref, xs_ref, z_ref, d_ref, g_ref, w_ref, out_ref):
    y = yf_ref[...].astype(F32) + yb_ref[...].astype(F32) + d_ref[...] * xs_ref[...].astype(F32)
    y = y * _silu(z_ref[...].astype(F32))
    gw = D_INNER // SSM_GROUPS
    parts = []
    for g in range(SSM_GROUPS):
        yg = y[:, g * gw:(g + 1) * gw]
        parts.append(yg * lax.rsqrt(jnp.mean(yg * yg, axis=-1, keepdims=True) + RMS_EPS))
    yn = (jnp.concatenate(parts, axis=1) * g_ref[...]).astype(BF16)
    out_ref[...] = x_ref[...] + _dot(yn, w_ref[...])


def _odd_out(x2, yf, yb, xbc_c, z, dskip, gain, w, tm=512):
    t = x2.shape[0]
    row = lambda i: (i, 0)
    return pl.pallas_call(
        _odd_out_kernel,
        grid=(t // tm,),
        in_specs=[pl.BlockSpec((tm, D_MODEL), row), pl.BlockSpec((tm, D_INNER), row),
                  pl.BlockSpec((tm, D_INNER), row), pl.BlockSpec((tm, D_INNER), row),
                  pl.BlockSpec((tm, D_INNER), row), _full((1, D_INNER)), _full((1, D_INNER)),
                  _full((D_INNER, D_MODEL))],
        out_specs=pl.BlockSpec((tm, D_MODEL), row),
        out_shape=jax.ShapeDtypeStruct((t, D_MODEL), F32),
        compiler_params=_params(("parallel",)),
        name="odd_out",
    )(x2, yf, yb, xbc_c, z, dskip, gain, w)


def _odd_mixer(x2, bsz, seq, ln, w_in, conv_w, conv_b, dt_bias, a_log, d_skip, norm_g, w_out):
    pad = jnp.zeros((D_MODEL, LANES - 2 * SSM_HEADS), w_in.dtype)
    w = jnp.concatenate([w_in, pad], axis=1).astype(BF16)
    z, xbc, dt_raw = _odd_in(x2, ln[None, :], w)
    xbc_c = _conv(xbc, seq, conv_w.astype(F32), conv_b.astype(F32)[None, :])
    lane_pad = jnp.zeros((LANES - 2 * SSM_HEADS,), F32)
    bias_row = jnp.concatenate([dt_bias.astype(F32).reshape(-1), lane_pad])[None, :]
    alog_row = jnp.concatenate([a_log.astype(F32).reshape(-1), lane_pad])[None, :]
    ch = jnp.arange(D_INNER, dtype=jnp.int32) // SSM_HEAD_DIM
    hd = jnp.arange(LANES, dtype=jnp.int32)
    ys = []
    for reverse in (False, True):
        off = SSM_HEADS if reverse else 0
        expand = jnp.where(hd[:, None] == off + ch[None, :], 1.0, 0.0).astype(BF16)
        ys.append(_ssd(xbc_c, dt_raw, bias_row, alog_row, expand, bsz, seq, reverse))
    dskip = jnp.repeat(d_skip.astype(F32), SSM_HEAD_DIM)[None, :]
    return _odd_out(x2, ys[0], ys[1], xbc_c, z, dskip, norm_g.astype(F32)[None, :], w_out.astype(BF16))


ROUTE_LANES = N_EXPERT_GROUPS + N_EXPERTS


def _router_kernel(x_ref, g_ref, w_ref, b_ref, ls_ref, meta_ref, cnt_ref, run_ref):
    i = pl.program_id(0)

    @pl.when(i == 0)
    def _():
        run_ref[...] = jnp.zeros_like(run_ref)

    h = _rms_rows(x_ref[...], g_ref[...])
    hh = h.astype(BF16)
    hl = (h - hh.astype(F32)).astype(BF16)
    w_hi = w_ref[0]
    w_lo = w_ref[1]
    logits = _dot(hh, w_hi) + _dot(hl, w_hi) + _dot(hh, w_lo) + b_ref[...]
    tm = logits.shape[0]
    lane = lax.broadcasted_iota(jnp.int32, (tm, LANES), 1).astype(F32)

    def top1(v):
        m = jnp.max(v, axis=-1, keepdims=True)
        idx = jnp.min(jnp.where(v == m, lane, float(LANES)), axis=-1, keepdims=True)
        return m, idx

    gl = jnp.where(lane < N_EXPERT_GROUPS, logits, NEG_BIG)
    gmax, gidx = top1(gl)
    g_p = 1.0 / jnp.sum(jnp.exp(gl - gmax), axis=-1, keepdims=True)
    lo = N_EXPERT_GROUPS + EXPERTS_PER_GROUP * gidx
    e1 = jnp.where((lane >= lo) & (lane < lo + EXPERTS_PER_GROUP), logits, NEG_BIG)
    m1, i1 = top1(e1)
    e2 = jnp.where(lane == i1, NEG_BIG, e1)
    m2, i2 = top1(e2)
    r = jnp.exp(m2 - m1)
    w_a = g_p / (1.0 + r)
    w_b = g_p * r / (1.0 + r)
    e_a = i1 - N_EXPERT_GROUPS
    e_b = i2 - N_EXPERT_GROUPS

    onehot = jnp.where((lane == e_a) | (lane == e_b), 1.0, 0.0)
    before = _dot(ls_ref[...], onehot.astype(BF16)) + run_ref[...]
    rank_a = jnp.sum(jnp.where(lane == e_a, before, 0.0), axis=-1, keepdims=True)
    rank_b = jnp.sum(jnp.where(lane == e_b, before, 0.0), axis=-1, keepdims=True)
    run_ref[...] = run_ref[...] + jnp.sum(onehot, axis=0, keepdims=True)
    cnt_ref[...] = jnp.broadcast_to(run_ref[...], cnt_ref.shape)

    meta = jnp.where(lane == 0, e_a, 0.0)
    meta = jnp.where(lane == 1, e_b, meta)
    meta = jnp.where(lane == 2, rank_a, meta)
    meta = jnp.where(lane == 3, rank_b, meta)
    meta = jnp.where(lane == 4, w_a, meta)
    meta = jnp.where(lane == 5, w_b, meta)
    meta_ref[...] = meta


def _router(x2, ln, w2, b, lstrict, tm):
    t = x2.shape[0]
    row = lambda i: (i, 0)
    return pl.pallas_call(
        _router_kernel,
        grid=(t // tm,),
        in_specs=[pl.BlockSpec((tm, D_MODEL), row), _full((1, D_MODEL)), _full((2, D_MODEL, LANES)),
                  _full((1, LANES)), _full((tm, tm))],
        out_specs=[pl.BlockSpec((tm, LANES), row), _full((8, LANES))],
        out_shape=[jax.ShapeDtypeStruct((t, LANES), F32), jax.ShapeDtypeStruct((8, LANES), F32)],
        scratch_shapes=[pltpu.VMEM((1, LANES), F32)],
        compiler_params=_params(("arbitrary",)),
        name="moe_router",
    )(x2, ln, w2, b, lstrict)


SUBLANES = 8
PACKED = D_MODEL // 2
U32 = jnp.uint32


def _pack_rows(v):
    bits = lax.bitcast_convert_type(v.astype(BF16).astype(F32), U32)
    return (bits[:, :PACKED] >> 16) | bits[:, PACKED:]


def _unpack_rows(w):
    lo = lax.bitcast_convert_type(w << 16, F32)
    hi = lax.bitcast_convert_type(w & jnp.uint32(0xFFFF0000), F32)
    return jnp.concatenate([lo, hi], axis=1)


def _dispatch_kernel(pad_end_ref, nused_ref, dest_ref, x_ref, g_ref, xin_hbm, h_ref, zero_ref, sem, zsem,
                     *, tm, nb):
    def zero_block(first_row):
        return pltpu.make_async_copy(zero_ref, xin_hbm.at[pl.ds(pl.multiple_of(first_row, MOE_BLOCK), MOE_BLOCK), :],
                                     zsem)

    def seg_zero(action):
        for e in range(N_EXPERTS):
            end = pad_end_ref[e]
            prev = pad_end_ref[e - 1] if e else 0

            @pl.when(end > prev)
            def _():
                action(zero_block(end - MOE_BLOCK))

    def tail_zero(action):
        def body(b, carry):
            action(zero_block(b * MOE_BLOCK))
            return carry

        lax.fori_loop(nused_ref[0], nb, body, 0)

    @pl.when(pl.program_id(0) == 0)
    def _():
        zero_ref[...] = jnp.zeros_like(zero_ref)
        seg_zero(lambda c: c.start())
        tail_zero(lambda c: c.start())
        seg_zero(lambda c: c.wait())
        tail_zero(lambda c: c.wait())

    h_ref[...] = _pack_rows(_rms_rows(x_ref[...], g_ref[...])).reshape(h_ref.shape)

    def row_copy(grp, c, dst_row):
        return pltpu.make_async_copy(h_ref.at[grp, pl.ds(c, 1), :], xin_hbm.at[pl.ds(dst_row, 1), :], sem)

    def issue(grp, carry):
        for c in range(SUBLANES):
            for k in range(2):
                row_copy(grp, c, dest_ref[0, 0, 2 * SUBLANES * grp + 2 * c + k]).start()
        return carry

    lax.fori_loop(0, tm // SUBLANES, issue, 0)

    def drain(grp, carry):
        for _ in range(2 * SUBLANES):
            row_copy(0, 0, 0).wait()
        return carry

    lax.fori_loop(0, tm // SUBLANES, drain, 0)


def _dispatch(pad_end, nused, dest, x2, ln, nb, tm):
    t = x2.shape[0]
    return pl.pallas_call(
        functools.partial(_dispatch_kernel, tm=tm, nb=nb),
        grid_spec=pltpu.PrefetchScalarGridSpec(
            num_scalar_prefetch=2,
            grid=(t // tm,),
            in_specs=[pl.BlockSpec((1, 1, 2 * tm), lambda i, pe, nu: (i, 0, 0), memory_space=pltpu.SMEM),
                      pl.BlockSpec((tm, D_MODEL), lambda i, pe, nu: (i, 0)),
                      pl.BlockSpec((1, D_MODEL), lambda i, pe, nu: (0, 0))],
            out_specs=pl.BlockSpec(memory_space=pl.ANY),
            scratch_shapes=[pltpu.VMEM((tm // SUBLANES, SUBLANES, PACKED), U32),
                            pltpu.VMEM((MOE_BLOCK, PACKED), U32),
                            pltpu.SemaphoreType.DMA, pltpu.SemaphoreType.DMA]),
        out_shape=jax.ShapeDtypeStruct((nb * MOE_BLOCK, PACKED), U32),
        compiler_params=_params(("arbitrary",)),
        name="moe_dispatch",
    )(pad_end, nused, dest, x2, ln)


EXPERT_BLOCKS_PER_STEP = 2


def _experts_kernel(be_ref, nused_ref, x_ref, w1a_ref, w3a_ref, w2a_ref, w1b_ref, w3b_ref, w2b_ref, o_ref):
    i = pl.program_id(0)
    first = EXPERT_BLOCKS_PER_STEP * i

    def swiglu(rows, w1_ref, w3_ref, w2_ref):
        x = _unpack_rows(x_ref[rows, :]).astype(BF16)
        a = _silu(_dot(x, w1_ref[0])) * _dot(x, w3_ref[0])
        o_ref[rows, :] = _pack_rows(_dot(a.astype(BF16), w2_ref[0]))

    fused = (first + 1 < nused_ref[0]) & (be_ref[first] == be_ref[first + 1])

    @pl.when(fused)
    def _():
        swiglu(slice(0, EXPERT_BLOCKS_PER_STEP * MOE_BLOCK), w1a_ref, w3a_ref, w2a_ref)

    for half, wrefs in enumerate(((w1a_ref, w3a_ref, w2a_ref), (w1b_ref, w3b_ref, w2b_ref))):
        rows = slice(half * MOE_BLOCK, (half + 1) * MOE_BLOCK)
        live = first + half < nused_ref[0]

        @pl.when(jnp.logical_not(fused) & live)
        def _():
            swiglu(rows, *wrefs)

        @pl.when(jnp.logical_not(live))
        def _():
            o_ref[rows, :] = jnp.zeros((MOE_BLOCK, PACKED), U32)


def _experts(block_e, nused, xin, w1, w3, w2):
    n = EXPERT_BLOCKS_PER_STEP
    nb = xin.shape[0] // MOE_BLOCK
    assert nb % n == 0
    row = lambda i, be, nu: (i, 0)
    specs = []
    for half in range(n):
        wsel = functools.partial(lambda i, be, nu, half: (be[n * i + half], 0, 0), half=half)
        specs += [pl.BlockSpec((1, D_MODEL, D_EXPERT), wsel), pl.BlockSpec((1, D_MODEL, D_EXPERT), wsel),
                  pl.BlockSpec((1, D_EXPERT, D_MODEL), wsel)]
    return pl.pallas_call(
        _experts_kernel,
        grid_spec=pltpu.PrefetchScalarGridSpec(
            num_scalar_prefetch=2,
            grid=(nb // n,),
            in_specs=[pl.BlockSpec((n * MOE_BLOCK, PACKED), row)] + specs,
            out_specs=pl.BlockSpec((n * MOE_BLOCK, PACKED), row)),
        out_shape=jax.ShapeDtypeStruct(xin.shape, U32),
        compiler_params=_params(("parallel",)),
        name="moe_experts",
    )(block_e, nused, xin, *([w1, w3, w2] * n))


def _combine_ple_kernel(dest_ref, dest_next_ref, x_ref, meta_ref, p_ref, g_ref, wg_ref, wp_ref, y_hbm,
                        out_ref, ya_ref, yb_ref, sem, *, tm):
    i = pl.program_id(0)
    slot = i % 2

    def row_gather(src_row, buf, s, grp, c):
        return pltpu.make_async_copy(y_hbm.at[pl.ds(src_row, 1), :], buf.at[s, grp, pl.ds(c, 1), :], sem.at[s])

    def gather_tile(idx_ref, s):
        def issue(grp, carry):
            for c in range(SUBLANES):
                for k, buf in enumerate((ya_ref, yb_ref)):
                    row_gather(idx_ref[0, 0, 2 * SUBLANES * grp + 2 * c + k], buf, s, grp, c).start()
            return carry

        lax.fori_loop(0, tm // SUBLANES, issue, 0)

    @pl.when(i == 0)
    def _():
        gather_tile(dest_ref, 0)

    @pl.when(i + 1 < pl.num_programs(0))
    def _():
        gather_tile(dest_next_ref, 1 - slot)

    def drain(grp, carry):
        for _ in range(SUBLANES):
            for buf in (ya_ref, yb_ref):
                row_gather(0, buf, slot, 0, 0).wait()
        return carry

    lax.fori_loop(0, tm // SUBLANES, drain, 0)

    meta = meta_ref[...]
    ya = _unpack_rows(ya_ref[slot].reshape(tm, PACKED))
    yb = _unpack_rows(yb_ref[slot].reshape(tm, PACKED))
    x = x_ref[...] + (meta[:, 4:5] * ya + meta[:, 5:6] * yb)
    hn = _rms_rows(x, g_ref[...]).astype(BF16)
    gate = 1.0 / (1.0 + jnp.exp(-_dot(hn, wg_ref[...])))
    out_ref[...] = x + _dot(p_ref[...].astype(BF16), wp_ref[...]) * gate


def _combine_ple(dest, x2, meta, p2, ln, wg, wp, yout, tm):
    t = x2.shape[0]
    nt = t // tm
    row = lambda i: (i, 0)
    full = lambda shape: pl.BlockSpec(shape, lambda i: (0,) * len(shape))
    return pl.pallas_call(
        functools.partial(_combine_ple_kernel, tm=tm),
        grid_spec=pltpu.PrefetchScalarGridSpec(
            num_scalar_prefetch=0,
            grid=(nt,),
            in_specs=[pl.BlockSpec((1, 1, 2 * tm), lambda i: (i, 0, 0), memory_space=pltpu.SMEM),
                      pl.BlockSpec((1, 1, 2 * tm), lambda i: (jnp.minimum(i + 1, nt - 1), 0, 0),
                                   memory_space=pltpu.SMEM),
                      pl.BlockSpec((tm, D_MODEL), row), pl.BlockSpec((tm, LANES), row),
                      pl.BlockSpec((tm, PLE_DIM), row), full((1, D_MODEL)), full((D_MODEL, D_MODEL)),
                      full((PLE_DIM, D_MODEL)), pl.BlockSpec(memory_space=pl.ANY)],
            out_specs=pl.BlockSpec((tm, D_MODEL), row),
            scratch_shapes=[pltpu.VMEM((2, tm // SUBLANES, SUBLANES, PACKED), U32),
                            pltpu.VMEM((2, tm // SUBLANES, SUBLANES, PACKED), U32),
                            pltpu.SemaphoreType.DMA((2,))]),
        out_shape=jax.ShapeDtypeStruct((t, D_MODEL), F32),
        compiler_params=_params(("arbitrary",)),
        name="moe_combine_ple",
    )(dest, dest, x2, meta, p2, ln, wg, wp, yout)


MOE_TM = 512


def _moe_ple(x2, p2, ln_ffn, w_rg, b_rg, w_re, b_re, w1, w3, w2, ln_ple, w_gate, w_proj):
    t = x2.shape[0]
    tm = MOE_TM
    wr = jnp.concatenate([w_rg, w_re, jnp.zeros((D_MODEL, LANES - ROUTE_LANES), w_rg.dtype)], axis=1).astype(F32)
    wr_hi = wr.astype(BF16)
    wr_lo = (wr - wr_hi.astype(F32)).astype(BF16)
    br = jnp.concatenate([b_rg, b_re, jnp.zeros((LANES - ROUTE_LANES,), b_rg.dtype)]).astype(F32)[None, :]
    ii = jnp.arange(tm, dtype=jnp.int32)
    lstrict = jnp.where(ii[:, None] > ii[None, :], 1.0, 0.0).astype(BF16)
    meta, cnt = _router(x2, ln_ffn[None, :], jnp.stack([wr_hi, wr_lo]), br, lstrict, tm)

    counts = cnt[0, :N_EXPERTS].astype(jnp.int32)
    padded = (counts + MOE_BLOCK - 1) // MOE_BLOCK * MOE_BLOCK
    pad_end = jnp.cumsum(padded)
    start = (pad_end - padded).astype(jnp.int32)
    nb = (t * 2) // MOE_BLOCK + N_EXPERTS
    first_row = jnp.arange(nb, dtype=jnp.int32) * MOE_BLOCK
    block_e = jnp.minimum(jnp.sum((pad_end[None, :] <= first_row[:, None]).astype(jnp.int32), axis=1),
                          N_EXPERTS - 1).astype(jnp.int32)
    nused = (pad_end[-1:] // MOE_BLOCK).astype(jnp.int32)
    er = meta[:, :4].astype(jnp.int32)
    eid = jnp.arange(N_EXPERTS, dtype=jnp.int32)
    seg_start = jnp.sum(jnp.where(er[:, :2, None] == eid, start, 0), axis=-1)
    dest = (seg_start + er[:, 2:4]).reshape(t // tm, 1, 2 * tm)

    xin = _dispatch(pad_end.astype(jnp.int32), nused, dest, x2, ln_ffn[None, :], nb, tm)
    yout = _experts(block_e, nused, xin, w1.astype(BF16), w3.astype(BF16), w2.astype(BF16))
    return _combine_ple(dest, x2, meta, p2, ln_ple[None, :], w_gate.astype(BF16), w_proj.astype(BF16), yout, tm)


def _trunk(x, p, prm):
    bsz, seq, _ = x.shape
    x2 = x.reshape(bsz * seq, D_MODEL)
    for i in range(DEPTH):
        j = i // 2
        if i % 2 == 0:
            x2 = _even_mixer(x2, bsz, seq, prm["ln_mix_e"][j], prm["w_in_e"][j], prm["q_gain"][j],
                             prm["k_gain"][j], prm["sink"][j], prm["w_out_e"][j])
        else:
            x2 = _odd_mixer(x2, bsz, seq, prm["ln_mix_o"][j], prm["w_in_o"][j], prm["conv_w"][j],
                            prm["conv_b"][j], prm["dt_bias"][j], prm["a_log"][j], prm["d_skip"][j],
                            prm["ssm_gain"][j], prm["w_out_o"][j])
        x2 = _moe_ple(x2, p[i].reshape(bsz * seq, PLE_DIM), prm["ln_ffn"][i], prm["w_router_g"][i],
                      prm["b_router_g"][i], prm["w_router_e"][i], prm["b_router_e"][i], prm["w1"][i],
                      prm["w3"][i], prm["w2"][i], prm["ln_ple"][i], prm["w_ple_gate"][i], prm["w_ple_proj"][i])
    return x2.reshape(bsz, seq, D_MODEL)


def kernel(x_prompt, x_sample, p_prompt, p_sample, ln_mix_e, w_in_e, q_gain, k_gain, sink, w_out_e, ln_mix_o, w_in_o, conv_w, conv_b, dt_bias, a_log, d_skip, ssm_gain, w_out_o, ln_ffn, w_router_g, b_router_g, w_router_e, b_router_e, w1, w3, w2, ln_ple, w_ple_gate, w_ple_proj):
    prm = dict(ln_mix_e=ln_mix_e, w_in_e=w_in_e, q_gain=q_gain, k_gain=k_gain, sink=sink,
               w_out_e=w_out_e, ln_mix_o=ln_mix_o, w_in_o=w_in_o, conv_w=conv_w, conv_b=conv_b,
               dt_bias=dt_bias, a_log=a_log, d_skip=d_skip, ssm_gain=ssm_gain, w_out_o=w_out_o,
               ln_ffn=ln_ffn, w_router_g=w_router_g, b_router_g=b_router_g,
               w_router_e=w_router_e, b_router_e=b_router_e, w1=w1, w3=w3, w2=w2,
               ln_ple=ln_ple, w_ple_gate=w_ple_gate, w_ple_proj=w_ple_proj)
    return (_trunk(x_prompt, p_prompt, prm), _trunk(x_sample, p_sample, prm))
```

```python
import functools
import math

import jax
import jax.numpy as jnp
from jax import lax
from jax.experimental import pallas as pl
from jax.experimental.pallas import tpu as pltpu

D_MODEL = 1024
DEPTH = 4
RMS_EPS = 1e-6
N_FOURIER_GROUPS = 4
FOURIER_WIDTH = 512
FOURIER_GROUP = 128
FFT_N2 = 128
HEAD_DIM = 64
N_Q_HEADS = 8
N_KV_HEADS = 2
ATTN_WIDTH = 512
KV_WIDTH = 128
ATTN_BLOCK = 128
ROPE_THETA = 10000.0
D_INNER = 2048
SSM_HEAD_DIM = 64
SSM_HEADS = 32
SSM_STATE = 128
SSM_GROUPS = 4
HEADS_PER_GROUP = 8
CONV_K = 5
CHUNK = 128
GN = SSM_GROUPS * SSM_STATE
CONV_DIM = D_INNER + 2 * GN
N_EXPERT_GROUPS = 4
EXPERTS_PER_GROUP = 8
N_EXPERTS = 32
D_EXPERT = 512
MOE_BLOCK = 256
PLE_DIM = 256

LANES = 128
NEG_BIG = -1e30
VMEM_LIMIT = 52 * 1024 * 1024

F32 = jnp.float32
BF16 = jnp.bfloat16


def _params(sem, vmem=VMEM_LIMIT):
    return pltpu.CompilerParams(dimension_semantics=sem, vmem_limit_bytes=vmem)


def _dot(a, b):
    return jnp.dot(a, b, preferred_element_type=F32)


def _dot_nt(a, b):
    return lax.dot_general(a, b, (((1,), (1,)), ((), ())), preferred_element_type=F32)


def _split3(v):
    hi = v.astype(BF16)
    r1 = v - hi.astype(F32)
    mid = r1.astype(BF16)
    lo = (r1 - mid.astype(F32)).astype(BF16)
    return hi, mid, lo


def _dot3(v, m_bf16):
    hi, mid, lo = _split3(v)
    return _dot(hi, m_bf16) + _dot(mid, m_bf16) + _dot(lo, m_bf16)


def _dot3_left(m_bf16, v):
    hi, mid, lo = _split3(v)
    return _dot(m_bf16, hi) + _dot(m_bf16, mid) + _dot(m_bf16, lo)


def _rms_rows(x, g):
    return x * lax.rsqrt(jnp.mean(x * x, axis=-1, keepdims=True) + RMS_EPS) * g


def _silu(x):
    return x * (1.0 / (1.0 + jnp.exp(-x)))


def _full(shape):
    return pl.BlockSpec(shape, lambda *_: (0,) * len(shape))


def _even_in_kernel(x_ref, g_ref, w_ref, dft_ref, bd_ref, qg_ref, kg_ref, cos_ref, sin_ref,
                    gout_ref, q_ref, k_ref, v_ref):
    h = _rms_rows(x_ref[...], g_ref[...]).astype(BF16)
    u = _dot(h, w_ref[...])
    a = u[:, :FOURIER_WIDTH].astype(BF16)
    for g in range(N_FOURIER_GROUPS):
        r = _dot(a[:, g * LANES:(g + 1) * LANES], dft_ref[...])
        gout_ref[:, g * LANES:(g + 1) * LANES] = r[:, :LANES].astype(BF16)
        gout_ref[:, FOURIER_WIDTH + g * LANES:FOURIER_WIDTH + (g + 1) * LANES] = r[:, LANES:].astype(BF16)

    cos = cos_ref[...]
    sin = sin_ref[...]

    def norm_rope(t, gain, width):
        n = width // LANES
        bd = bd_ref[:width, :width]
        t2 = t * t
        hi = t2.astype(BF16)
        lo = (t2 - hi.astype(F32)).astype(BF16)
        ms = _dot(hi, bd) + _dot(lo, bd)
        tn = t * lax.rsqrt(ms + RMS_EPS) * gain
        lane = lax.broadcasted_iota(jnp.int32, tn.shape, 1)
        first_half = (lane % HEAD_DIM) < (HEAD_DIM // 2)
        rot = jnp.where(first_half, pltpu.roll(tn, width - HEAD_DIM // 2, 1), pltpu.roll(tn, HEAD_DIM // 2, 1))
        c = jnp.concatenate([cos] * n, axis=1)
        s = jnp.concatenate([sin] * n, axis=1)
        return tn * c + rot * s

    q = norm_rope(u[:, 512:1024], qg_ref[...], ATTN_WIDTH) * (HEAD_DIM ** -0.5)
    k = norm_rope(u[:, 1024:1280], kg_ref[...], 2 * KV_WIDTH)
    q_ref[...] = q.astype(BF16)
    k_ref[...] = k.astype(BF16)
    v_ref[...] = u[:, 1280:1536].astype(BF16)


def _even_in(x2, seq, ln, w, dft, bd, qg, kg, cos_t, sin_t, tm=512):
    t = x2.shape[0]
    nseq = seq // tm
    row = lambda i: (i, 0)
    pos = lambda i: (i % nseq, 0)
    return pl.pallas_call(
        _even_in_kernel,
        grid=(t // tm,),
        in_specs=[pl.BlockSpec((tm, D_MODEL), row), _full((1, D_MODEL)), _full((D_MODEL, 1536)),
                  _full((LANES, 2 * LANES)), _full((512, 512)), _full((1, 512)), _full((1, 256)),
                  pl.BlockSpec((tm, LANES), pos), pl.BlockSpec((tm, LANES), pos)],
        out_specs=[pl.BlockSpec((tm, 1024), row), pl.BlockSpec((tm, 512), row),
                   pl.BlockSpec((tm, 256), row), pl.BlockSpec((tm, 256), row)],
        out_shape=[jax.ShapeDtypeStruct((t, 1024), BF16), jax.ShapeDtypeStruct((t, 512), BF16),
                   jax.ShapeDtypeStruct((t, 256), BF16), jax.ShapeDtypeStruct((t, 256), BF16)],
        compiler_params=_params(("parallel",)),
        name="even_in",
    )(x2, ln, w, dft, bd, qg, kg, cos_t, sin_t)


def _fft1_kernel(g_ref, c_ref, s_ref, y_ref):
    g = g_ref[0]
    p = _dot(c_ref[...], g)
    q = _dot(s_ref[...], g)
    for j in range(g.shape[1] // 1024):
        o = j * 1024
        y_ref[0, :, o:o + 512] = (p[:, o:o + 512] + q[:, o + 512:o + 1024]).astype(BF16)
        y_ref[0, :, o + 512:o + 1024] = (p[:, o + 512:o + 1024] - q[:, o:o + 512]).astype(BF16)


def _fft1(g3, c1, s1, cb=8192):
    b, n1, width = g3.shape
    blk = lambda i, j: (i, 0, j)
    return pl.pallas_call(
        _fft1_kernel,
        grid=(b, width // cb),
        in_specs=[pl.BlockSpec((1, n1, cb), blk), _full((n1, n1)), _full((n1, n1))],
        out_specs=pl.BlockSpec((1, n1, cb), blk),
        out_shape=jax.ShapeDtypeStruct(g3.shape, BF16),
        compiler_params=_params(("parallel", "parallel")),
        name="fft_stage1",
    )(g3, c1, s1)


def _fft2_kernel(y_ref, mc_ref, ms_ref, o_ref, *, scale):
    for j in range(y_ref.shape[1]):
        yr = y_ref[0, j, :, :512]
        yi = y_ref[0, j, :, 512:]
        r = _dot(mc_ref[j], yr) + _dot(ms_ref[j], yi)
        o_ref[0, :, j * 512:(j + 1) * 512] = (r * scale).astype(BF16)


def _fft2(y4, mc, ms, scale, kb=8):
    b, n1, n2, _ = y4.shape
    return pl.pallas_call(
        functools.partial(_fft2_kernel, scale=scale),
        grid=(n1 // kb, b),
        in_specs=[pl.BlockSpec((1, kb, n2, 1024), lambda k, i: (i, k, 0, 0)),
                  pl.BlockSpec((kb, n2, n2), lambda k, i: (k, 0, 0)),
                  pl.BlockSpec((kb, n2, n2), lambda k, i: (k, 0, 0))],
        out_specs=pl.BlockSpec((1, n2, kb * 512), lambda k, i: (i, 0, k)),
        out_shape=jax.ShapeDtypeStruct((b, n2, n1 * 512), BF16),
        compiler_params=_params(("parallel", "parallel")),
        name="fft_stage2",
    )(y4, mc, ms)


def _dft_tables(seq):
    n1, n2 = seq // FFT_N2, FFT_N2
    i1 = jnp.arange(n1, dtype=jnp.int32)
    ang1 = ((i1[:, None] * i1[None, :]) % n1).astype(F32) * (2.0 * math.pi / n1)
    c1, s1 = jnp.cos(ang1).astype(BF16), jnp.sin(ang1).astype(BF16)
    i2 = jnp.arange(n2, dtype=jnp.int32)
    ph = (i2[None, None, :] * i1[:, None, None] + n1 * i2[None, :, None] * i2[None, None, :]) % seq
    ang2 = ph.astype(F32) * (2.0 * math.pi / seq)
    mc, ms = jnp.cos(ang2).astype(BF16), jnp.sin(ang2).astype(BF16)
    ic = jnp.arange(FOURIER_GROUP, dtype=jnp.int32)
    angc = ((ic[:, None] * ic[None, :]) % FOURIER_GROUP).astype(F32) * (2.0 * math.pi / FOURIER_GROUP)
    dftc = jnp.concatenate([jnp.cos(angc), -jnp.sin(angc)], axis=1).astype(BF16)
    return c1, s1, mc, ms, dftc


def _attn_kernel(sink_ref, q_ref, kp_ref, ko_ref, kn_ref, vp_ref, vo_ref, vn_ref, o_ref, kc_ref, vc_ref):
    i = pl.program_id(1)
    last = pl.num_programs(1) - 1
    tq = q_ref.shape[0]
    nb = tq // ATTN_BLOCK
    kc_ref[0:ATTN_BLOCK] = kp_ref[...]
    kc_ref[ATTN_BLOCK:ATTN_BLOCK + tq] = ko_ref[...]
    kc_ref[ATTN_BLOCK + tq:] = kn_ref[...]
    vc_ref[0:ATTN_BLOCK] = vp_ref[...]
    vc_ref[ATTN_BLOCK:ATTN_BLOCK + tq] = vo_ref[...]
    vc_ref[ATTN_BLOCK + tq:] = vn_ref[...]

    w3 = 3 * ATTN_BLOCK
    qpos = lax.broadcasted_iota(jnp.int32, (ATTN_BLOCK, w3), 0) + ATTN_BLOCK
    kpos = lax.broadcasted_iota(jnp.int32, (ATTN_BLOCK, w3), 1)
    band = jnp.abs(kpos - qpos) <= ATTN_BLOCK
    lane = lax.broadcasted_iota(jnp.int32, (w3, LANES), 1)
    lo_half = lane < HEAD_DIM
    out_lane = lax.broadcasted_iota(jnp.int32, (ATTN_BLOCK, LANES), 1) < HEAD_DIM

    for jb in range(nb):
        ok = band
        if jb == 0:
            ok = ok & ((kpos >= ATTN_BLOCK) | (i > 0))
        if jb == nb - 1:
            ok = ok & ((kpos < 2 * ATTN_BLOCK) | (i < last))
        bias = jnp.where(ok, 0.0, NEG_BIG).astype(F32)
        qb = q_ref[jb * ATTN_BLOCK:(jb + 1) * ATTN_BLOCK, :]
        kw = kc_ref[jb * ATTN_BLOCK:jb * ATTN_BLOCK + w3, :]
        vw = vc_ref[jb * ATTN_BLOCK:jb * ATTN_BLOCK + w3, :]
        for g in range(N_KV_HEADS):
            kg = kw[:, g * LANES:(g + 1) * LANES]
            vg = vw[:, g * LANES:(g + 1) * LANES]
            zero = jnp.zeros_like(kg)
            kbd = jnp.concatenate([jnp.where(lo_half, kg, zero), jnp.where(lo_half, zero, kg)], axis=0)
            vbd = jnp.concatenate([jnp.where(lo_half, vg, zero), jnp.where(lo_half, zero, vg)], axis=0)
            for pr in range(2):
                col = (2 * g + pr) * LANES
                s = _dot_nt(qb[:, col:col + LANES], kbd)
                es, rs = [], []
                for half in range(2):
                    sk = sink_ref[4 * g + 2 * pr + half]
                    sh = s[:, half * w3:(half + 1) * w3] + bias
                    m = jnp.maximum(jnp.max(sh, axis=-1, keepdims=True), sk)
                    e = jnp.exp(sh - m)
                    den = jnp.sum(e, axis=-1, keepdims=True) + jnp.exp(sk - m)
                    es.append(e.astype(BF16))
                    rs.append(1.0 / den)
                o = _dot(jnp.concatenate(es, axis=1), vbd)
                o = o * jnp.where(out_lane, rs[0], rs[1])
                o_ref[jb * ATTN_BLOCK:(jb + 1) * ATTN_BLOCK, col:col + LANES] = o.astype(BF16)


def _attention(q, k, v, sink, bsz, seq, tq=512):
    t = q.shape[0]
    nq = seq // tq
    r = tq // ATTN_BLOCK
    nblk = t // ATTN_BLOCK
    own = lambda b, i, s: (b * nq + i, 0)
    prev = lambda b, i, s: (jnp.maximum((b * nq + i) * r - 1, 0), 0)
    nxt = lambda b, i, s: (jnp.minimum((b * nq + i + 1) * r, nblk - 1), 0)
    kv = lambda m: pl.BlockSpec((ATTN_BLOCK if m is not own else tq, 256), m)
    return pl.pallas_call(
        _attn_kernel,
        grid_spec=pltpu.PrefetchScalarGridSpec(
            num_scalar_prefetch=1,
            grid=(bsz, nq),
            in_specs=[pl.BlockSpec((tq, 512), own), kv(prev), kv(own), kv(nxt), kv(prev), kv(own), kv(nxt)],
            out_specs=pl.BlockSpec((tq, 512), own),
            scratch_shapes=[pltpu.VMEM((tq + 2 * ATTN_BLOCK, 256), BF16),
                            pltpu.VMEM((tq + 2 * ATTN_BLOCK, 256), BF16)]),
        out_shape=jax.ShapeDtypeStruct((t, 512), BF16),
        compiler_params=_params(("parallel", "parallel")),
        name="window_attention",
    )(sink, q, k, k, k, v, v, v)


def _even_out_kernel(x_ref, f_ref, o_ref, w_ref, out_ref):
    out_ref[...] = (x_ref[...] + _dot(f_ref[...], w_ref[:512, :]) + _dot(o_ref[...], w_ref[512:, :]))


def _even_out(x2, f, o, w, tm=512):
    t = x2.shape[0]
    row = lambda i: (i, 0)
    return pl.pallas_call(
        _even_out_kernel,
        grid=(t // tm,),
        in_specs=[pl.BlockSpec((tm, D_MODEL), row), pl.BlockSpec((tm, 512), row),
                  pl.BlockSpec((tm, 512), row), _full((1024, D_MODEL))],
        out_specs=pl.BlockSpec((tm, D_MODEL), row),
        out_shape=jax.ShapeDtypeStruct((t, D_MODEL), F32),
        compiler_params=_params(("parallel",)),
        name="even_out",
    )(x2, f, o, w)


def _even_mixer(x2, bsz, seq, ln, w_in, q_gain, k_gain, sink, w_out):
    c1, s1, mc, ms, dftc = _dft_tables(seq)
    n1 = seq // FFT_N2
    kcols = w_in[:, 1024:1152]
    vcols = w_in[:, 1152:1280]
    dup = lambda c: jnp.concatenate([c[:, :64], c[:, :64], c[:, 64:], c[:, 64:]], axis=1)
    w = jnp.concatenate([w_in[:, :1024], dup(kcols), dup(vcols)], axis=1).astype(BF16)
    hid = jnp.arange(512, dtype=jnp.int32) // HEAD_DIM
    bd = jnp.where(hid[:, None] == hid[None, :], 1.0 / HEAD_DIM, 0.0).astype(BF16)
    inv = 1.0 / (ROPE_THETA ** (jnp.arange(0, HEAD_DIM, 2, dtype=F32) / HEAD_DIM))
    ang = jnp.arange(seq, dtype=F32)[:, None] * inv[None, :]
    cos, sin = jnp.cos(ang), jnp.sin(ang)
    cos_t = jnp.concatenate([cos, cos, cos, cos], axis=1)
    sin_t = jnp.concatenate([-sin, sin, -sin, sin], axis=1)
    qg = jnp.tile(q_gain.astype(F32), N_Q_HEADS)[None, :]
    kg = jnp.tile(k_gain.astype(F32), 2 * N_KV_HEADS)[None, :]
    g, q, k, v = _even_in(x2, seq, ln[None, :], w, dftc, bd, qg, kg, cos_t, sin_t)
    y = _fft1(g.reshape(bsz, n1, FFT_N2 * 1024), c1, s1)
    scale = 1.0 / math.sqrt(seq * FOURIER_GROUP)
    f = _fft2(y.reshape(bsz, n1, FFT_N2, 1024), mc, ms, scale).reshape(bsz * seq, 512)
    o = _attention(q, k, v, sink.astype(F32), bsz, seq)
    return _even_out(x2, f, o, w_out.astype(BF16))


def _odd_in_kernel(x_ref, g_ref, w_ref, z_ref, xbc_ref, dt_ref):
    h = _rms_rows(x_ref[...], g_ref[...]).astype(BF16)
    z_ref[...] = _dot(h, w_ref[:, :D_INNER]).astype(BF16)
    xbc_ref[...] = _dot(h, w_ref[:, D_INNER:D_INNER + CONV_DIM]).astype(BF16)
    dt_ref[...] = _dot(h, w_ref[:, D_INNER + CONV_DIM:])


def _odd_in(x2, ln, w, tm=256):
    t = x2.shape[0]
    row = lambda i: (i, 0)
    wcols = D_INNER + CONV_DIM + LANES
    return pl.pallas_call(
        _odd_in_kernel,
        grid=(t // tm,),
        in_specs=[pl.BlockSpec((tm, D_MODEL), row), _full((1, D_MODEL)), _full((D_MODEL, wcols))],
        out_specs=[pl.BlockSpec((tm, D_INNER), row), pl.BlockSpec((tm, CONV_DIM), row),
                   pl.BlockSpec((tm, LANES), row)],
        out_shape=[jax.ShapeDtypeStruct((t, D_INNER), BF16), jax.ShapeDtypeStruct((t, CONV_DIM), BF16),
                   jax.ShapeDtypeStruct((t, LANES), F32)],
        compiler_params=_params(("parallel",)),
        name="odd_in",
    )(x2, ln, w)


CONV_HALO = 16


def _conv_kernel(prev_ref, main_ref, next_ref, w_ref, b_ref, o_ref, win_ref, *, tiles_per_seq):
    i = pl.program_id(0) % tiles_per_seq
    tc = main_ref.shape[0]
    keep_prev = (i > 0).astype(F32)
    keep_next = (i < tiles_per_seq - 1).astype(F32)
    win_ref[0:CONV_HALO, :] = prev_ref[...].astype(F32) * keep_prev
    win_ref[CONV_HALO:CONV_HALO + tc, :] = main_ref[...].astype(F32)
    win_ref[CONV_HALO + tc:, :] = next_ref[...].astype(F32) * keep_next
    cb = 512
    for c in range(CONV_DIM // cb):
        cols = slice(c * cb, (c + 1) * cb)
        win = win_ref[:, cols]
        acc = jnp.broadcast_to(b_ref[:, cols], (tc, cb))
        for k in range(CONV_K):
            shift = CONV_K // 2 - k
            rolled = win if shift == 0 else pltpu.roll(win, shift % win.shape[0], 0)
            acc = acc + rolled[CONV_HALO:CONV_HALO + tc, :] * w_ref[k:k + 1, cols]
        o_ref[:, cols] = _silu(acc).astype(BF16)


def _conv(xbc, seq, w, b, tc=512):
    t = xbc.shape[0]
    r = tc // CONV_HALO
    nh = t // CONV_HALO
    return pl.pallas_call(
        functools.partial(_conv_kernel, tiles_per_seq=seq // tc),
        grid=(t // tc,),
        in_specs=[pl.BlockSpec((CONV_HALO, CONV_DIM), lambda i: (jnp.maximum(i * r - 1, 0), 0)),
                  pl.BlockSpec((tc, CONV_DIM), lambda i: (i, 0)),
                  pl.BlockSpec((CONV_HALO, CONV_DIM), lambda i: (jnp.minimum((i + 1) * r, nh - 1), 0)),
                  _full((CONV_K, CONV_DIM)), _full((1, CONV_DIM))],
        out_specs=pl.BlockSpec((tc, CONV_DIM), lambda i: (i, 0)),
        out_shape=jax.ShapeDtypeStruct((t, CONV_DIM), BF16),
        scratch_shapes=[pltpu.VMEM((tc + 2 * CONV_HALO, CONV_DIM), F32)],
        compiler_params=_params(("parallel",)),
        name="ssd_conv",
    )(xbc, xbc, xbc, w, b)


N_PAIRS = SSM_HEADS // 2
PAIRS_PER_GROUP = HEADS_PER_GROUP // 2


LOG2E = math.log2(math.e)
SSD_STEP = 2 * CHUNK


def _ssd_chunk(xbc_ref, dt_ref, bias, a_row, expand, y_ref, state_ref, r0, reverse, mask, tri_col, lo_half):
    off = SSM_HEADS if reverse else 0
    rows = slice(r0, r0 + CHUNK)
    dt_lh = dt_ref[rows, :] + bias
    dt_lh = jnp.maximum(dt_lh, 0.0) + jnp.log(1.0 + jnp.exp(-jnp.abs(dt_lh)))
    acs_lh = _dot3_left(tri_col, dt_lh * a_row)
    edge = 0 if reverse else CHUNK - 1
    acs2_lh = acs_lh * LOG2E
    tot2_row = acs2_lh[edge:edge + 1, :]
    r_hl = acs2_lh.T
    dt_hi = dt_lh.astype(BF16)
    dt_mid = (dt_lh - dt_hi.astype(F32)).astype(BF16)
    dt_exp = _dot(dt_hi, expand) + _dot(dt_mid, expand)
    lo_row = lo_half[0:1, :]

    for g in range(SSM_GROUPS):
        b_g = xbc_ref[rows, D_INNER + g * SSM_STATE:D_INNER + (g + 1) * SSM_STATE]
        c_g = xbc_ref[rows, D_INNER + GN + g * SSM_STATE:D_INNER + GN + (g + 1) * SSM_STATE]
        cb_b = _dot_nt(c_g, b_g).astype(BF16)
        bt = b_g.astype(F32).T.astype(BF16)
        gcols = slice(g * PAIRS_PER_GROUP * LANES, (g + 1) * PAIRS_PER_GROUP * LANES)
        y_off = _dot(c_g, state_ref[:, gcols].astype(BF16))
        for pp in range(PAIRS_PER_GROUP):
            p = g * PAIRS_PER_GROUP + pp
            hd0, hd1 = off + 2 * p, off + 2 * p + 1
            colb = [jnp.broadcast_to(acs2_lh[:, hd:hd + 1], (CHUNK, CHUNK)) for hd in (hd0, hd1)]
            ms = [cb_b * jnp.exp2(jnp.where(mask, cb_l - r_hl[hd:hd + 1, :], NEG_BIG)).astype(BF16)
                  for cb_l, hd in zip(colb, (hd0, hd1))]
            pcols = slice(p * LANES, (p + 1) * LANES)
            xdt = xbc_ref[rows, pcols].astype(F32) * dt_exp[:, pcols]
            xdt_b = xdt.astype(BF16)
            zx = jnp.zeros_like(xdt_b)
            xbd = jnp.concatenate([jnp.where(lo_half, xdt_b, zx), jnp.where(lo_half, zx, xdt_b)], axis=0)
            colsel = jnp.where(lo_half, colb[0], colb[1])
            y = _dot(jnp.concatenate(ms, axis=1), xbd)
            y = y + jnp.exp2(colsel) * y_off[:, pp * LANES:(pp + 1) * LANES]
            y_ref[rows, pcols] = y.astype(BF16)
            tot2 = jnp.where(lo_row, tot2_row[:, hd0:hd0 + 1], tot2_row[:, hd1:hd1 + 1])
            snew = _dot(bt, (xdt * jnp.exp2(tot2 - colsel)).astype(BF16))
            state_ref[:, pcols] = state_ref[:, pcols] * jnp.exp2(tot2) + snew


def _ssd_kernel(xbc_ref, dt_ref, bias_ref, alog_ref, exp_ref, y_ref, state_ref, *, reverse):
    @pl.when(pl.program_id(1) == 0)
    def _():
        state_ref[...] = jnp.zeros_like(state_ref)

    r_io = lax.broadcasted_iota(jnp.int32, (CHUNK, CHUNK), 0)
    c_io = lax.broadcasted_iota(jnp.int32, (CHUNK, CHUNK), 1)
    mask = (r_io <= c_io) if reverse else (r_io >= c_io)
    tri_col = jnp.where(mask, 1.0, 0.0).astype(BF16)
    lo_half = lax.broadcasted_iota(jnp.int32, (CHUNK, LANES), 1) < SSM_HEAD_DIM
    bias = bias_ref[...]
    a_row = -jnp.exp(alog_ref[...])
    n_sub = SSD_STEP // CHUNK
    order = range(n_sub - 1, -1, -1) if reverse else range(n_sub)
    for sub in order:
        _ssd_chunk(xbc_ref, dt_ref, bias, a_row, exp_ref[...], y_ref, state_ref, sub * CHUNK, reverse,
                   mask, tri_col, lo_half)


def _ssd(xbc_c, dt_raw, bias_row, alog_row, expand, bsz, seq, reverse):
    t = xbc_c.shape[0]
    ns = seq // SSD_STEP
    if reverse:
        blk = lambda b, c: (b * ns + ns - 1 - c, 0)
    else:
        blk = lambda b, c: (b * ns + c, 0)
    return pl.pallas_call(
        functools.partial(_ssd_kernel, reverse=reverse),
        grid=(bsz, ns),
        in_specs=[pl.BlockSpec((SSD_STEP, CONV_DIM), blk), pl.BlockSpec((SSD_STEP, LANES), blk),
                  _full((1, LANES)), _full((1, LANES)), _full((LANES, D_INNER))],
        out_specs=pl.BlockSpec((SSD_STEP, D_INNER), blk),
        out_shape=jax.ShapeDtypeStruct((t, D_INNER), BF16),
        scratch_shapes=[pltpu.VMEM((SSM_STATE, D_INNER), F32)],
        compiler_params=_params(("parallel", "arbitrary")),
        name="ssd_scan_bwd" if reverse else "ssd_scan_fwd",
    )(xbc_c, dt_raw, bias_row, alog_row, expand)


def _odd_out_kernel(x_ref, yf_ref, yb_ref, xs_ref, z_ref, d_ref, g_ref, w_ref, out_ref):
    y = yf_ref[...].astype(F32) + yb_ref[...].astype(F32) + d_ref[...] * xs_ref[...].astype(F32)
    y = y * _silu(z_ref[...].astype(F32))
    gw = D_INNER // SSM_GROUPS
    parts = []
    for g in range(SSM_GROUPS):
        yg = y[:, g * gw:(g + 1) * gw]
        parts.append(yg * lax.rsqrt(jnp.mean(yg * yg, axis=-1, keepdims=True) + RMS_EPS))
    yn = (jnp.concatenate(parts, axis=1) * g_ref[...]).astype(BF16)
    out_ref[...] = x_ref[...] + _dot(yn, w_ref[...])


def _odd_out(x2, yf, yb, xbc_c, z, dskip, gain, w, tm=512):
    t = x2.shape[0]
    row = lambda i: (i, 0)
    return pl.pallas_call(
        _odd_out_kernel,
        grid=(t // tm,),
        in_specs=[pl.BlockSpec((tm, D_MODEL), row), pl.BlockSpec((tm, D_INNER), row),
                  pl.BlockSpec((tm, D_INNER), row), pl.BlockSpec((tm, D_INNER), row),
                  pl.BlockSpec((tm, D_INNER), row), _full((1, D_INNER)), _full((1, D_INNER)),
                  _full((D_INNER, D_MODEL))],
        out_specs=pl.BlockSpec((tm, D_MODEL), row),
        out_shape=jax.ShapeDtypeStruct((t, D_MODEL), F32),
        compiler_params=_params(("parallel",)),
        name="odd_out",
    )(x2, yf, yb, xbc_c, z, dskip, gain, w)


def _odd_mixer(x2, bsz, seq, ln, w_in, conv_w, conv_b, dt_bias, a_log, d_skip, norm_g, w_out):
    pad = jnp.zeros((D_MODEL, LANES - 2 * SSM_HEADS), w_in.dtype)
    w = jnp.concatenate([w_in, pad], axis=1).astype(BF16)
    z, xbc, dt_raw = _odd_in(x2, ln[None, :], w)
    xbc_c = _conv(xbc, seq, conv_w.astype(F32), conv_b.astype(F32)[None, :])
    lane_pad = jnp.zeros((LANES - 2 * SSM_HEADS,), F32)
    bias_row = jnp.concatenate([dt_bias.astype(F32).reshape(-1), lane_pad])[None, :]
    alog_row = jnp.concatenate([a_log.astype(F32).reshape(-1), lane_pad])[None, :]
    ch = jnp.arange(D_INNER, dtype=jnp.int32) // SSM_HEAD_DIM
    hd = jnp.arange(LANES, dtype=jnp.int32)
    ys = []
    for reverse in (False, True):
        off = SSM_HEADS if reverse else 0
        expand = jnp.where(hd[:, None] == off + ch[None, :], 1.0, 0.0).astype(BF16)
        ys.append(_ssd(xbc_c, dt_raw, bias_row, alog_row, expand, bsz, seq, reverse))
    dskip = jnp.repeat(d_skip.astype(F32), SSM_HEAD_DIM)[None, :]
    return _odd_out(x2, ys[0], ys[1], xbc_c, z, dskip, norm_g.astype(F32)[None, :], w_out.astype(BF16))


ROUTE_LANES = N_EXPERT_GROUPS + N_EXPERTS


def _router_kernel(x_ref, g_ref, w_ref, b_ref, ls_ref, meta_ref, cnt_ref, run_ref):
    i = pl.program_id(0)

    @pl.when(i == 0)
    def _():
        run_ref[...] = jnp.zeros_like(run_ref)

    h = _rms_rows(x_ref[...], g_ref[...])
    hh = h.astype(BF16)
    hl = (h - hh.astype(F32)).astype(BF16)
    w_hi = w_ref[0]
    w_lo = w_ref[1]
    logits = _dot(hh, w_hi) + _dot(hl, w_hi) + _dot(hh, w_lo) + b_ref[...]
    tm = logits.shape[0]
    lane = lax.broadcasted_iota(jnp.int32, (tm, LANES), 1).astype(F32)

    def top1(v):
        m = jnp.max(v, axis=-1, keepdims=True)
        idx = jnp.min(jnp.where(v == m, lane, float(LANES)), axis=-1, keepdims=True)
        return m, idx

    gl = jnp.where(lane < N_EXPERT_GROUPS, logits, NEG_BIG)
    gmax, gidx = top1(gl)
    g_p = 1.0 / jnp.sum(jnp.exp(gl - gmax), axis=-1, keepdims=True)
    lo = N_EXPERT_GROUPS + EXPERTS_PER_GROUP * gidx
    e1 = jnp.where((lane >= lo) & (lane < lo + EXPERTS_PER_GROUP), logits, NEG_BIG)
    m1, i1 = top1(e1)
    e2 = jnp.where(lane == i1, NEG_BIG, e1)
    m2, i2 = top1(e2)
    r = jnp.exp(m2 - m1)
    w_a = g_p / (1.0 + r)
    w_b = g_p * r / (1.0 + r)
    e_a = i1 - N_EXPERT_GROUPS
    e_b = i2 - N_EXPERT_GROUPS

    onehot = jnp.where((lane == e_a) | (lane == e_b), 1.0, 0.0)
    before = _dot(ls_ref[...], onehot.astype(BF16)) + run_ref[...]
    rank_a = jnp.sum(jnp.where(lane == e_a, before, 0.0), axis=-1, keepdims=True)
    rank_b = jnp.sum(jnp.where(lane == e_b, before, 0.0), axis=-1, keepdims=True)
    run_ref[...] = run_ref[...] + jnp.sum(onehot, axis=0, keepdims=True)
    cnt_ref[...] = jnp.broadcast_to(run_ref[...], cnt_ref.shape)

    meta = jnp.where(lane == 0, e_a, 0.0)
    meta = jnp.where(lane == 1, e_b, meta)
    meta = jnp.where(lane == 2, rank_a, meta)
    meta = jnp.where(lane == 3, rank_b, meta)
    meta = jnp.where(lane == 4, w_a, meta)
    meta = jnp.where(lane == 5, w_b, meta)
    meta_ref[...] = meta


def _router(x2, ln, w2, b, lstrict, tm):
    t = x2.shape[0]
    row = lambda i: (i, 0)
    return pl.pallas_call(
        _router_kernel,
        grid=(t // tm,),
        in_specs=[pl.BlockSpec((tm, D_MODEL), row), _full((1, D_MODEL)), _full((2, D_MODEL, LANES)),
                  _full((1, LANES)), _full((tm, tm))],
        out_specs=[pl.BlockSpec((tm, LANES), row), _full((8, LANES))],
        out_shape=[jax.ShapeDtypeStruct((t, LANES), F32), jax.ShapeDtypeStruct((8, LANES), F32)],
        scratch_shapes=[pltpu.VMEM((1, LANES), F32)],
        compiler_params=_params(("arbitrary",)),
        name="moe_router",
    )(x2, ln, w2, b, lstrict)


SUBLANES = 8
PACKED = D_MODEL // 2
DMA_THREADS = 2
U32 = jnp.uint32


def _pack_rows(v):
    bits = lax.bitcast_convert_type(v.astype(BF16).astype(F32), U32)
    return (bits[:, :PACKED] >> 16) | bits[:, PACKED:]


def _unpack_rows(w):
    lo = lax.bitcast_convert_type(w << 16, F32)
    hi = lax.bitcast_convert_type(w & jnp.uint32(0xFFFF0000), F32)
    return jnp.concatenate([lo, hi], axis=1)


def _dispatch_kernel(pad_end_ref, nused_ref, dest_ref, x_ref, g_ref, xin_hbm, h_ref, zero_ref, sem, zsem,
                     *, tm, nb):
    def zero_block(first_row):
        return pltpu.make_async_copy(zero_ref, xin_hbm.at[pl.ds(pl.multiple_of(first_row, MOE_BLOCK), MOE_BLOCK), :],
                                     zsem)

    def seg_zero(action):
        for e in range(N_EXPERTS):
            end = pad_end_ref[e]
            prev = pad_end_ref[e - 1] if e else 0

            @pl.when(end > prev)
            def _():
                action(zero_block(end - MOE_BLOCK))

    def tail_zero(action):
        def body(b, carry):
            action(zero_block(b * MOE_BLOCK))
            return carry

        lax.fori_loop(nused_ref[0], nb, body, 0)

    @pl.when(pl.program_id(0) == 0)
    def _():
        zero_ref[...] = jnp.zeros_like(zero_ref)
        seg_zero(lambda c: c.start())
        tail_zero(lambda c: c.start())
        seg_zero(lambda c: c.wait())
        tail_zero(lambda c: c.wait())

    h_ref[...] = _pack_rows(_rms_rows(x_ref[...], g_ref[...])).reshape(h_ref.shape)

    def row_copy(grp, c, dst_row):
        return pltpu.make_async_copy(h_ref.at[grp, pl.ds(c, 1), :], xin_hbm.at[pl.ds(dst_row, 1), :], sem)

    def issue(grp, carry):
        for c in range(SUBLANES):
            for k in range(2):
                row_copy(grp, c, dest_ref[0, 0, 2 * SUBLANES * grp + 2 * c + k]).start(
                    priority=(2 * c + k) % DMA_THREADS)
        return carry

    lax.fori_loop(0, tm // SUBLANES, issue, 0)

    def drain(grp, carry):
        for _ in range(2 * SUBLANES):
            row_copy(0, 0, 0).wait()
        return carry

    lax.fori_loop(0, tm // SUBLANES, drain, 0)


def _dispatch(pad_end, nused, dest, x2, ln, nb, tm):
    t = x2.shape[0]
    return pl.pallas_call(
        functools.partial(_dispatch_kernel, tm=tm, nb=nb),
        grid_spec=pltpu.PrefetchScalarGridSpec(
            num_scalar_prefetch=2,
            grid=(t // tm,),
            in_specs=[pl.BlockSpec((1, 1, 2 * tm), lambda i, pe, nu: (i, 0, 0), memory_space=pltpu.SMEM),
                      pl.BlockSpec((tm, D_MODEL), lambda i, pe, nu: (i, 0)),
                      pl.BlockSpec((1, D_MODEL), lambda i, pe, nu: (0, 0))],
            out_specs=pl.BlockSpec(memory_space=pl.ANY),
            scratch_shapes=[pltpu.VMEM((tm // SUBLANES, SUBLANES, PACKED), U32),
                            pltpu.VMEM((MOE_BLOCK, PACKED), U32),
                            pltpu.SemaphoreType.DMA, pltpu.SemaphoreType.DMA]),
        out_shape=jax.ShapeDtypeStruct((nb * MOE_BLOCK, PACKED), U32),
        compiler_params=_params(("arbitrary",)),
        name="moe_dispatch",
    )(pad_end, nused, dest, x2, ln)


EXPERT_BLOCKS_PER_STEP = 2


def _experts_kernel(be_ref, nused_ref, x_ref, w1a_ref, w3a_ref, w2a_ref, w1b_ref, w3b_ref, w2b_ref, o_ref):
    i = pl.program_id(0)
    first = EXPERT_BLOCKS_PER_STEP * i

    def swiglu(rows, w1_ref, w3_ref, w2_ref):
        x = _unpack_rows(x_ref[rows, :]).astype(BF16)
        a = _silu(_dot(x, w1_ref[0])) * _dot(x, w3_ref[0])
        o_ref[rows, :] = _pack_rows(_dot(a.astype(BF16), w2_ref[0]))

    fused = (first + 1 < nused_ref[0]) & (be_ref[first] == be_ref[first + 1])

    @pl.when(fused)
    def _():
        swiglu(slice(0, EXPERT_BLOCKS_PER_STEP * MOE_BLOCK), w1a_ref, w3a_ref, w2a_ref)

    for half, wrefs in enumerate(((w1a_ref, w3a_ref, w2a_ref), (w1b_ref, w3b_ref, w2b_ref))):
        rows = slice(half * MOE_BLOCK, (half + 1) * MOE_BLOCK)
        live = first + half < nused_ref[0]

        @pl.when(jnp.logical_not(fused) & live)
        def _():
            swiglu(rows, *wrefs)

        @pl.when(jnp.logical_not(live))
        def _():
            o_ref[rows, :] = jnp.zeros((MOE_BLOCK, PACKED), U32)


def _experts(block_e, nused, xin, w1, w3, w2):
    n = EXPERT_BLOCKS_PER_STEP
    nb = xin.shape[0] // MOE_BLOCK
    assert nb % n == 0
    row = lambda i, be, nu: (i, 0)
    specs = []
    for half in range(n):
        wsel = functools.partial(lambda i, be, nu, half: (be[n * i + half], 0, 0), half=half)
        specs += [pl.BlockSpec((1, D_MODEL, D_EXPERT), wsel), pl.BlockSpec((1, D_MODEL, D_EXPERT), wsel),
                  pl.BlockSpec((1, D_EXPERT, D_MODEL), wsel)]
    return pl.pallas_call(
        _experts_kernel,
        grid_spec=pltpu.PrefetchScalarGridSpec(
            num_scalar_prefetch=2,
            grid=(nb // n,),
            in_specs=[pl.BlockSpec((n * MOE_BLOCK, PACKED), row)] + specs,
            out_specs=pl.BlockSpec((n * MOE_BLOCK, PACKED), row)),
        out_shape=jax.ShapeDtypeStruct(xin.shape, U32),
        compiler_params=_params(("parallel",)),
        name="moe_experts",
    )(block_e, nused, xin, *([w1, w3, w2] * n))


def _combine_ple_kernel(dest_ref, dest_next_ref, x_ref, meta_ref, p_ref, g_ref, wg_ref, wp_ref, y_hbm,
                        out_ref, ya_ref, yb_ref, sem, *, tm):
    i = pl.program_id(0)
    slot = i % 2

    def row_gather(src_row, buf, s, grp, c):
        return pltpu.make_async_copy(y_hbm.at[pl.ds(src_row, 1), :], buf.at[s, grp, pl.ds(c, 1), :], sem.at[s])

    def gather_tile(idx_ref, s):
        def issue(grp, carry):
            for c in range(SUBLANES):
                for k, buf in enumerate((ya_ref, yb_ref)):
                    row_gather(idx_ref[0, 0, 2 * SUBLANES * grp + 2 * c + k], buf, s, grp, c).start(
                        priority=(2 * c + k) % DMA_THREADS)
            return carry

        lax.fori_loop(0, tm // SUBLANES, issue, 0)

    @pl.when(i == 0)
    def _():
        gather_tile(dest_ref, 0)

    @pl.when(i + 1 < pl.num_programs(0))
    def _():
        gather_tile(dest_next_ref, 1 - slot)

    def drain(grp, carry):
        for _ in range(SUBLANES):
            for buf in (ya_ref, yb_ref):
                row_gather(0, buf, slot, 0, 0).wait()
        return carry

    lax.fori_loop(0, tm // SUBLANES, drain, 0)

    meta = meta_ref[...]
    ya = _unpack_rows(ya_ref[slot].reshape(tm, PACKED))
    yb = _unpack_rows(yb_ref[slot].reshape(tm, PACKED))
    x = x_ref[...] + (meta[:, 4:5] * ya + meta[:, 5:6] * yb)
    hn = _rms_rows(x, g_ref[...]).astype(BF16)
    gate = 1.0 / (1.0 + jnp.exp(-_dot(hn, wg_ref[...])))
    out_ref[...] = x + _dot(p_ref[...].astype(BF16), wp_ref[...]) * gate


def _combine_ple(dest, x2, meta, p2, ln, wg, wp, yout, tm):
    t = x2.shape[0]
    nt = t // tm
    row = lambda i: (i, 0)
    full = lambda shape: pl.BlockSpec(shape, lambda i: (0,) * len(shape))
    return pl.pallas_call(
        functools.partial(_combine_ple_kernel, tm=tm),
        grid_spec=pltpu.PrefetchScalarGridSpec(
            num_scalar_prefetch=0,
            grid=(nt,),
            in_specs=[pl.BlockSpec((1, 1, 2 * tm), lambda i: (i, 0, 0), memory_space=pltpu.SMEM),
                      pl.BlockSpec((1, 1, 2 * tm), lambda i: (jnp.minimum(i + 1, nt - 1), 0, 0),
                                   memory_space=pltpu.SMEM),
                      pl.BlockSpec((tm, D_MODEL), row), pl.BlockSpec((tm, LANES), row),
                      pl.BlockSpec((tm, PLE_DIM), row), full((1, D_MODEL)), full((D_MODEL, D_MODEL)),
                      full((PLE_DIM, D_MODEL)), pl.BlockSpec(memory_space=pl.ANY)],
            out_specs=pl.BlockSpec((tm, D_MODEL), row),
            scratch_shapes=[pltpu.VMEM((2, tm // SUBLANES, SUBLANES, PACKED), U32),
                            pltpu.VMEM((2, tm // SUBLANES, SUBLANES, PACKED), U32),
                            pltpu.SemaphoreType.DMA((2,))]),
        out_shape=jax.ShapeDtypeStruct((t, D_MODEL), F32),
        compiler_params=_params(("arbitrary",)),
        name="moe_combine_ple",
    )(dest, dest, x2, meta, p2, ln, wg, wp, yout)


MOE_TM = 512


def _moe_ple(x2, p2, ln_ffn, w_rg, b_rg, w_re, b_re, w1, w3, w2, ln_ple, w_gate, w_proj):
    t = x2.shape[0]
    tm = MOE_TM
    wr = jnp.concatenate([w_rg, w_re, jnp.zeros((D_MODEL, LANES - ROUTE_LANES), w_rg.dtype)], axis=1).astype(F32)
    wr_hi = wr.astype(BF16)
    wr_lo = (wr - wr_hi.astype(F32)).astype(BF16)
    br = jnp.concatenate([b_rg, b_re, jnp.zeros((LANES - ROUTE_LANES,), b_rg.dtype)]).astype(F32)[None, :]
    ii = jnp.arange(tm, dtype=jnp.int32)
    lstrict = jnp.where(ii[:, None] > ii[None, :], 1.0, 0.0).astype(BF16)
    meta, cnt = _router(x2, ln_ffn[None, :], jnp.stack([wr_hi, wr_lo]), br, lstrict, tm)

    counts = cnt[0, :N_EXPERTS].astype(jnp.int32)
    padded = (counts + MOE_BLOCK - 1) // MOE_BLOCK * MOE_BLOCK
    pad_end = jnp.cumsum(padded)
    start = (pad_end - padded).astype(jnp.int32)
    nb = (t * 2) // MOE_BLOCK + N_EXPERTS
    first_row = jnp.arange(nb, dtype=jnp.int32) * MOE_BLOCK
    block_e = jnp.minimum(jnp.sum((pad_end[None, :] <= first_row[:, None]).astype(jnp.int32), axis=1),
                          N_EXPERTS - 1).astype(jnp.int32)
    nused = (pad_end[-1:] // MOE_BLOCK).astype(jnp.int32)
    er = meta[:, :4].astype(jnp.int32)
    eid = jnp.arange(N_EXPERTS, dtype=jnp.int32)
    seg_start = jnp.sum(jnp.where(er[:, :2, None] == eid, start, 0), axis=-1)
    dest = (seg_start + er[:, 2:4]).reshape(t // tm, 1, 2 * tm)

    xin = _dispatch(pad_end.astype(jnp.int32), nused, dest, x2, ln_ffn[None, :], nb, tm)
    yout = _experts(block_e, nused, xin, w1.astype(BF16), w3.astype(BF16), w2.astype(BF16))
    return _combine_ple(dest, x2, meta, p2, ln_ple[None, :], w_gate.astype(BF16), w_proj.astype(BF16), yout, tm)


def _trunk(x, p, prm):
    bsz, seq, _ = x.shape
    x2 = x.reshape(bsz * seq, D_MODEL)
    for i in range(DEPTH):
        j = i // 2
        if i % 2 == 0:
            x2 = _even_mixer(x2, bsz, seq, prm["ln_mix_e"][j], prm["w_in_e"][j], prm["q_gain"][j],
                             prm["k_gain"][j], prm["sink"][j], prm["w_out_e"][j])
        else:
            x2 = _odd_mixer(x2, bsz, seq, prm["ln_mix_o"][j], prm["w_in_o"][j], prm["conv_w"][j],
                            prm["conv_b"][j], prm["dt_bias"][j], prm["a_log"][j], prm["d_skip"][j],
                            prm["ssm_gain"][j], prm["w_out_o"][j])
        x2 = _moe_ple(x2, p[i].reshape(bsz * seq, PLE_DIM), prm["ln_ffn"][i], prm["w_router_g"][i],
                      prm["b_router_g"][i], prm["w_router_e"][i], prm["b_router_e"][i], prm["w1"][i],
                      prm["w3"][i], prm["w2"][i], prm["ln_ple"][i], prm["w_ple_gate"][i], prm["w_ple_proj"][i])
    return x2.reshape(bsz, seq, D_MODEL)


def kernel(x_prompt, x_sample, p_prompt, p_sample, ln_mix_e, w_in_e, q_gain, k_gain, sink, w_out_e, ln_mix_o, w_in_o, conv_w, conv_b, dt_bias, a_log, d_skip, ssm_gain, w_out_o, ln_ffn, w_router_g, b_router_g, w_router_e, b_router_e, w1, w3, w2, ln_ple, w_ple_gate, w_ple_proj):
    prm = dict(ln_mix_e=ln_mix_e, w_in_e=w_in_e, q_gain=q_gain, k_gain=k_gain, sink=sink,
               w_out_e=w_out_e, ln_mix_o=ln_mix_o, w_in_o=w_in_o, conv_w=conv_w, conv_b=conv_b,
               dt_bias=dt_bias, a_log=a_log, d_skip=d_skip, ssm_gain=ssm_gain, w_out_o=w_out_o,
               ln_ffn=ln_ffn, w_router_g=w_router_g, b_router_g=b_router_g,
               w_router_e=w_router_e, b_router_e=b_router_e, w1=w1, w3=w3, w2=w2,
               ln_ple=ln_ple, w_ple_gate=w_ple_gate, w_ple_proj=w_ple_proj)
    return (_trunk(x_prompt, p_prompt, prm), _trunk(x_sample, p_sample, prm))
```

```python
import functools
import math

import jax
import jax.numpy as jnp
from jax import lax
from jax.experimental import pallas as pl
from jax.experimental.pallas import tpu as pltpu

D_MODEL = 1024
DEPTH = 4
RMS_EPS = 1e-6
N_FOURIER_GROUPS = 4
FOURIER_WIDTH = 512
FOURIER_GROUP = 128
FFT_N2 = 128
HEAD_DIM = 64
N_Q_HEADS = 8
N_KV_HEADS = 2
ATTN_WIDTH = 512
KV_WIDTH = 128
ATTN_BLOCK = 128
ROPE_THETA = 10000.0
D_INNER = 2048
SSM_HEAD_DIM = 64
SSM_HEADS = 32
SSM_STATE = 128
SSM_GROUPS = 4
HEADS_PER_GROUP = 8
CONV_K = 5
CHUNK = 128
GN = SSM_GROUPS * SSM_STATE
CONV_DIM = D_INNER + 2 * GN
N_EXPERT_GROUPS = 4
EXPERTS_PER_GROUP = 8
N_EXPERTS = 32
D_EXPERT = 512
MOE_BLOCK = 256
PLE_DIM = 256

LANES = 128
NEG_BIG = -1e30
VMEM_LIMIT = 52 * 1024 * 1024

F32 = jnp.float32
BF16 = jnp.bfloat16


def _params(sem, vmem=VMEM_LIMIT):
    return pltpu.CompilerParams(dimension_semantics=sem, vmem_limit_bytes=vmem)


def _dot(a, b):
    return jnp.dot(a, b, preferred_element_type=F32)


def _dot_nt(a, b):
    return lax.dot_general(a, b, (((1,), (1,)), ((), ())), preferred_element_type=F32)


def _split3(v):
    hi = v.astype(BF16)
    r1 = v - hi.astype(F32)
    mid = r1.astype(BF16)
    lo = (r1 - mid.astype(F32)).astype(BF16)
    return hi, mid, lo


def _dot3(v, m_bf16):
    hi, mid, lo = _split3(v)
    return _dot(hi, m_bf16) + _dot(mid, m_bf16) + _dot(lo, m_bf16)


def _dot3_left(m_bf16, v):
    hi, mid, lo = _split3(v)
    return _dot(m_bf16, hi) + _dot(m_bf16, mid) + _dot(m_bf16, lo)


def _rms_rows(x, g):
    return x * lax.rsqrt(jnp.mean(x * x, axis=-1, keepdims=True) + RMS_EPS) * g


def _silu(x):
    return x * (1.0 / (1.0 + jnp.exp(-x)))


def _full(shape):
    return pl.BlockSpec(shape, lambda *_: (0,) * len(shape))


EVEN_IN_SUB = 256


def _even_in_kernel(x_ref, g_ref, w_ref, dft_ref, bd_ref, qg_ref, kg_ref, cos_ref, sin_ref,
                    gout_ref, q_ref, k_ref, v_ref):
    for j in range(x_ref.shape[0] // EVEN_IN_SUB):
        _even_in_rows(slice(j * EVEN_IN_SUB, (j + 1) * EVEN_IN_SUB), x_ref, g_ref, w_ref, dft_ref, bd_ref,
                      qg_ref, kg_ref, cos_ref, sin_ref, gout_ref, q_ref, k_ref, v_ref)


def _even_in_rows(rows, x_ref, g_ref, w_ref, dft_ref, bd_ref, qg_ref, kg_ref, cos_ref, sin_ref,
                  gout_ref, q_ref, k_ref, v_ref):
    h = _rms_rows(x_ref[rows, :], g_ref[...]).astype(BF16)
    u = _dot(h, w_ref[...])
    a = u[:, :FOURIER_WIDTH].astype(BF16)
    for g in range(N_FOURIER_GROUPS):
        r = _dot(a[:, g * LANES:(g + 1) * LANES], dft_ref[...])
        gout_ref[rows, g * LANES:(g + 1) * LANES] = r[:, :LANES].astype(BF16)
        gout_ref[rows, FOURIER_WIDTH + g * LANES:FOURIER_WIDTH + (g + 1) * LANES] = r[:, LANES:].astype(BF16)

    cos = cos_ref[rows, :]
    sin = sin_ref[rows, :]

    def norm_rope(t, gain, width):
        n = width // LANES
        bd = bd_ref[:width, :width]
        t2 = t * t
        hi = t2.astype(BF16)
        lo = (t2 - hi.astype(F32)).astype(BF16)
        ms = _dot(hi, bd) + _dot(lo, bd)
        tn = t * lax.rsqrt(ms + RMS_EPS) * gain
        lane = lax.broadcasted_iota(jnp.int32, tn.shape, 1)
        first_half = (lane % HEAD_DIM) < (HEAD_DIM // 2)
        rot = jnp.where(first_half, pltpu.roll(tn, width - HEAD_DIM // 2, 1), pltpu.roll(tn, HEAD_DIM // 2, 1))
        c = jnp.concatenate([cos] * n, axis=1)
        s = jnp.concatenate([sin] * n, axis=1)
        return tn * c + rot * s

    q = norm_rope(u[:, 512:1024], qg_ref[...], ATTN_WIDTH) * (HEAD_DIM ** -0.5)
    k = norm_rope(u[:, 1024:1280], kg_ref[...], 2 * KV_WIDTH)
    q_ref[rows, :] = q.astype(BF16)
    k_ref[rows, :] = k.astype(BF16)
    v_ref[rows, :] = u[:, 1280:1536].astype(BF16)


def _even_in(x2, seq, ln, w, dft, bd, qg, kg, cos_t, sin_t, tm=512):
    t = x2.shape[0]
    nseq = seq // tm
    row = lambda i: (i, 0)
    pos = lambda i: (i % nseq, 0)
    return pl.pallas_call(
        _even_in_kernel,
        grid=(t // tm,),
        in_specs=[pl.BlockSpec((tm, D_MODEL), row), _full((1, D_MODEL)), _full((D_MODEL, 1536)),
                  _full((LANES, 2 * LANES)), _full((512, 512)), _full((1, 512)), _full((1, 256)),
                  pl.BlockSpec((tm, LANES), pos), pl.BlockSpec((tm, LANES), pos)],
        out_specs=[pl.BlockSpec((tm, 1024), row), pl.BlockSpec((tm, 512), row),
                   pl.BlockSpec((tm, 256), row), pl.BlockSpec((tm, 256), row)],
        out_shape=[jax.ShapeDtypeStruct((t, 1024), BF16), jax.ShapeDtypeStruct((t, 512), BF16),
                   jax.ShapeDtypeStruct((t, 256), BF16), jax.ShapeDtypeStruct((t, 256), BF16)],
        compiler_params=_params(("parallel",)),
        name="even_in",
    )(x2, ln, w, dft, bd, qg, kg, cos_t, sin_t)


def _fft1_kernel(g_ref, c_ref, s_ref, y_ref):
    g = g_ref[0]
    p = _dot(c_ref[...], g)
    q = _dot(s_ref[...], g)
    for j in range(g.shape[1] // 1024):
        o = j * 1024
        y_ref[0, :, o:o + 512] = (p[:, o:o + 512] + q[:, o + 512:o + 1024]).astype(BF16)
        y_ref[0, :, o + 512:o + 1024] = (p[:, o + 512:o + 1024] - q[:, o:o + 512]).astype(BF16)


def _fft1(g3, c1, s1, cb=8192):
    b, n1, width = g3.shape
    blk = lambda i, j: (i, 0, j)
    return pl.pallas_call(
        _fft1_kernel,
        grid=(b, width // cb),
        in_specs=[pl.BlockSpec((1, n1, cb), blk), _full((n1, n1)), _full((n1, n1))],
        out_specs=pl.BlockSpec((1, n1, cb), blk),
        out_shape=jax.ShapeDtypeStruct(g3.shape, BF16),
        compiler_params=_params(("parallel", "parallel")),
        name="fft_stage1",
    )(g3, c1, s1)


def _fft2_kernel(y_ref, mc_ref, ms_ref, o_ref, *, scale):
    for j in range(y_ref.shape[1]):
        yr = y_ref[0, j, :, :512]
        yi = y_ref[0, j, :, 512:]
        r = _dot(mc_ref[j], yr) + _dot(ms_ref[j], yi)
        o_ref[0, :, j * 512:(j + 1) * 512] = (r * scale).astype(BF16)


def _fft2(y4, mc, ms, scale, kb=8):
    b, n1, n2, _ = y4.shape
    return pl.pallas_call(
        functools.partial(_fft2_kernel, scale=scale),
        grid=(n1 // kb, b),
        in_specs=[pl.BlockSpec((1, kb, n2, 1024), lambda k, i: (i, k, 0, 0)),
                  pl.BlockSpec((kb, n2, n2), lambda k, i: (k, 0, 0)),
                  pl.BlockSpec((kb, n2, n2), lambda k, i: (k, 0, 0))],
        out_specs=pl.BlockSpec((1, n2, kb * 512), lambda k, i: (i, 0, k)),
        out_shape=jax.ShapeDtypeStruct((b, n2, n1 * 512), BF16),
        compiler_params=_params(("parallel", "parallel")),
        name="fft_stage2",
    )(y4, mc, ms)


def _dft_tables(seq):
    n1, n2 = seq // FFT_N2, FFT_N2
    i1 = jnp.arange(n1, dtype=jnp.int32)
    ang1 = ((i1[:, None] * i1[None, :]) % n1).astype(F32) * (2.0 * math.pi / n1)
    c1, s1 = jnp.cos(ang1).astype(BF16), jnp.sin(ang1).astype(BF16)
    i2 = jnp.arange(n2, dtype=jnp.int32)
    ph = (i2[None, None, :] * i1[:, None, None] + n1 * i2[None, :, None] * i2[None, None, :]) % seq
    ang2 = ph.astype(F32) * (2.0 * math.pi / seq)
    mc, ms = jnp.cos(ang2).astype(BF16), jnp.sin(ang2).astype(BF16)
    ic = jnp.arange(FOURIER_GROUP, dtype=jnp.int32)
    angc = ((ic[:, None] * ic[None, :]) % FOURIER_GROUP).astype(F32) * (2.0 * math.pi / FOURIER_GROUP)
    dftc = jnp.concatenate([jnp.cos(angc), -jnp.sin(angc)], axis=1).astype(BF16)
    return c1, s1, mc, ms, dftc


def _attn_kernel(sink_ref, q_ref, kp_ref, ko_ref, kn_ref, vp_ref, vo_ref, vn_ref, o_ref, kc_ref, vc_ref):
    i = pl.program_id(1)
    last = pl.num_programs(1) - 1
    tq = q_ref.shape[0]
    nb = tq // ATTN_BLOCK
    kc_ref[0:ATTN_BLOCK] = kp_ref[...]
    kc_ref[ATTN_BLOCK:ATTN_BLOCK + tq] = ko_ref[...]
    kc_ref[ATTN_BLOCK + tq:] = kn_ref[...]
    vc_ref[0:ATTN_BLOCK] = vp_ref[...]
    vc_ref[ATTN_BLOCK:ATTN_BLOCK + tq] = vo_ref[...]
    vc_ref[ATTN_BLOCK + tq:] = vn_ref[...]

    w3 = 3 * ATTN_BLOCK
    qpos = lax.broadcasted_iota(jnp.int32, (ATTN_BLOCK, w3), 0) + ATTN_BLOCK
    kpos = lax.broadcasted_iota(jnp.int32, (ATTN_BLOCK, w3), 1)
    band = jnp.abs(kpos - qpos) <= ATTN_BLOCK
    lane = lax.broadcasted_iota(jnp.int32, (w3, LANES), 1)
    lo_half = lane < HEAD_DIM
    out_lane = lax.broadcasted_iota(jnp.int32, (ATTN_BLOCK, LANES), 1) < HEAD_DIM

    for jb in range(nb):
        ok = band
        if jb == 0:
            ok = ok & ((kpos >= ATTN_BLOCK) | (i > 0))
        if jb == nb - 1:
            ok = ok & ((kpos < 2 * ATTN_BLOCK) | (i < last))
        bias = jnp.where(ok, 0.0, NEG_BIG).astype(F32)
        qb = q_ref[jb * ATTN_BLOCK:(jb + 1) * ATTN_BLOCK, :]
        kw = kc_ref[jb * ATTN_BLOCK:jb * ATTN_BLOCK + w3, :]
        vw = vc_ref[jb * ATTN_BLOCK:jb * ATTN_BLOCK + w3, :]
        for g in range(N_KV_HEADS):
            kg = kw[:, g * LANES:(g + 1) * LANES]
            vg = vw[:, g * LANES:(g + 1) * LANES]
            zero = jnp.zeros_like(kg)
            kbd = jnp.concatenate([jnp.where(lo_half, kg, zero), jnp.where(lo_half, zero, kg)], axis=0)
            vbd = jnp.concatenate([jnp.where(lo_half, vg, zero), jnp.where(lo_half, zero, vg)], axis=0)
            for pr in range(2):
                col = (2 * g + pr) * LANES
                s = _dot_nt(qb[:, col:col + LANES], kbd)
                es, rs = [], []
                for half in range(2):
                    sk = sink_ref[4 * g + 2 * pr + half]
                    sh = s[:, half * w3:(half + 1) * w3] + bias
                    m = jnp.maximum(jnp.max(sh, axis=-1, keepdims=True), sk)
                    e = jnp.exp(sh - m)
                    den = jnp.sum(e, axis=-1, keepdims=True) + jnp.exp(sk - m)
                    es.append(e.astype(BF16))
                    rs.append(1.0 / den)
                o = _dot(jnp.concatenate(es, axis=1), vbd)
                o = o * jnp.where(out_lane, rs[0], rs[1])
                o_ref[jb * ATTN_BLOCK:(jb + 1) * ATTN_BLOCK, col:col + LANES] = o.astype(BF16)


def _attention(q, k, v, sink, bsz, seq, tq=512):
    t = q.shape[0]
    nq = seq // tq
    r = tq // ATTN_BLOCK
    nblk = t // ATTN_BLOCK
    own = lambda b, i, s: (b * nq + i, 0)
    prev = lambda b, i, s: (jnp.maximum((b * nq + i) * r - 1, 0), 0)
    nxt = lambda b, i, s: (jnp.minimum((b * nq + i + 1) * r, nblk - 1), 0)
    kv = lambda m: pl.BlockSpec((ATTN_BLOCK if m is not own else tq, 256), m)
    return pl.pallas_call(
        _attn_kernel,
        grid_spec=pltpu.PrefetchScalarGridSpec(
            num_scalar_prefetch=1,
            grid=(bsz, nq),
            in_specs=[pl.BlockSpec((tq, 512), own), kv(prev), kv(own), kv(nxt), kv(prev), kv(own), kv(nxt)],
            out_specs=pl.BlockSpec((tq, 512), own),
            scratch_shapes=[pltpu.VMEM((tq + 2 * ATTN_BLOCK, 256), BF16),
                            pltpu.VMEM((tq + 2 * ATTN_BLOCK, 256), BF16)]),
        out_shape=jax.ShapeDtypeStruct((t, 512), BF16),
        compiler_params=_params(("parallel", "parallel")),
        name="window_attention",
    )(sink, q, k, k, k, v, v, v)


def _even_out_kernel(x_ref, f_ref, o_ref, w_ref, out_ref):
    out_ref[...] = (x_ref[...] + _dot(f_ref[...], w_ref[:512, :]) + _dot(o_ref[...], w_ref[512:, :]))


def _even_out(x2, f, o, w, tm=512):
    t = x2.shape[0]
    row = lambda i: (i, 0)
    return pl.pallas_call(
        _even_out_kernel,
        grid=(t // tm,),
        in_specs=[pl.BlockSpec((tm, D_MODEL), row), pl.BlockSpec((tm, 512), row),
                  pl.BlockSpec((tm, 512), row), _full((1024, D_MODEL))],
        out_specs=pl.BlockSpec((tm, D_MODEL), row),
        out_shape=jax.ShapeDtypeStruct((t, D_MODEL), F32),
        compiler_params=_params(("parallel",)),
        name="even_out",
    )(x2, f, o, w)


def _even_mixer(x2, bsz, seq, ln, w_in, q_gain, k_gain, sink, w_out):
    c1, s1, mc, ms, dftc = _dft_tables(seq)
    n1 = seq // FFT_N2
    kcols = w_in[:, 1024:1152]
    vcols = w_in[:, 1152:1280]
    dup = lambda c: jnp.concatenate([c[:, :64], c[:, :64], c[:, 64:], c[:, 64:]], axis=1)
    w = jnp.concatenate([w_in[:, :1024], dup(kcols), dup(vcols)], axis=1).astype(BF16)
    hid = jnp.arange(512, dtype=jnp.int32) // HEAD_DIM
    bd = jnp.where(hid[:, None] == hid[None, :], 1.0 / HEAD_DIM, 0.0).astype(BF16)
    inv = 1.0 / (ROPE_THETA ** (jnp.arange(0, HEAD_DIM, 2, dtype=F32) / HEAD_DIM))
    ang = jnp.arange(seq, dtype=F32)[:, None] * inv[None, :]
    cos, sin = jnp.cos(ang), jnp.sin(ang)
    cos_t = jnp.concatenate([cos, cos, cos, cos], axis=1)
    sin_t = jnp.concatenate([-sin, sin, -sin, sin], axis=1)
    qg = jnp.tile(q_gain.astype(F32), N_Q_HEADS)[None, :]
    kg = jnp.tile(k_gain.astype(F32), 2 * N_KV_HEADS)[None, :]
    g, q, k, v = _even_in(x2, seq, ln[None, :], w, dftc, bd, qg, kg, cos_t, sin_t)
    y = _fft1(g.reshape(bsz, n1, FFT_N2 * 1024), c1, s1)
    scale = 1.0 / math.sqrt(seq * FOURIER_GROUP)
    f = _fft2(y.reshape(bsz, n1, FFT_N2, 1024), mc, ms, scale).reshape(bsz * seq, 512)
    o = _attention(q, k, v, sink.astype(F32), bsz, seq)
    return _even_out(x2, f, o, w_out.astype(BF16))


def _odd_in_kernel(x_ref, g_ref, w_ref, z_ref, xbc_ref, dt_ref):
    h = _rms_rows(x_ref[...], g_ref[...]).astype(BF16)
    z_ref[...] = _dot(h, w_ref[:, :D_INNER]).astype(BF16)
    xbc_ref[...] = _dot(h, w_ref[:, D_INNER:D_INNER + CONV_DIM]).astype(BF16)
    dt_ref[...] = _dot(h, w_ref[:, D_INNER + CONV_DIM:])


def _odd_in(x2, ln, w, tm=256):
    t = x2.shape[0]
    row = lambda i: (i, 0)
    wcols = D_INNER + CONV_DIM + LANES
    return pl.pallas_call(
        _odd_in_kernel,
        grid=(t // tm,),
        in_specs=[pl.BlockSpec((tm, D_MODEL), row), _full((1, D_MODEL)), _full((D_MODEL, wcols))],
        out_specs=[pl.BlockSpec((tm, D_INNER), row), pl.BlockSpec((tm, CONV_DIM), row),
                   pl.BlockSpec((tm, LANES), row)],
        out_shape=[jax.ShapeDtypeStruct((t, D_INNER), BF16), jax.ShapeDtypeStruct((t, CONV_DIM), BF16),
                   jax.ShapeDtypeStruct((t, LANES), F32)],
        compiler_params=_params(("parallel",)),
        name="odd_in",
    )(x2, ln, w)


CONV_HALO = 16


def _conv_kernel(prev_ref, main_ref, next_ref, w_ref, b_ref, o_ref, win_ref, *, tiles_per_seq):
    i = pl.program_id(0) % tiles_per_seq
    tc = main_ref.shape[0]
    keep_prev = (i > 0).astype(F32)
    keep_next = (i < tiles_per_seq - 1).astype(F32)
    win_ref[0:CONV_HALO, :] = prev_ref[...].astype(F32) * keep_prev
    win_ref[CONV_HALO:CONV_HALO + tc, :] = main_ref[...].astype(F32)
    win_ref[CONV_HALO + tc:, :] = next_ref[...].astype(F32) * keep_next
    cb = 512
    for c in range(CONV_DIM // cb):
        cols = slice(c * cb, (c + 1) * cb)
        win = win_ref[:, cols]
        acc = jnp.broadcast_to(b_ref[:, cols], (tc, cb))
        for k in range(CONV_K):
            shift = CONV_K // 2 - k
            rolled = win if shift == 0 else pltpu.roll(win, shift % win.shape[0], 0)
            acc = acc + rolled[CONV_HALO:CONV_HALO + tc, :] * w_ref[k:k + 1, cols]
        o_ref[:, cols] = _silu(acc).astype(BF16)


def _conv(xbc, seq, w, b, tc=512):
    t = xbc.shape[0]
    r = tc // CONV_HALO
    nh = t // CONV_HALO
    return pl.pallas_call(
        functools.partial(_conv_kernel, tiles_per_seq=seq // tc),
        grid=(t // tc,),
        in_specs=[pl.BlockSpec((CONV_HALO, CONV_DIM), lambda i: (jnp.maximum(i * r - 1, 0), 0)),
                  pl.BlockSpec((tc, CONV_DIM), lambda i: (i, 0)),
                  pl.BlockSpec((CONV_HALO, CONV_DIM), lambda i: (jnp.minimum((i + 1) * r, nh - 1), 0)),
                  _full((CONV_K, CONV_DIM)), _full((1, CONV_DIM))],
        out_specs=pl.BlockSpec((tc, CONV_DIM), lambda i: (i, 0)),
        out_shape=jax.ShapeDtypeStruct((t, CONV_DIM), BF16),
        scratch_shapes=[pltpu.VMEM((tc + 2 * CONV_HALO, CONV_DIM), F32)],
        compiler_params=_params(("parallel",)),
        name="ssd_conv",
    )(xbc, xbc, xbc, w, b)


N_PAIRS = SSM_HEADS // 2
PAIRS_PER_GROUP = HEADS_PER_GROUP // 2


LOG2E = math.log2(math.e)
SSD_STEP = 2 * CHUNK


def _ssd_chunk(xbc_ref, dt_ref, bias, a_row, expand, y_ref, state_ref, r0, reverse, mask, tri_col, lo_half):
    off = SSM_HEADS if reverse else 0
    rows = slice(r0, r0 + CHUNK)
    dt_lh = dt_ref[rows, :] + bias
    dt_lh = jnp.maximum(dt_lh, 0.0) + jnp.log(1.0 + jnp.exp(-jnp.abs(dt_lh)))
    acs_lh = _dot3_left(tri_col, dt_lh * a_row)
    edge = 0 if reverse else CHUNK - 1
    acs2_lh = acs_lh * LOG2E
    tot2_row = acs2_lh[edge:edge + 1, :]
    r_hl = acs2_lh.T
    dt_hi = dt_lh.astype(BF16)
    dt_mid = (dt_lh - dt_hi.astype(F32)).astype(BF16)
    dt_exp = _dot(dt_hi, expand) + _dot(dt_mid, expand)
    lo_row = lo_half[0:1, :]

    for g in range(SSM_GROUPS):
        b_g = xbc_ref[rows, D_INNER + g * SSM_STATE:D_INNER + (g + 1) * SSM_STATE]
        c_g = xbc_ref[rows, D_INNER + GN + g * SSM_STATE:D_INNER + GN + (g + 1) * SSM_STATE]
        cb_b = _dot_nt(c_g, b_g).astype(BF16)
        bt = b_g.astype(F32).T.astype(BF16)
        gcols = slice(g * PAIRS_PER_GROUP * LANES, (g + 1) * PAIRS_PER_GROUP * LANES)
        y_off = _dot(c_g, state_ref[:, gcols].astype(BF16))
        for pp in range(PAIRS_PER_GROUP):
            p = g * PAIRS_PER_GROUP + pp
            hd0, hd1 = off + 2 * p, off + 2 * p + 1
            colb = [jnp.broadcast_to(acs2_lh[:, hd:hd + 1], (CHUNK, CHUNK)) for hd in (hd0, hd1)]
            ms = [cb_b * jnp.exp2(jnp.where(mask, cb_l - r_hl[hd:hd + 1, :], NEG_BIG)).astype(BF16)
                  for cb_l, hd in zip(colb, (hd0, hd1))]
            pcols = slice(p * LANES, (p + 1) * LANES)
            xdt = xbc_ref[rows, pcols].astype(F32) * dt_exp[:, pcols]
            xdt_b = xdt.astype(BF16)
            zx = jnp.zeros_like(xdt_b)
            xbd = jnp.concatenate([jnp.where(lo_half, xdt_b, zx), jnp.where(lo_half, zx, xdt_b)], axis=0)
            colsel = jnp.where(lo_half, colb[0], colb[1])
            y = _dot(jnp.concatenate(ms, axis=1), xbd)
            y = y + jnp.exp2(colsel) * y_off[:, pp * LANES:(pp + 1) * LANES]
            y_ref[rows, pcols] = y.astype(BF16)
            tot2 = jnp.where(lo_row, tot2_row[:, hd0:hd0 + 1], tot2_row[:, hd1:hd1 + 1])
            snew = _dot(bt, (xdt * jnp.exp2(tot2 - colsel)).astype(BF16))
            state_ref[:, pcols] = state_ref[:, pcols] * jnp.exp2(tot2) + snew


def _ssd_kernel(xbc_ref, dt_ref, bias_ref, alog_ref, exp_ref, y_ref, state_ref, *, reverse):
    @pl.when(pl.program_id(1) == 0)
    def _():
        state_ref[...] = jnp.zeros_like(state_ref)

    r_io = lax.broadcasted_iota(jnp.int32, (CHUNK, CHUNK), 0)
    c_io = lax.broadcasted_iota(jnp.int32, (CHUNK, CHUNK), 1)
    mask = (r_io <= c_io) if reverse else (r_io >= c_io)
    tri_col = jnp.where(mask, 1.0, 0.0).astype(BF16)
    lo_half = lax.broadcasted_iota(jnp.int32, (CHUNK, LANES), 1) < SSM_HEAD_DIM
    bias = bias_ref[...]
    a_row = -jnp.exp(alog_ref[...])
    n_sub = SSD_STEP // CHUNK
    order = range(n_sub - 1, -1, -1) if reverse else range(n_sub)
    for sub in order:
        _ssd_chunk(xbc_ref, dt_ref, bias, a_row, exp_ref[...], y_ref, state_ref, sub * CHUNK, reverse,
                   mask, tri_col, lo_half)


def _ssd(xbc_c, dt_raw, bias_row, alog_row, expand, bsz, seq, reverse):
    t = xbc_c.shape[0]
    ns = seq // SSD_STEP
    if reverse:
        blk = lambda b, c: (b * ns + ns - 1 - c, 0)
    else:
        blk = lambda b, c: (b * ns + c, 0)
    return pl.pallas_call(
        functools.partial(_ssd_kernel, reverse=reverse),
        grid=(bsz, ns),
        in_specs=[pl.BlockSpec((SSD_STEP, CONV_DIM), blk), pl.BlockSpec((SSD_STEP, LANES), blk),
                  _full((1, LANES)), _full((1, LANES)), _full((LANES, D_INNER))],
        out_specs=pl.BlockSpec((SSD_STEP, D_INNER), blk),
        out_shape=jax.ShapeDtypeStruct((t, D_INNER), BF16),
        scratch_shapes=[pltpu.VMEM((SSM_STATE, D_INNER), F32)],
        compiler_params=_params(("parallel", "arbitrary")),
        name="ssd_scan_bwd" if reverse else "ssd_scan_fwd",
    )(xbc_c, dt_raw, bias_row, alog_row, expand)


ODD_OUT_SUB = 256


def _odd_out_kernel(x_ref, yf_ref, yb_ref, xs_ref, z_ref, d_ref, g_ref, w_ref, out_ref):
    gw = D_INNER // SSM_GROUPS
    sub = ODD_OUT_SUB
    for j in range(x_ref.shape[0] // sub):
        rows = slice(j * sub, (j + 1) * sub)
        y = yf_ref[rows, :].astype(F32) + yb_ref[rows, :].astype(F32) + d_ref[...] * xs_ref[rows, :].astype(F32)
        y = y * _silu(z_ref[rows, :].astype(F32))
        parts = []
        for g in range(SSM_GROUPS):
            yg = y[:, g * gw:(g + 1) * gw]
            parts.append(yg * lax.rsqrt(jnp.mean(yg * yg, axis=-1, keepdims=True) + RMS_EPS))
        yn = (jnp.concatenate(parts, axis=1) * g_ref[...]).astype(BF16)
        out_ref[rows, :] = x_ref[rows, :] + _dot(yn, w_ref[...])


def _odd_out(x2, yf, yb, xbc_c, z, dskip, gain, w, tm=512):
    t = x2.shape[0]
    row = lambda i: (i, 0)
    return pl.pallas_call(
        _odd_out_kernel,
        grid=(t // tm,),
        in_specs=[pl.BlockSpec((tm, D_MODEL), row), pl.BlockSpec((tm, D_INNER), row),
                  pl.BlockSpec((tm, D_INNER), row), pl.BlockSpec((tm, D_INNER), row),
                  pl.BlockSpec((tm, D_INNER), row), _full((1, D_INNER)), _full((1, D_INNER)),
                  _full((D_INNER, D_MODEL))],
        out_specs=pl.BlockSpec((tm, D_MODEL), row),
        out_shape=jax.ShapeDtypeStruct((t, D_MODEL), F32),
        compiler_params=_params(("parallel",)),
        name="odd_out",
    )(x2, yf, yb, xbc_c, z, dskip, gain, w)


def _odd_mixer(x2, bsz, seq, ln, w_in, conv_w, conv_b, dt_bias, a_log, d_skip, norm_g, w_out):
    pad = jnp.zeros((D_MODEL, LANES - 2 * SSM_HEADS), w_in.dtype)
    w = jnp.concatenate([w_in, pad], axis=1).astype(BF16)
    z, xbc, dt_raw = _odd_in(x2, ln[None, :], w)
    xbc_c = _conv(xbc, seq, conv_w.astype(F32), conv_b.astype(F32)[None, :])
    lane_pad = jnp.zeros((LANES - 2 * SSM_HEADS,), F32)
    bias_row = jnp.concatenate([dt_bias.astype(F32).reshape(-1), lane_pad])[None, :]
    alog_row = jnp.concatenate([a_log.astype(F32).reshape(-1), lane_pad])[None, :]
    ch = jnp.arange(D_INNER, dtype=jnp.int32) // SSM_HEAD_DIM
    hd = jnp.arange(LANES, dtype=jnp.int32)
    ys = []
    for reverse in (False, True):
        off = SSM_HEADS if reverse else 0
        expand = jnp.where(hd[:, None] == off + ch[None, :], 1.0, 0.0).astype(BF16)
        ys.append(_ssd(xbc_c, dt_raw, bias_row, alog_row, expand, bsz, seq, reverse))
    dskip = jnp.repeat(d_skip.astype(F32), SSM_HEAD_DIM)[None, :]
    return _odd_out(x2, ys[0], ys[1], xbc_c, z, dskip, norm_g.astype(F32)[None, :], w_out.astype(BF16))


ROUTE_LANES = N_EXPERT_GROUPS + N_EXPERTS


def _router_kernel(x_ref, g_ref, w_ref, b_ref, ls_ref, meta_ref, cnt_ref, run_ref):
    i = pl.program_id(0)

    @pl.when(i == 0)
    def _():
        run_ref[...] = jnp.zeros_like(run_ref)

    h = _rms_rows(x_ref[...], g_ref[...])
    hh = h.astype(BF16)
    hl = (h - hh.astype(F32)).astype(BF16)
    w_hi = w_ref[0]
    w_lo = w_ref[1]
    logits = _dot(hh, w_hi) + _dot(hl, w_hi) + _dot(hh, w_lo) + b_ref[...]
    tm = logits.shape[0]
    lane = lax.broadcasted_iota(jnp.int32, (tm, LANES), 1).astype(F32)

    def top1(v):
        m = jnp.max(v, axis=-1, keepdims=True)
        idx = jnp.min(jnp.where(v == m, lane, float(LANES)), axis=-1, keepdims=True)
        return m, idx

    gl = jnp.where(lane < N_EXPERT_GROUPS, logits, NEG_BIG)
    gmax, gidx = top1(gl)
    g_p = 1.0 / jnp.sum(jnp.exp(gl - gmax), axis=-1, keepdims=True)
    lo = N_EXPERT_GROUPS + EXPERTS_PER_GROUP * gidx
    e1 = jnp.where((lane >= lo) & (lane < lo + EXPERTS_PER_GROUP), logits, NEG_BIG)
    m1, i1 = top1(e1)
    e2 = jnp.where(lane == i1, NEG_BIG, e1)
    m2, i2 = top1(e2)
    r = jnp.exp(m2 - m1)
    w_a = g_p / (1.0 + r)
    w_b = g_p * r / (1.0 + r)
    e_a = i1 - N_EXPERT_GROUPS
    e_b = i2 - N_EXPERT_GROUPS

    onehot = jnp.where((lane == e_a) | (lane == e_b), 1.0, 0.0)
    before = _dot(ls_ref[...], onehot.astype(BF16)) + run_ref[...]
    rank_a = jnp.sum(jnp.where(lane == e_a, before, 0.0), axis=-1, keepdims=True)
    rank_b = jnp.sum(jnp.where(lane == e_b, before, 0.0), axis=-1, keepdims=True)
    run_ref[...] = run_ref[...] + jnp.sum(onehot, axis=0, keepdims=True)
    cnt_ref[...] = jnp.broadcast_to(run_ref[...], cnt_ref.shape)

    meta = jnp.where(lane == 0, e_a, 0.0)
    meta = jnp.where(lane == 1, e_b, meta)
    meta = jnp.where(lane == 2, rank_a, meta)
    meta = jnp.where(lane == 3, rank_b, meta)
    meta = jnp.where(lane == 4, w_a, meta)
    meta = jnp.where(lane == 5, w_b, meta)
    meta_ref[...] = meta


def _router(x2, ln, w2, b, lstrict, tm):
    t = x2.shape[0]
    row = lambda i: (i, 0)
    return pl.pallas_call(
        _router_kernel,
        grid=(t // tm,),
        in_specs=[pl.BlockSpec((tm, D_MODEL), row), _full((1, D_MODEL)), _full((2, D_MODEL, LANES)),
                  _full((1, LANES)), _full((tm, tm))],
        out_specs=[pl.BlockSpec((tm, LANES), row), _full((8, LANES))],
        out_shape=[jax.ShapeDtypeStruct((t, LANES), F32), jax.ShapeDtypeStruct((8, LANES), F32)],
        scratch_shapes=[pltpu.VMEM((1, LANES), F32)],
        compiler_params=_params(("arbitrary",)),
        name="moe_router",
    )(x2, ln, w2, b, lstrict)


SUBLANES = 8
PACKED = D_MODEL // 2
DMA_THREADS = 2
U32 = jnp.uint32


def _pack_rows(v):
    bits = lax.bitcast_convert_type(v.astype(BF16).astype(F32), U32)
    return (bits[:, :PACKED] >> 16) | bits[:, PACKED:]


def _unpack_rows(w):
    lo = lax.bitcast_convert_type(w << 16, F32)
    hi = lax.bitcast_convert_type(w & jnp.uint32(0xFFFF0000), F32)
    return jnp.concatenate([lo, hi], axis=1)


def _dispatch_kernel(pad_end_ref, nused_ref, dest_ref, x_ref, g_ref, xin_hbm, h_ref, zero_ref, sem, zsem,
                     *, tm, nb):
    def zero_block(first_row):
        return pltpu.make_async_copy(zero_ref, xin_hbm.at[pl.ds(pl.multiple_of(first_row, MOE_BLOCK), MOE_BLOCK), :],
                                     zsem)

    def seg_zero(action):
        for e in range(N_EXPERTS):
            end = pad_end_ref[e]
            prev = pad_end_ref[e - 1] if e else 0

            @pl.when(end > prev)
            def _():
                action(zero_block(end - MOE_BLOCK))

    def tail_zero(action):
        def body(b, carry):
            action(zero_block(b * MOE_BLOCK))
            return carry

        lax.fori_loop(nused_ref[0], nb, body, 0)

    @pl.when(pl.program_id(0) == 0)
    def _():
        zero_ref[...] = jnp.zeros_like(zero_ref)
        seg_zero(lambda c: c.start())
        tail_zero(lambda c: c.start())
        seg_zero(lambda c: c.wait())
        tail_zero(lambda c: c.wait())

    h_ref[...] = _pack_rows(_rms_rows(x_ref[...], g_ref[...])).reshape(h_ref.shape)

    def row_copy(grp, c, dst_row):
        return pltpu.make_async_copy(h_ref.at[grp, pl.ds(c, 1), :], xin_hbm.at[pl.ds(dst_row, 1), :], sem)

    def issue(grp, carry):
        for c in range(SUBLANES):
            for k in range(2):
                row_copy(grp, c, dest_ref[0, 0, 2 * SUBLANES * grp + 2 * c + k]).start(
                    priority=(2 * c + k) % DMA_THREADS)
        return carry

    lax.fori_loop(0, tm // SUBLANES, issue, 0)

    def drain(grp, carry):
        for _ in range(2 * SUBLANES):
            row_copy(0, 0, 0).wait()
        return carry

    lax.fori_loop(0, tm // SUBLANES, drain, 0)


def _dispatch(pad_end, nused, dest, x2, ln, nb, tm):
    t = x2.shape[0]
    return pl.pallas_call(
        functools.partial(_dispatch_kernel, tm=tm, nb=nb),
        grid_spec=pltpu.PrefetchScalarGridSpec(
            num_scalar_prefetch=2,
            grid=(t // tm,),
            in_specs=[pl.BlockSpec((1, 1, 2 * tm), lambda i, pe, nu: (i, 0, 0), memory_space=pltpu.SMEM),
                      pl.BlockSpec((tm, D_MODEL), lambda i, pe, nu: (i, 0)),
                      pl.BlockSpec((1, D_MODEL), lambda i, pe, nu: (0, 0))],
            out_specs=pl.BlockSpec(memory_space=pl.ANY),
            scratch_shapes=[pltpu.VMEM((tm // SUBLANES, SUBLANES, PACKED), U32),
                            pltpu.VMEM((MOE_BLOCK, PACKED), U32),
                            pltpu.SemaphoreType.DMA, pltpu.SemaphoreType.DMA]),
        out_shape=jax.ShapeDtypeStruct((nb * MOE_BLOCK, PACKED), U32),
        compiler_params=_params(("arbitrary",)),
        name="moe_dispatch",
    )(pad_end, nused, dest, x2, ln)


EXPERT_BLOCKS_PER_STEP = 2


def _experts_kernel(be_ref, nused_ref, x_ref, w1a_ref, w3a_ref, w2a_ref, w1b_ref, w3b_ref, w2b_ref, o_ref):
    i = pl.program_id(0)
    first = EXPERT_BLOCKS_PER_STEP * i

    def swiglu(rows, w1_ref, w3_ref, w2_ref):
        x = _unpack_rows(x_ref[rows, :]).astype(BF16)
        a = _silu(_dot(x, w1_ref[0])) * _dot(x, w3_ref[0])
        o_ref[rows, :] = _pack_rows(_dot(a.astype(BF16), w2_ref[0]))

    fused = (first + 1 < nused_ref[0]) & (be_ref[first] == be_ref[first + 1])

    @pl.when(fused)
    def _():
        swiglu(slice(0, EXPERT_BLOCKS_PER_STEP * MOE_BLOCK), w1a_ref, w3a_ref, w2a_ref)

    for half, wrefs in enumerate(((w1a_ref, w3a_ref, w2a_ref), (w1b_ref, w3b_ref, w2b_ref))):
        rows = slice(half * MOE_BLOCK, (half + 1) * MOE_BLOCK)
        live = first + half < nused_ref[0]

        @pl.when(jnp.logical_not(fused) & live)
        def _():
            swiglu(rows, *wrefs)

        @pl.when(jnp.logical_not(live))
        def _():
            o_ref[rows, :] = jnp.zeros((MOE_BLOCK, PACKED), U32)


def _experts(block_e, nused, xin, w1, w3, w2):
    n = EXPERT_BLOCKS_PER_STEP
    nb = xin.shape[0] // MOE_BLOCK
    assert nb % n == 0
    row = lambda i, be, nu: (i, 0)
    specs = []
    for half in range(n):
        wsel = functools.partial(lambda i, be, nu, half: (be[n * i + half], 0, 0), half=half)
        specs += [pl.BlockSpec((1, D_MODEL, D_EXPERT), wsel), pl.BlockSpec((1, D_MODEL, D_EXPERT), wsel),
                  pl.BlockSpec((1, D_EXPERT, D_MODEL), wsel)]
    return pl.pallas_call(
        _experts_kernel,
        grid_spec=pltpu.PrefetchScalarGridSpec(
            num_scalar_prefetch=2,
            grid=(nb // n,),
            in_specs=[pl.BlockSpec((n * MOE_BLOCK, PACKED), row)] + specs,
            out_specs=pl.BlockSpec((n * MOE_BLOCK, PACKED), row)),
        out_shape=jax.ShapeDtypeStruct(xin.shape, U32),
        compiler_params=_params(("parallel",)),
        name="moe_experts",
    )(block_e, nused, xin, *([w1, w3, w2] * n))


def _combine_ple_kernel(dest_ref, dest_next_ref, x_ref, meta_ref, p_ref, g_ref, wg_ref, wp_ref, y_hbm,
                        out_ref, ya_ref, yb_ref, sem, *, tm):
    i = pl.program_id(0)
    slot = i % 2

    def row_gather(src_row, buf, s, grp, c):
        return pltpu.make_async_copy(y_hbm.at[pl.ds(src_row, 1), :], buf.at[s, grp, pl.ds(c, 1), :], sem.at[s])

    def gather_tile(idx_ref, s):
        def issue(grp, carry):
            for c in range(SUBLANES):
                for k, buf in enumerate((ya_ref, yb_ref)):
                    row_gather(idx_ref[0, 0, 2 * SUBLANES * grp + 2 * c + k], buf, s, grp, c).start(
                        priority=(2 * c + k) % DMA_THREADS)
            return carry

        lax.fori_loop(0, tm // SUBLANES, issue, 0)

    @pl.when(i == 0)
    def _():
        gather_tile(dest_ref, 0)

    @pl.when(i + 1 < pl.num_programs(0))
    def _():
        gather_tile(dest_next_ref, 1 - slot)

    def drain(grp, carry):
        for _ in range(SUBLANES):
            for buf in (ya_ref, yb_ref):
                row_gather(0, buf, slot, 0, 0).wait()
        return carry

    lax.fori_loop(0, tm // SUBLANES, drain, 0)

    sub = COMBINE_SUB
    for j in range(tm // sub):
        rows = slice(j * sub, (j + 1) * sub)
        grps = slice(j * sub // SUBLANES, (j + 1) * sub // SUBLANES)
        meta = meta_ref[rows, :]
        ya = _unpack_rows(ya_ref[slot, grps].reshape(sub, PACKED))
        yb = _unpack_rows(yb_ref[slot, grps].reshape(sub, PACKED))
        x = x_ref[rows, :] + (meta[:, 4:5] * ya + meta[:, 5:6] * yb)
        hn = _rms_rows(x, g_ref[...]).astype(BF16)
        gate = 1.0 / (1.0 + jnp.exp(-_dot(hn, wg_ref[...])))
        out_ref[rows, :] = x + _dot(p_ref[rows, :].astype(BF16), wp_ref[...]) * gate


def _combine_ple(dest, x2, meta, p2, ln, wg, wp, yout, tm):
    t = x2.shape[0]
    nt = t // tm
    row = lambda i: (i, 0)
    full = lambda shape: pl.BlockSpec(shape, lambda i: (0,) * len(shape))
    return pl.pallas_call(
        functools.partial(_combine_ple_kernel, tm=tm),
        grid_spec=pltpu.PrefetchScalarGridSpec(
            num_scalar_prefetch=0,
            grid=(nt,),
            in_specs=[pl.BlockSpec((1, 1, 2 * tm), lambda i: (i, 0, 0), memory_space=pltpu.SMEM),
                      pl.BlockSpec((1, 1, 2 * tm), lambda i: (jnp.minimum(i + 1, nt - 1), 0, 0),
                                   memory_space=pltpu.SMEM),
                      pl.BlockSpec((tm, D_MODEL), row), pl.BlockSpec((tm, LANES), row),
                      pl.BlockSpec((tm, PLE_DIM), row), full((1, D_MODEL)), full((D_MODEL, D_MODEL)),
                      full((PLE_DIM, D_MODEL)), pl.BlockSpec(memory_space=pl.ANY)],
            out_specs=pl.BlockSpec((tm, D_MODEL), row),
            scratch_shapes=[pltpu.VMEM((2, tm // SUBLANES, SUBLANES, PACKED), U32),
                            pltpu.VMEM((2, tm // SUBLANES, SUBLANES, PACKED), U32),
                            pltpu.SemaphoreType.DMA((2,))]),
        out_shape=jax.ShapeDtypeStruct((t, D_MODEL), F32),
        compiler_params=_params(("arbitrary",)),
        name="moe_combine_ple",
    )(dest, dest, x2, meta, p2, ln, wg, wp, yout)


MOE_TM = 512
COMBINE_SUB = 256


def _moe_ple(x2, p2, ln_ffn, w_rg, b_rg, w_re, b_re, w1, w3, w2, ln_ple, w_gate, w_proj):
    t = x2.shape[0]
    tm = MOE_TM
    wr = jnp.concatenate([w_rg, w_re, jnp.zeros((D_MODEL, LANES - ROUTE_LANES), w_rg.dtype)], axis=1).astype(F32)
    wr_hi = wr.astype(BF16)
    wr_lo = (wr - wr_hi.astype(F32)).astype(BF16)
    br = jnp.concatenate([b_rg, b_re, jnp.zeros((LANES - ROUTE_LANES,), b_rg.dtype)]).astype(F32)[None, :]
    ii = jnp.arange(tm, dtype=jnp.int32)
    lstrict = jnp.where(ii[:, None] > ii[None, :], 1.0, 0.0).astype(BF16)
    meta, cnt = _router(x2, ln_ffn[None, :], jnp.stack([wr_hi, wr_lo]), br, lstrict, tm)

    counts = cnt[0, :N_EXPERTS].astype(jnp.int32)
    padded = (counts + MOE_BLOCK - 1) // MOE_BLOCK * MOE_BLOCK
    pad_end = jnp.cumsum(padded)
    start = (pad_end - padded).astype(jnp.int32)
    nb = (t * 2) // MOE_BLOCK + N_EXPERTS
    first_row = jnp.arange(nb, dtype=jnp.int32) * MOE_BLOCK
    block_e = jnp.minimum(jnp.sum((pad_end[None, :] <= first_row[:, None]).astype(jnp.int32), axis=1),
                          N_EXPERTS - 1).astype(jnp.int32)
    nused = (pad_end[-1:] // MOE_BLOCK).astype(jnp.int32)
    er = meta[:, :4].astype(jnp.int32)
    eid = jnp.arange(N_EXPERTS, dtype=jnp.int32)
    seg_start = jnp.sum(jnp.where(er[:, :2, None] == eid, start, 0), axis=-1)
    dest = (seg_start + er[:, 2:4]).reshape(t // tm, 1, 2 * tm)

    xin = _dispatch(pad_end.astype(jnp.int32), nused, dest, x2, ln_ffn[None, :], nb, tm)
    yout = _experts(block_e, nused, xin, w1.astype(BF16), w3.astype(BF16), w2.astype(BF16))
    return _combine_ple(dest, x2, meta, p2, ln_ple[None, :], w_gate.astype(BF16), w_proj.astype(BF16), yout, tm)


def _trunk(x, p, prm):
    bsz, seq, _ = x.shape
    x2 = x.reshape(bsz * seq, D_MODEL)
    for i in range(DEPTH):
        j = i // 2
        if i % 2 == 0:
            x2 = _even_mixer(x2, bsz, seq, prm["ln_mix_e"][j], prm["w_in_e"][j], prm["q_gain"][j],
                             prm["k_gain"][j], prm["sink"][j], prm["w_out_e"][j])
        else:
            x2 = _odd_mixer(x2, bsz, seq, prm["ln_mix_o"][j], prm["w_in_o"][j], prm["conv_w"][j],
                            prm["conv_b"][j], prm["dt_bias"][j], prm["a_log"][j], prm["d_skip"][j],
                            prm["ssm_gain"][j], prm["w_out_o"][j])
        x2 = _moe_ple(x2, p[i].reshape(bsz * seq, PLE_DIM), prm["ln_ffn"][i], prm["w_router_g"][i],
                      prm["b_router_g"][i], prm["w_router_e"][i], prm["b_router_e"][i], prm["w1"][i],
                      prm["w3"][i], prm["w2"][i], prm["ln_ple"][i], prm["w_ple_gate"][i], prm["w_ple_proj"][i])
    return x2.reshape(bsz, seq, D_MODEL)


def kernel(x_prompt, x_sample, p_prompt, p_sample, ln_mix_e, w_in_e, q_gain, k_gain, sink, w_out_e, ln_mix_o, w_in_o, conv_w, conv_b, dt_bias, a_log, d_skip, ssm_gain, w_out_o, ln_ffn, w_router_g, b_router_g, w_router_e, b_router_e, w1, w3, w2, ln_ple, w_ple_gate, w_ple_proj):
    prm = dict(ln_mix_e=ln_mix_e, w_in_e=w_in_e, q_gain=q_gain, k_gain=k_gain, sink=sink,
               w_out_e=w_out_e, ln_mix_o=ln_mix_o, w_in_o=w_in_o, conv_w=conv_w, conv_b=conv_b,
               dt_bias=dt_bias, a_log=a_log, d_skip=d_skip, ssm_gain=ssm_gain, w_out_o=w_out_o,
               ln_ffn=ln_ffn, w_router_g=w_router_g, b_router_g=b_router_g,
               w_router_e=w_router_e, b_router_e=b_router_e, w1=w1, w3=w3, w2=w2,
               ln_ple=ln_ple, w_ple_gate=w_ple_gate, w_ple_proj=w_ple_proj)
    return (_trunk(x_prompt, p_prompt, prm), _trunk(x_sample, p_sample, prm))
```

```python
import functools
import math

import jax
import jax.numpy as jnp
from jax import lax
from jax.experimental import pallas as pl
from jax.experimental.pallas import tpu as pltpu

D_MODEL = 1024
DEPTH = 4
RMS_EPS = 1e-6
N_FOURIER_GROUPS = 4
FOURIER_WIDTH = 512
FOURIER_GROUP = 128
FFT_N2 = 128
HEAD_DIM = 64
N_Q_HEADS = 8
N_KV_HEADS = 2
ATTN_WIDTH = 512
KV_WIDTH = 128
ATTN_BLOCK = 128
ROPE_THETA = 10000.0
D_INNER = 2048
SSM_HEAD_DIM = 64
SSM_HEADS = 32
SSM_STATE = 128
SSM_GROUPS = 4
HEADS_PER_GROUP = 8
CONV_K = 5
CHUNK = 128
GN = SSM_GROUPS * SSM_STATE
CONV_DIM = D_INNER + 2 * GN
N_EXPERT_GROUPS = 4
EXPERTS_PER_GROUP = 8
N_EXPERTS = 32
D_EXPERT = 512
MOE_BLOCK = 256
PLE_DIM = 256

LANES = 128
NEG_BIG = -1e30
VMEM_LIMIT = 52 * 1024 * 1024

F32 = jnp.float32
BF16 = jnp.bfloat16


def _params(sem, vmem=VMEM_LIMIT):
    return pltpu.CompilerParams(dimension_semantics=sem, vmem_limit_bytes=vmem)


def _dot(a, b):
    return jnp.dot(a, b, preferred_element_type=F32)


def _dot_nt(a, b):
    return lax.dot_general(a, b, (((1,), (1,)), ((), ())), preferred_element_type=F32)


def _split3(v):
    hi = v.astype(BF16)
    r1 = v - hi.astype(F32)
    mid = r1.astype(BF16)
    lo = (r1 - mid.astype(F32)).astype(BF16)
    return hi, mid, lo


def _dot3(v, m_bf16):
    hi, mid, lo = _split3(v)
    return _dot(hi, m_bf16) + _dot(mid, m_bf16) + _dot(lo, m_bf16)


def _dot3_left(m_bf16, v):
    hi, mid, lo = _split3(v)
    return _dot(m_bf16, hi) + _dot(m_bf16, mid) + _dot(m_bf16, lo)


def _rms_rows(x, g):
    return x * lax.rsqrt(jnp.mean(x * x, axis=-1, keepdims=True) + RMS_EPS) * g


def _silu(x):
    return x * (1.0 / (1.0 + jnp.exp(-x)))


def _full(shape):
    return pl.BlockSpec(shape, lambda *_: (0,) * len(shape))


EVEN_IN_SUB = 256


def _even_in_kernel(x_ref, g_ref, w_ref, dft_ref, bd_ref, qg_ref, kg_ref, cos_ref, sin_ref,
                    gout_ref, q_ref, k_ref, v_ref):
    for j in range(x_ref.shape[0] // EVEN_IN_SUB):
        _even_in_rows(slice(j * EVEN_IN_SUB, (j + 1) * EVEN_IN_SUB), x_ref, g_ref, w_ref, dft_ref, bd_ref,
                      qg_ref, kg_ref, cos_ref, sin_ref, gout_ref, q_ref, k_ref, v_ref)


def _even_in_rows(rows, x_ref, g_ref, w_ref, dft_ref, bd_ref, qg_ref, kg_ref, cos_ref, sin_ref,
                  gout_ref, q_ref, k_ref, v_ref):
    h = _rms_rows(x_ref[rows, :], g_ref[...]).astype(BF16)
    u = _dot(h, w_ref[...])
    a = u[:, :FOURIER_WIDTH].astype(BF16)
    for g in range(N_FOURIER_GROUPS):
        r = _dot(a[:, g * LANES:(g + 1) * LANES], dft_ref[...])
        gout_ref[rows, g * LANES:(g + 1) * LANES] = r[:, :LANES].astype(BF16)
        gout_ref[rows, FOURIER_WIDTH + g * LANES:FOURIER_WIDTH + (g + 1) * LANES] = r[:, LANES:].astype(BF16)

    cos = cos_ref[rows, :]
    sin = sin_ref[rows, :]

    def norm_rope(t, gain, width):
        n = width // LANES
        bd = bd_ref[:width, :width]
        t2 = t * t
        hi = t2.astype(BF16)
        lo = (t2 - hi.astype(F32)).astype(BF16)
        ms = _dot(hi, bd) + _dot(lo, bd)
        tn = t * lax.rsqrt(ms + RMS_EPS) * gain
        lane = lax.broadcasted_iota(jnp.int32, tn.shape, 1)
        first_half = (lane % HEAD_DIM) < (HEAD_DIM // 2)
        rot = jnp.where(first_half, pltpu.roll(tn, width - HEAD_DIM // 2, 1), pltpu.roll(tn, HEAD_DIM // 2, 1))
        c = jnp.concatenate([cos] * n, axis=1)
        s = jnp.concatenate([sin] * n, axis=1)
        return tn * c + rot * s

    q = norm_rope(u[:, 512:1024], qg_ref[...], ATTN_WIDTH) * (HEAD_DIM ** -0.5)
    k = norm_rope(u[:, 1024:1280], kg_ref[...], 2 * KV_WIDTH)
    q_ref[rows, :] = q.astype(BF16)
    k_ref[rows, :] = k.astype(BF16)
    v_ref[rows, :] = u[:, 1280:1536].astype(BF16)


def _even_in(x2, seq, ln, w, dft, bd, qg, kg, cos_t, sin_t, tm=512):
    t = x2.shape[0]
    nseq = seq // tm
    row = lambda i: (i, 0)
    pos = lambda i: (i % nseq, 0)
    return pl.pallas_call(
        _even_in_kernel,
        grid=(t // tm,),
        in_specs=[pl.BlockSpec((tm, D_MODEL), row), _full((1, D_MODEL)), _full((D_MODEL, 1536)),
                  _full((LANES, 2 * LANES)), _full((512, 512)), _full((1, 512)), _full((1, 256)),
                  pl.BlockSpec((tm, LANES), pos), pl.BlockSpec((tm, LANES), pos)],
        out_specs=[pl.BlockSpec((tm, 1024), row), pl.BlockSpec((tm, 512), row),
                   pl.BlockSpec((tm, 256), row), pl.BlockSpec((tm, 256), row)],
        out_shape=[jax.ShapeDtypeStruct((t, 1024), BF16), jax.ShapeDtypeStruct((t, 512), BF16),
                   jax.ShapeDtypeStruct((t, 256), BF16), jax.ShapeDtypeStruct((t, 256), BF16)],
        compiler_params=_params(("parallel",)),
        name="even_in",
    )(x2, ln, w, dft, bd, qg, kg, cos_t, sin_t)


def _fft1_kernel(g_ref, c_ref, s_ref, y_ref):
    g = g_ref[0]
    p = _dot(c_ref[...], g)
    q = _dot(s_ref[...], g)
    for j in range(g.shape[1] // 1024):
        o = j * 1024
        y_ref[0, :, o:o + 512] = (p[:, o:o + 512] + q[:, o + 512:o + 1024]).astype(BF16)
        y_ref[0, :, o + 512:o + 1024] = (p[:, o + 512:o + 1024] - q[:, o:o + 512]).astype(BF16)


def _fft1(g3, c1, s1, cb=8192):
    b, n1, width = g3.shape
    blk = lambda i, j: (i, 0, j)
    return pl.pallas_call(
        _fft1_kernel,
        grid=(b, width // cb),
        in_specs=[pl.BlockSpec((1, n1, cb), blk), _full((n1, n1)), _full((n1, n1))],
        out_specs=pl.BlockSpec((1, n1, cb), blk),
        out_shape=jax.ShapeDtypeStruct(g3.shape, BF16),
        compiler_params=_params(("parallel", "parallel")),
        name="fft_stage1",
    )(g3, c1, s1)


def _fft2_kernel(y_ref, mc_ref, ms_ref, o_ref, *, scale):
    for j in range(y_ref.shape[1]):
        yr = y_ref[0, j, :, :512]
        yi = y_ref[0, j, :, 512:]
        r = _dot(mc_ref[j], yr) + _dot(ms_ref[j], yi)
        o_ref[0, :, j * 512:(j + 1) * 512] = (r * scale).astype(BF16)


def _fft2(y4, mc, ms, scale, kb=8):
    b, n1, n2, _ = y4.shape
    return pl.pallas_call(
        functools.partial(_fft2_kernel, scale=scale),
        grid=(n1 // kb, b),
        in_specs=[pl.BlockSpec((1, kb, n2, 1024), lambda k, i: (i, k, 0, 0)),
                  pl.BlockSpec((kb, n2, n2), lambda k, i: (k, 0, 0)),
                  pl.BlockSpec((kb, n2, n2), lambda k, i: (k, 0, 0))],
        out_specs=pl.BlockSpec((1, n2, kb * 512), lambda k, i: (i, 0, k)),
        out_shape=jax.ShapeDtypeStruct((b, n2, n1 * 512), BF16),
        compiler_params=_params(("parallel", "parallel")),
        name="fft_stage2",
    )(y4, mc, ms)


def _dft_tables(seq):
    n1, n2 = seq // FFT_N2, FFT_N2
    i1 = jnp.arange(n1, dtype=jnp.int32)
    ang1 = ((i1[:, None] * i1[None, :]) % n1).astype(F32) * (2.0 * math.pi / n1)
    c1, s1 = jnp.cos(ang1).astype(BF16), jnp.sin(ang1).astype(BF16)
    i2 = jnp.arange(n2, dtype=jnp.int32)
    ph = (i2[None, None, :] * i1[:, None, None] + n1 * i2[None, :, None] * i2[None, None, :]) % seq
    ang2 = ph.astype(F32) * (2.0 * math.pi / seq)
    mc, ms = jnp.cos(ang2).astype(BF16), jnp.sin(ang2).astype(BF16)
    ic = jnp.arange(FOURIER_GROUP, dtype=jnp.int32)
    angc = ((ic[:, None] * ic[None, :]) % FOURIER_GROUP).astype(F32) * (2.0 * math.pi / FOURIER_GROUP)
    dftc = jnp.concatenate([jnp.cos(angc), -jnp.sin(angc)], axis=1).astype(BF16)
    return c1, s1, mc, ms, dftc


def _attn_kernel(sink_ref, q_ref, kp_ref, ko_ref, kn_ref, vp_ref, vo_ref, vn_ref, o_ref, kc_ref, vc_ref):
    i = pl.program_id(1)
    last = pl.num_programs(1) - 1
    tq = q_ref.shape[0]
    nb = tq // ATTN_BLOCK
    kc_ref[0:ATTN_BLOCK] = kp_ref[...]
    kc_ref[ATTN_BLOCK:ATTN_BLOCK + tq] = ko_ref[...]
    kc_ref[ATTN_BLOCK + tq:] = kn_ref[...]
    vc_ref[0:ATTN_BLOCK] = vp_ref[...]
    vc_ref[ATTN_BLOCK:ATTN_BLOCK + tq] = vo_ref[...]
    vc_ref[ATTN_BLOCK + tq:] = vn_ref[...]

    w3 = 3 * ATTN_BLOCK
    qpos = lax.broadcasted_iota(jnp.int32, (ATTN_BLOCK, w3), 0) + ATTN_BLOCK
    kpos = lax.broadcasted_iota(jnp.int32, (ATTN_BLOCK, w3), 1)
    band = jnp.abs(kpos - qpos) <= ATTN_BLOCK
    lane = lax.broadcasted_iota(jnp.int32, (w3, LANES), 1)
    lo_half = lane < HEAD_DIM
    out_lane = lax.broadcasted_iota(jnp.int32, (ATTN_BLOCK, LANES), 1) < HEAD_DIM

    for jb in range(nb):
        ok = band
        if jb == 0:
            ok = ok & ((kpos >= ATTN_BLOCK) | (i > 0))
        if jb == nb - 1:
            ok = ok & ((kpos < 2 * ATTN_BLOCK) | (i < last))
        bias = jnp.where(ok, 0.0, NEG_BIG).astype(F32)
        qb = q_ref[jb * ATTN_BLOCK:(jb + 1) * ATTN_BLOCK, :]
        kw = kc_ref[jb * ATTN_BLOCK:jb * ATTN_BLOCK + w3, :]
        vw = vc_ref[jb * ATTN_BLOCK:jb * ATTN_BLOCK + w3, :]
        vbds, scores = [], []
        for g in range(N_KV_HEADS):
            kg = kw[:, g * LANES:(g + 1) * LANES]
            vg = vw[:, g * LANES:(g + 1) * LANES]
            zero = jnp.zeros_like(kg)
            kbd = jnp.concatenate([jnp.where(lo_half, kg, zero), jnp.where(lo_half, zero, kg)], axis=0)
            vbds.append(jnp.concatenate([jnp.where(lo_half, vg, zero), jnp.where(lo_half, zero, vg)], axis=0))
            for pr in range(2):
                col = (2 * g + pr) * LANES
                scores.append(_dot_nt(qb[:, col:col + LANES], kbd))
        probs, scales = [], []
        for pair, s in enumerate(scores):
            es, rs = [], []
            for half in range(2):
                sk = sink_ref[2 * pair + half]
                sh = s[:, half * w3:(half + 1) * w3] + bias
                m = jnp.maximum(jnp.max(sh, axis=-1, keepdims=True), sk)
                e = jnp.exp(sh - m)
                den = jnp.sum(e, axis=-1, keepdims=True) + jnp.exp(sk - m)
                es.append(e.astype(BF16))
                rs.append(1.0 / den)
            probs.append(jnp.concatenate(es, axis=1))
            scales.append(jnp.where(out_lane, rs[0], rs[1]))
        for pair in range(2 * N_KV_HEADS):
            o = _dot(probs[pair], vbds[pair // 2]) * scales[pair]
            o_ref[jb * ATTN_BLOCK:(jb + 1) * ATTN_BLOCK, pair * LANES:(pair + 1) * LANES] = o.astype(BF16)


def _attention(q, k, v, sink, bsz, seq, tq=512):
    t = q.shape[0]
    nq = seq // tq
    r = tq // ATTN_BLOCK
    nblk = t // ATTN_BLOCK
    own = lambda b, i, s: (b * nq + i, 0)
    prev = lambda b, i, s: (jnp.maximum((b * nq + i) * r - 1, 0), 0)
    nxt = lambda b, i, s: (jnp.minimum((b * nq + i + 1) * r, nblk - 1), 0)
    kv = lambda m: pl.BlockSpec((ATTN_BLOCK if m is not own else tq, 256), m)
    return pl.pallas_call(
        _attn_kernel,
        grid_spec=pltpu.PrefetchScalarGridSpec(
            num_scalar_prefetch=1,
            grid=(bsz, nq),
            in_specs=[pl.BlockSpec((tq, 512), own), kv(prev), kv(own), kv(nxt), kv(prev), kv(own), kv(nxt)],
            out_specs=pl.BlockSpec((tq, 512), own),
            scratch_shapes=[pltpu.VMEM((tq + 2 * ATTN_BLOCK, 256), BF16),
                            pltpu.VMEM((tq + 2 * ATTN_BLOCK, 256), BF16)]),
        out_shape=jax.ShapeDtypeStruct((t, 512), BF16),
        compiler_params=_params(("parallel", "parallel")),
        name="window_attention",
    )(sink, q, k, k, k, v, v, v)


def _even_out_kernel(x_ref, f_ref, o_ref, w_ref, out_ref):
    out_ref[...] = (x_ref[...] + _dot(f_ref[...], w_ref[:512, :]) + _dot(o_ref[...], w_ref[512:, :]))


def _even_out(x2, f, o, w, tm=512):
    t = x2.shape[0]
    row = lambda i: (i, 0)
    return pl.pallas_call(
        _even_out_kernel,
        grid=(t // tm,),
        in_specs=[pl.BlockSpec((tm, D_MODEL), row), pl.BlockSpec((tm, 512), row),
                  pl.BlockSpec((tm, 512), row), _full((1024, D_MODEL))],
        out_specs=pl.BlockSpec((tm, D_MODEL), row),
        out_shape=jax.ShapeDtypeStruct((t, D_MODEL), F32),
        compiler_params=_params(("parallel",)),
        name="even_out",
    )(x2, f, o, w)


def _even_mixer(x2, bsz, seq, ln, w_in, q_gain, k_gain, sink, w_out):
    c1, s1, mc, ms, dftc = _dft_tables(seq)
    n1 = seq // FFT_N2
    kcols = w_in[:, 1024:1152]
    vcols = w_in[:, 1152:1280]
    dup = lambda c: jnp.concatenate([c[:, :64], c[:, :64], c[:, 64:], c[:, 64:]], axis=1)
    w = jnp.concatenate([w_in[:, :1024], dup(kcols), dup(vcols)], axis=1).astype(BF16)
    hid = jnp.arange(512, dtype=jnp.int32) // HEAD_DIM
    bd = jnp.where(hid[:, None] == hid[None, :], 1.0 / HEAD_DIM, 0.0).astype(BF16)
    inv = 1.0 / (ROPE_THETA ** (jnp.arange(0, HEAD_DIM, 2, dtype=F32) / HEAD_DIM))
    ang = jnp.arange(seq, dtype=F32)[:, None] * inv[None, :]
    cos, sin = jnp.cos(ang), jnp.sin(ang)
    cos_t = jnp.concatenate([cos, cos, cos, cos], axis=1)
    sin_t = jnp.concatenate([-sin, sin, -sin, sin], axis=1)
    qg = jnp.tile(q_gain.astype(F32), N_Q_HEADS)[None, :]
    kg = jnp.tile(k_gain.astype(F32), 2 * N_KV_HEADS)[None, :]
    g, q, k, v = _even_in(x2, seq, ln[None, :], w, dftc, bd, qg, kg, cos_t, sin_t)
    y = _fft1(g.reshape(bsz, n1, FFT_N2 * 1024), c1, s1)
    scale = 1.0 / math.sqrt(seq * FOURIER_GROUP)
    f = _fft2(y.reshape(bsz, n1, FFT_N2, 1024), mc, ms, scale).reshape(bsz * seq, 512)
    o = _attention(q, k, v, sink.astype(F32), bsz, seq)
    return _even_out(x2, f, o, w_out.astype(BF16))


def _odd_in_kernel(x_ref, g_ref, w_ref, z_ref, xbc_ref, dt_ref):
    h = _rms_rows(x_ref[...], g_ref[...]).astype(BF16)
    z_ref[...] = _dot(h, w_ref[:, :D_INNER]).astype(BF16)
    xbc_ref[...] = _dot(h, w_ref[:, D_INNER:D_INNER + CONV_DIM]).astype(BF16)
    dt_ref[...] = _dot(h, w_ref[:, D_INNER + CONV_DIM:])


def _odd_in(x2, ln, w, tm=256):
    t = x2.shape[0]
    row = lambda i: (i, 0)
    wcols = D_INNER + CONV_DIM + LANES
    return pl.pallas_call(
        _odd_in_kernel,
        grid=(t // tm,),
        in_specs=[pl.BlockSpec((tm, D_MODEL), row), _full((1, D_MODEL)), _full((D_MODEL, wcols))],
        out_specs=[pl.BlockSpec((tm, D_INNER), row), pl.BlockSpec((tm, CONV_DIM), row),
                   pl.BlockSpec((tm, LANES), row)],
        out_shape=[jax.ShapeDtypeStruct((t, D_INNER), BF16), jax.ShapeDtypeStruct((t, CONV_DIM), BF16),
                   jax.ShapeDtypeStruct((t, LANES), F32)],
        compiler_params=_params(("parallel",)),
        name="odd_in",
    )(x2, ln, w)


CONV_HALO = 16


def _conv_kernel(prev_ref, main_ref, next_ref, w_ref, b_ref, o_ref, win_ref, *, tiles_per_seq):
    i = pl.program_id(0) % tiles_per_seq
    tc = main_ref.shape[0]
    keep_prev = (i > 0).astype(F32)
    keep_next = (i < tiles_per_seq - 1).astype(F32)
    win_ref[0:CONV_HALO, :] = prev_ref[...].astype(F32) * keep_prev
    win_ref[CONV_HALO:CONV_HALO + tc, :] = main_ref[...].astype(F32)
    win_ref[CONV_HALO + tc:, :] = next_ref[...].astype(F32) * keep_next
    cb = 512
    for c in range(CONV_DIM // cb):
        cols = slice(c * cb, (c + 1) * cb)
        win = win_ref[:, cols]
        acc = jnp.broadcast_to(b_ref[:, cols], (tc, cb))
        for k in range(CONV_K):
            shift = CONV_K // 2 - k
            rolled = win if shift == 0 else pltpu.roll(win, shift % win.shape[0], 0)
            acc = acc + rolled[CONV_HALO:CONV_HALO + tc, :] * w_ref[k:k + 1, cols]
        o_ref[:, cols] = _silu(acc).astype(BF16)


def _conv(xbc, seq, w, b, tc=512):
    t = xbc.shape[0]
    r = tc // CONV_HALO
    nh = t // CONV_HALO
    return pl.pallas_call(
        functools.partial(_conv_kernel, tiles_per_seq=seq // tc),
        grid=(t // tc,),
        in_specs=[pl.BlockSpec((CONV_HALO, CONV_DIM), lambda i: (jnp.maximum(i * r - 1, 0), 0)),
                  pl.BlockSpec((tc, CONV_DIM), lambda i: (i, 0)),
                  pl.BlockSpec((CONV_HALO, CONV_DIM), lambda i: (jnp.minimum((i + 1) * r, nh - 1), 0)),
                  _full((CONV_K, CONV_DIM)), _full((1, CONV_DIM))],
        out_specs=pl.BlockSpec((tc, CONV_DIM), lambda i: (i, 0)),
        out_shape=jax.ShapeDtypeStruct((t, CONV_DIM), BF16),
        scratch_shapes=[pltpu.VMEM((tc + 2 * CONV_HALO, CONV_DIM), F32)],
        compiler_params=_params(("parallel",)),
        name="ssd_conv",
    )(xbc, xbc, xbc, w, b)


N_PAIRS = SSM_HEADS // 2
PAIRS_PER_GROUP = HEADS_PER_GROUP // 2


LOG2E = math.log2(math.e)
SSD_STEP = 2 * CHUNK


def _ssd_chunk(xbc_ref, dt_ref, bias, a_row, expand, y_ref, state_ref, r0, reverse, mask, tri_col, lo_half):
    off = SSM_HEADS if reverse else 0
    rows = slice(r0, r0 + CHUNK)
    dt_lh = dt_ref[rows, :] + bias
    dt_lh = jnp.maximum(dt_lh, 0.0) + jnp.log(1.0 + jnp.exp(-jnp.abs(dt_lh)))
    acs_lh = _dot3_left(tri_col, dt_lh * a_row)
    edge = 0 if reverse else CHUNK - 1
    acs2_lh = acs_lh * LOG2E
    tot2_row = acs2_lh[edge:edge + 1, :]
    r_hl = acs2_lh.T
    dt_hi = dt_lh.astype(BF16)
    dt_mid = (dt_lh - dt_hi.astype(F32)).astype(BF16)
    dt_exp = _dot(dt_hi, expand) + _dot(dt_mid, expand)
    lo_row = lo_half[0:1, :]

    for g in range(SSM_GROUPS):
        b_g = xbc_ref[rows, D_INNER + g * SSM_STATE:D_INNER + (g + 1) * SSM_STATE]
        c_g = xbc_ref[rows, D_INNER + GN + g * SSM_STATE:D_INNER + GN + (g + 1) * SSM_STATE]
        cb_b = _dot_nt(c_g, b_g).astype(BF16)
        bt = b_g.astype(F32).T.astype(BF16)
        gcols = slice(g * PAIRS_PER_GROUP * LANES, (g + 1) * PAIRS_PER_GROUP * LANES)
        y_off = _dot(c_g, state_ref[:, gcols].astype(BF16))
        for pp in range(PAIRS_PER_GROUP):
            p = g * PAIRS_PER_GROUP + pp
            hd0, hd1 = off + 2 * p, off + 2 * p + 1
            colb = [jnp.broadcast_to(acs2_lh[:, hd:hd + 1], (CHUNK, CHUNK)) for hd in (hd0, hd1)]
            ms = [cb_b * jnp.exp2(jnp.where(mask, cb_l - r_hl[hd:hd + 1, :], NEG_BIG)).astype(BF16)
                  for cb_l, hd in zip(colb, (hd0, hd1))]
            pcols = slice(p * LANES, (p + 1) * LANES)
            xdt = xbc_ref[rows, pcols].astype(F32) * dt_exp[:, pcols]
            xdt_b = xdt.astype(BF16)
            zx = jnp.zeros_like(xdt_b)
            xbd = jnp.concatenate([jnp.where(lo_half, xdt_b, zx), jnp.where(lo_half, zx, xdt_b)], axis=0)
            colsel = jnp.where(lo_half, colb[0], colb[1])
            y = _dot(jnp.concatenate(ms, axis=1), xbd)
            y = y + jnp.exp2(colsel) * y_off[:, pp * LANES:(pp + 1) * LANES]
            y_ref[rows, pcols] = y.astype(BF16)
            tot2 = jnp.where(lo_row, tot2_row[:, hd0:hd0 + 1], tot2_row[:, hd1:hd1 + 1])
            snew = _dot(bt, (xdt * jnp.exp2(tot2 - colsel)).astype(BF16))
            state_ref[:, pcols] = state_ref[:, pcols] * jnp.exp2(tot2) + snew


def _ssd_kernel(xbc_ref, dt_ref, bias_ref, alog_ref, exp_ref, y_ref, state_ref, *, reverse):
    @pl.when(pl.program_id(1) == 0)
    def _():
        state_ref[...] = jnp.zeros_like(state_ref)

    r_io = lax.broadcasted_iota(jnp.int32, (CHUNK, CHUNK), 0)
    c_io = lax.broadcasted_iota(jnp.int32, (CHUNK, CHUNK), 1)
    mask = (r_io <= c_io) if reverse else (r_io >= c_io)
    tri_col = jnp.where(mask, 1.0, 0.0).astype(BF16)
    lo_half = lax.broadcasted_iota(jnp.int32, (CHUNK, LANES), 1) < SSM_HEAD_DIM
    bias = bias_ref[...]
    a_row = -jnp.exp(alog_ref[...])
    n_sub = SSD_STEP // CHUNK
    order = range(n_sub - 1, -1, -1) if reverse else range(n_sub)
    for sub in order:
        _ssd_chunk(xbc_ref, dt_ref, bias, a_row, exp_ref[...], y_ref, state_ref, sub * CHUNK, reverse,
                   mask, tri_col, lo_half)


def _ssd(xbc_c, dt_raw, bias_row, alog_row, expand, bsz, seq, reverse):
    t = xbc_c.shape[0]
    ns = seq // SSD_STEP
    if reverse:
        blk = lambda b, c: (b * ns + ns - 1 - c, 0)
    else:
        blk = lambda b, c: (b * ns + c, 0)
    return pl.pallas_call(
        functools.partial(_ssd_kernel, reverse=reverse),
        grid=(bsz, ns),
        in_specs=[pl.BlockSpec((SSD_STEP, CONV_DIM), blk), pl.BlockSpec((SSD_STEP, LANES), blk),
                  _full((1, LANES)), _full((1, LANES)), _full((LANES, D_INNER))],
        out_specs=pl.BlockSpec((SSD_STEP, D_INNER), blk),
        out_shape=jax.ShapeDtypeStruct((t, D_INNER), BF16),
        scratch_shapes=[pltpu.VMEM((SSM_STATE, D_INNER), F32)],
        compiler_params=_params(("parallel", "arbitrary")),
        name="ssd_scan_bwd" if reverse else "ssd_scan_fwd",
    )(xbc_c, dt_raw, bias_row, alog_row, expand)


ODD_OUT_SUB = 512


def _odd_out_kernel(x_ref, yf_ref, yb_ref, xs_ref, z_ref, d_ref, g_ref, w_ref, out_ref):
    gw = D_INNER // SSM_GROUPS
    sub = ODD_OUT_SUB
    for j in range(x_ref.shape[0] // sub):
        rows = slice(j * sub, (j + 1) * sub)
        y = yf_ref[rows, :].astype(F32) + yb_ref[rows, :].astype(F32) + d_ref[...] * xs_ref[rows, :].astype(F32)
        y = y * _silu(z_ref[rows, :].astype(F32))
        parts = []
        for g in range(SSM_GROUPS):
            yg = y[:, g * gw:(g + 1) * gw]
            parts.append(yg * lax.rsqrt(jnp.mean(yg * yg, axis=-1, keepdims=True) + RMS_EPS))
        yn = (jnp.concatenate(parts, axis=1) * g_ref[...]).astype(BF16)
        out_ref[rows, :] = x_ref[rows, :] + _dot(yn, w_ref[...])


def _odd_out(x2, yf, yb, xbc_c, z, dskip, gain, w, tm=512):
    t = x2.shape[0]
    row = lambda i: (i, 0)
    return pl.pallas_call(
        _odd_out_kernel,
        grid=(t // tm,),
        in_specs=[pl.BlockSpec((tm, D_MODEL), row), pl.BlockSpec((tm, D_INNER), row),
                  pl.BlockSpec((tm, D_INNER), row), pl.BlockSpec((tm, D_INNER), row),
                  pl.BlockSpec((tm, D_INNER), row), _full((1, D_INNER)), _full((1, D_INNER)),
                  _full((D_INNER, D_MODEL))],
        out_specs=pl.BlockSpec((tm, D_MODEL), row),
        out_shape=jax.ShapeDtypeStruct((t, D_MODEL), F32),
        compiler_params=_params(("parallel",)),
        name="odd_out",
    )(x2, yf, yb, xbc_c, z, dskip, gain, w)


def _odd_mixer(x2, bsz, seq, ln, w_in, conv_w, conv_b, dt_bias, a_log, d_skip, norm_g, w_out):
    pad = jnp.zeros((D_MODEL, LANES - 2 * SSM_HEADS), w_in.dtype)
    w = jnp.concatenate([w_in, pad], axis=1).astype(BF16)
    z, xbc, dt_raw = _odd_in(x2, ln[None, :], w)
    xbc_c = _conv(xbc, seq, conv_w.astype(F32), conv_b.astype(F32)[None, :])
    lane_pad = jnp.zeros((LANES - 2 * SSM_HEADS,), F32)
    bias_row = jnp.concatenate([dt_bias.astype(F32).reshape(-1), lane_pad])[None, :]
    alog_row = jnp.concatenate([a_log.astype(F32).reshape(-1), lane_pad])[None, :]
    ch = jnp.arange(D_INNER, dtype=jnp.int32) // SSM_HEAD_DIM
    hd = jnp.arange(LANES, dtype=jnp.int32)
    ys = []
    for reverse in (False, True):
        off = SSM_HEADS if reverse else 0
        expand = jnp.where(hd[:, None] == off + ch[None, :], 1.0, 0.0).astype(BF16)
        ys.append(_ssd(xbc_c, dt_raw, bias_row, alog_row, expand, bsz, seq, reverse))
    dskip = jnp.repeat(d_skip.astype(F32), SSM_HEAD_DIM)[None, :]
    return _odd_out(x2, ys[0], ys[1], xbc_c, z, dskip, norm_g.astype(F32)[None, :], w_out.astype(BF16))


ROUTE_LANES = N_EXPERT_GROUPS + N_EXPERTS


def _router_kernel(x_ref, g_ref, w_ref, b_ref, ls_ref, meta_ref, cnt_ref, run_ref):
    i = pl.program_id(0)

    @pl.when(i == 0)
    def _():
        run_ref[...] = jnp.zeros_like(run_ref)

    h = _rms_rows(x_ref[...], g_ref[...])
    hh = h.astype(BF16)
    hl = (h - hh.astype(F32)).astype(BF16)
    w_hi = w_ref[0]
    w_lo = w_ref[1]
    logits = _dot(hh, w_hi) + _dot(hl, w_hi) + _dot(hh, w_lo) + b_ref[...]
    tm = logits.shape[0]
    lane = lax.broadcasted_iota(jnp.int32, (tm, LANES), 1).astype(F32)

    def top1(v):
        m = jnp.max(v, axis=-1, keepdims=True)
        idx = jnp.min(jnp.where(v == m, lane, float(LANES)), axis=-1, keepdims=True)
        return m, idx

    gl = jnp.where(lane < N_EXPERT_GROUPS, logits, NEG_BIG)
    gmax, gidx = top1(gl)
    g_p = 1.0 / jnp.sum(jnp.exp(gl - gmax), axis=-1, keepdims=True)
    lo = N_EXPERT_GROUPS + EXPERTS_PER_GROUP * gidx
    e1 = jnp.where((lane >= lo) & (lane < lo + EXPERTS_PER_GROUP), logits, NEG_BIG)
    m1, i1 = top1(e1)
    e2 = jnp.where(lane == i1, NEG_BIG, e1)
    m2, i2 = top1(e2)
    r = jnp.exp(m2 - m1)
    w_a = g_p / (1.0 + r)
    w_b = g_p * r / (1.0 + r)
    e_a = i1 - N_EXPERT_GROUPS
    e_b = i2 - N_EXPERT_GROUPS

    onehot = jnp.where((lane == e_a) | (lane == e_b), 1.0, 0.0)
    before = _dot(ls_ref[...], onehot.astype(BF16)) + run_ref[...]
    rank_a = jnp.sum(jnp.where(lane == e_a, before, 0.0), axis=-1, keepdims=True)
    rank_b = jnp.sum(jnp.where(lane == e_b, before, 0.0), axis=-1, keepdims=True)
    run_ref[...] = run_ref[...] + jnp.sum(onehot, axis=0, keepdims=True)
    cnt_ref[...] = jnp.broadcast_to(run_ref[...], cnt_ref.shape)

    meta = jnp.where(lane == 0, e_a, 0.0)
    meta = jnp.where(lane == 1, e_b, meta)
    meta = jnp.where(lane == 2, rank_a, meta)
    meta = jnp.where(lane == 3, rank_b, meta)
    meta = jnp.where(lane == 4, w_a, meta)
    meta = jnp.where(lane == 5, w_b, meta)
    meta_ref[...] = meta


def _router(x2, ln, w2, b, lstrict, tm):
    t = x2.shape[0]
    row = lambda i: (i, 0)
    return pl.pallas_call(
        _router_kernel,
        grid=(t // tm,),
        in_specs=[pl.BlockSpec((tm, D_MODEL), row), _full((1, D_MODEL)), _full((2, D_MODEL, LANES)),
                  _full((1, LANES)), _full((tm, tm))],
        out_specs=[pl.BlockSpec((tm, LANES), row), _full((8, LANES))],
        out_shape=[jax.ShapeDtypeStruct((t, LANES), F32), jax.ShapeDtypeStruct((8, LANES), F32)],
        scratch_shapes=[pltpu.VMEM((1, LANES), F32)],
        compiler_params=_params(("arbitrary",)),
        name="moe_router",
    )(x2, ln, w2, b, lstrict)


SUBLANES = 8
PACKED = D_MODEL // 2
DMA_THREADS = 2
U32 = jnp.uint32


def _pack_rows(v):
    bits = lax.bitcast_convert_type(v.astype(BF16).astype(F32), U32)
    return (bits[:, :PACKED] >> 16) | bits[:, PACKED:]


def _unpack_rows(w):
    lo = lax.bitcast_convert_type(w << 16, F32)
    hi = lax.bitcast_convert_type(w & jnp.uint32(0xFFFF0000), F32)
    return jnp.concatenate([lo, hi], axis=1)


def _dispatch_kernel(pad_end_ref, nused_ref, dest_ref, x_ref, g_ref, xin_hbm, h_ref, zero_ref, sem, zsem,
                     *, tm, nb):
    def zero_block(first_row):
        return pltpu.make_async_copy(zero_ref, xin_hbm.at[pl.ds(pl.multiple_of(first_row, MOE_BLOCK), MOE_BLOCK), :],
                                     zsem)

    def seg_zero(action):
        for e in range(N_EXPERTS):
            end = pad_end_ref[e]
            prev = pad_end_ref[e - 1] if e else 0

            @pl.when(end > prev)
            def _():
                action(zero_block(end - MOE_BLOCK))

    def tail_zero(action):
        def body(b, carry):
            action(zero_block(b * MOE_BLOCK))
            return carry

        lax.fori_loop(nused_ref[0], nb, body, 0)

    @pl.when(pl.program_id(0) == 0)
    def _():
        zero_ref[...] = jnp.zeros_like(zero_ref)
        seg_zero(lambda c: c.start())
        tail_zero(lambda c: c.start())
        seg_zero(lambda c: c.wait())
        tail_zero(lambda c: c.wait())

    def row_copy(grp, c, dst_row):
        return pltpu.make_async_copy(h_ref.at[grp, pl.ds(c, 1), :], xin_hbm.at[pl.ds(dst_row, 1), :], sem)

    def issue(grp, carry):
        for c in range(SUBLANES):
            for k in range(2):
                row_copy(grp, c, dest_ref[0, 0, 2 * SUBLANES * grp + 2 * c + k]).start(
                    priority=(2 * c + k) % DMA_THREADS)
        return carry

    sub = DISPATCH_SUB
    sub_groups = sub // SUBLANES
    for j in range(tm // sub):
        packed = _pack_rows(_rms_rows(x_ref[j * sub:(j + 1) * sub, :], g_ref[...]))
        h_ref[j * sub_groups:(j + 1) * sub_groups] = packed.reshape(sub_groups, SUBLANES, PACKED)
        lax.fori_loop(j * sub_groups, (j + 1) * sub_groups, issue, 0)

    def drain(grp, carry):
        for _ in range(2 * SUBLANES):
            row_copy(0, 0, 0).wait()
        return carry

    lax.fori_loop(0, tm // SUBLANES, drain, 0)


def _dispatch(pad_end, nused, dest, x2, ln, nb, tm):
    t = x2.shape[0]
    return pl.pallas_call(
        functools.partial(_dispatch_kernel, tm=tm, nb=nb),
        grid_spec=pltpu.PrefetchScalarGridSpec(
            num_scalar_prefetch=2,
            grid=(t // tm,),
            in_specs=[pl.BlockSpec((1, 1, 2 * tm), lambda i, pe, nu: (i, 0, 0), memory_space=pltpu.SMEM),
                      pl.BlockSpec((tm, D_MODEL), lambda i, pe, nu: (i, 0)),
                      pl.BlockSpec((1, D_MODEL), lambda i, pe, nu: (0, 0))],
            out_specs=pl.BlockSpec(memory_space=pl.ANY),
            scratch_shapes=[pltpu.VMEM((tm // SUBLANES, SUBLANES, PACKED), U32),
                            pltpu.VMEM((MOE_BLOCK, PACKED), U32),
                            pltpu.SemaphoreType.DMA, pltpu.SemaphoreType.DMA]),
        out_shape=jax.ShapeDtypeStruct((nb * MOE_BLOCK, PACKED), U32),
        compiler_params=_params(("arbitrary",)),
        name="moe_dispatch",
    )(pad_end, nused, dest, x2, ln)


EXPERT_BLOCKS_PER_STEP = 2
DISPATCH_SUB = 128


def _experts_kernel(be_ref, nused_ref, x_ref, w1a_ref, w3a_ref, w2a_ref, w1b_ref, w3b_ref, w2b_ref, o_ref):
    i = pl.program_id(0)
    first = EXPERT_BLOCKS_PER_STEP * i

    def swiglu(rows, w1_ref, w3_ref, w2_ref):
        x = _unpack_rows(x_ref[rows, :]).astype(BF16)
        a = _silu(_dot(x, w1_ref[0])) * _dot(x, w3_ref[0])
        o_ref[rows, :] = _pack_rows(_dot(a.astype(BF16), w2_ref[0]))

    fused = (first + 1 < nused_ref[0]) & (be_ref[first] == be_ref[first + 1])

    @pl.when(fused)
    def _():
        swiglu(slice(0, EXPERT_BLOCKS_PER_STEP * MOE_BLOCK), w1a_ref, w3a_ref, w2a_ref)

    for half, wrefs in enumerate(((w1a_ref, w3a_ref, w2a_ref), (w1b_ref, w3b_ref, w2b_ref))):
        rows = slice(half * MOE_BLOCK, (half + 1) * MOE_BLOCK)
        live = first + half < nused_ref[0]

        @pl.when(jnp.logical_not(fused) & live)
        def _():
            swiglu(rows, *wrefs)

        @pl.when(jnp.logical_not(live))
        def _():
            o_ref[rows, :] = jnp.zeros((MOE_BLOCK, PACKED), U32)


def _experts(block_e, nused, xin, w1, w3, w2):
    n = EXPERT_BLOCKS_PER_STEP
    nb = xin.shape[0] // MOE_BLOCK
    assert nb % n == 0
    row = lambda i, be, nu: (i, 0)
    specs = []
    for half in range(n):
        wsel = functools.partial(lambda i, be, nu, half: (be[n * i + half], 0, 0), half=half)
        specs += [pl.BlockSpec((1, D_MODEL, D_EXPERT), wsel), pl.BlockSpec((1, D_MODEL, D_EXPERT), wsel),
                  pl.BlockSpec((1, D_EXPERT, D_MODEL), wsel)]
    return pl.pallas_call(
        _experts_kernel,
        grid_spec=pltpu.PrefetchScalarGridSpec(
            num_scalar_prefetch=2,
            grid=(nb // n,),
            in_specs=[pl.BlockSpec((n * MOE_BLOCK, PACKED), row)] + specs,
            out_specs=pl.BlockSpec((n * MOE_BLOCK, PACKED), row)),
        out_shape=jax.ShapeDtypeStruct(xin.shape, U32),
        compiler_params=_params(("parallel",)),
        name="moe_experts",
    )(block_e, nused, xin, *([w1, w3, w2] * n))


def _combine_ple_kernel(dest_ref, dest_next_ref, x_ref, meta_ref, p_ref, g_ref, wg_ref, wp_ref, y_hbm,
                        out_ref, ya_ref, yb_ref, sem, *, tm):
    i = pl.program_id(0)
    slot = i % 2

    def row_gather(src_row, buf, s, grp, c):
        return pltpu.make_async_copy(y_hbm.at[pl.ds(src_row, 1), :], buf.at[s, grp, pl.ds(c, 1), :], sem.at[s])

    sub = COMBINE_SUB
    n_sub = tm // sub
    sub_groups = sub // SUBLANES

    def gather_part(idx_ref, s, part):
        def issue(grp, carry):
            for c in range(SUBLANES):
                for k, buf in enumerate((ya_ref, yb_ref)):
                    row_gather(idx_ref[0, 0, 2 * SUBLANES * grp + 2 * c + k], buf, s, grp, c).start(
                        priority=(2 * c + k) % DMA_THREADS)
            return carry

        lax.fori_loop(part * sub_groups, (part + 1) * sub_groups, issue, 0)

    @pl.when(i == 0)
    def _():
        for part in range(n_sub):
            gather_part(dest_ref, 0, part)

    def drain(grp, carry):
        for _ in range(SUBLANES):
            for buf in (ya_ref, yb_ref):
                row_gather(0, buf, slot, 0, 0).wait()
        return carry

    lax.fori_loop(0, tm // SUBLANES, drain, 0)

    for j in range(n_sub):
        @pl.when(i + 1 < pl.num_programs(0))
        def _():
            gather_part(dest_next_ref, 1 - slot, j)

        rows = slice(j * sub, (j + 1) * sub)
        grps = slice(j * sub // SUBLANES, (j + 1) * sub // SUBLANES)
        meta = meta_ref[rows, :]
        ya = _unpack_rows(ya_ref[slot, grps].reshape(sub, PACKED))
        yb = _unpack_rows(yb_ref[slot, grps].reshape(sub, PACKED))
        x = x_ref[rows, :] + (meta[:, 4:5] * ya + meta[:, 5:6] * yb)
        hn = _rms_rows(x, g_ref[...]).astype(BF16)
        gate = 1.0 / (1.0 + jnp.exp(-_dot(hn, wg_ref[...])))
        out_ref[rows, :] = x + _dot(p_ref[rows, :].astype(BF16), wp_ref[...]) * gate


def _combine_ple(dest, x2, meta, p2, ln, wg, wp, yout, tm):
    t = x2.shape[0]
    nt = t // tm
    row = lambda i: (i, 0)
    full = lambda shape: pl.BlockSpec(shape, lambda i: (0,) * len(shape))
    return pl.pallas_call(
        functools.partial(_combine_ple_kernel, tm=tm),
        grid_spec=pltpu.PrefetchScalarGridSpec(
            num_scalar_prefetch=0,
            grid=(nt,),
            in_specs=[pl.BlockSpec((1, 1, 2 * tm), lambda i: (i, 0, 0), memory_space=pltpu.SMEM),
                      pl.BlockSpec((1, 1, 2 * tm), lambda i: (jnp.minimum(i + 1, nt - 1), 0, 0),
                                   memory_space=pltpu.SMEM),
                      pl.BlockSpec((tm, D_MODEL), row), pl.BlockSpec((tm, LANES), row),
                      pl.BlockSpec((tm, PLE_DIM), row), full((1, D_MODEL)), full((D_MODEL, D_MODEL)),
                      full((PLE_DIM, D_MODEL)), pl.BlockSpec(memory_space=pl.ANY)],
            out_specs=pl.BlockSpec((tm, D_MODEL), row),
            scratch_shapes=[pltpu.VMEM((2, tm // SUBLANES, SUBLANES, PACKED), U32),
                            pltpu.VMEM((2, tm // SUBLANES, SUBLANES, PACKED), U32),
                            pltpu.SemaphoreType.DMA((2,))]),
        out_shape=jax.ShapeDtypeStruct((t, D_MODEL), F32),
        compiler_params=_params(("arbitrary",)),
        name="moe_combine_ple",
    )(dest, dest, x2, meta, p2, ln, wg, wp, yout)


MOE_TM = 512
COMBINE_SUB = 256


def _moe_ple(x2, p2, ln_ffn, w_rg, b_rg, w_re, b_re, w1, w3, w2, ln_ple, w_gate, w_proj):
    t = x2.shape[0]
    tm = MOE_TM
    wr = jnp.concatenate([w_rg, w_re, jnp.zeros((D_MODEL, LANES - ROUTE_LANES), w_rg.dtype)], axis=1).astype(F32)
    wr_hi = wr.astype(BF16)
    wr_lo = (wr - wr_hi.astype(F32)).astype(BF16)
    br = jnp.concatenate([b_rg, b_re, jnp.zeros((LANES - ROUTE_LANES,), b_rg.dtype)]).astype(F32)[None, :]
    ii = jnp.arange(tm, dtype=jnp.int32)
    lstrict = jnp.where(ii[:, None] > ii[None, :], 1.0, 0.0).astype(BF16)
    meta, cnt = _router(x2, ln_ffn[None, :], jnp.stack([wr_hi, wr_lo]), br, lstrict, tm)

    counts = cnt[0, :N_EXPERTS].astype(jnp.int32)
    padded = (counts + MOE_BLOCK - 1) // MOE_BLOCK * MOE_BLOCK
    pad_end = jnp.cumsum(padded)
    start = (pad_end - padded).astype(jnp.int32)
    nb = (t * 2) // MOE_BLOCK + N_EXPERTS
    first_row = jnp.arange(nb, dtype=jnp.int32) * MOE_BLOCK
    block_e = jnp.minimum(jnp.sum((pad_end[None, :] <= first_row[:, None]).astype(jnp.int32), axis=1),
                          N_EXPERTS - 1).astype(jnp.int32)
    nused = (pad_end[-1:] // MOE_BLOCK).astype(jnp.int32)
    er = meta[:, :4].astype(jnp.int32)
    eid = jnp.arange(N_EXPERTS, dtype=jnp.int32)
    seg_start = jnp.sum(jnp.where(er[:, :2, None] == eid, start, 0), axis=-1)
    dest = (seg_start + er[:, 2:4]).reshape(t // tm, 1, 2 * tm)

    xin = _dispatch(pad_end.astype(jnp.int32), nused, dest, x2, ln_ffn[None, :], nb, tm)
    yout = _experts(block_e, nused, xin, w1.astype(BF16), w3.astype(BF16), w2.astype(BF16))
    return _combine_ple(dest, x2, meta, p2, ln_ple[None, :], w_gate.astype(BF16), w_proj.astype(BF16), yout, tm)


def _trunk(x, p, prm):
    bsz, seq, _ = x.shape
    x2 = x.reshape(bsz * seq, D_MODEL)
    for i in range(DEPTH):
        j = i // 2
        if i % 2 == 0:
            x2 = _even_mixer(x2, bsz, seq, prm["ln_mix_e"][j], prm["w_in_e"][j], prm["q_gain"][j],
                             prm["k_gain"][j], prm["sink"][j], prm["w_out_e"][j])
        else:
            x2 = _odd_mixer(x2, bsz, seq, prm["ln_mix_o"][j], prm["w_in_o"][j], prm["conv_w"][j],
                            prm["conv_b"][j], prm["dt_bias"][j], prm["a_log"][j], prm["d_skip"][j],
                            prm["ssm_gain"][j], prm["w_out_o"][j])
        x2 = _moe_ple(x2, p[i].reshape(bsz * seq, PLE_DIM), prm["ln_ffn"][i], prm["w_router_g"][i],
                      prm["b_router_g"][i], prm["w_router_e"][i], prm["b_router_e"][i], prm["w1"][i],
                      prm["w3"][i], prm["w2"][i], prm["ln_ple"][i], prm["w_ple_gate"][i], prm["w_ple_proj"][i])
    return x2.reshape(bsz, seq, D_MODEL)


def kernel(x_prompt, x_sample, p_prompt, p_sample, ln_mix_e, w_in_e, q_gain, k_gain, sink, w_out_e, ln_mix_o, w_in_o, conv_w, conv_b, dt_bias, a_log, d_skip, ssm_gain, w_out_o, ln_ffn, w_router_g, b_router_g, w_router_e, b_router_e, w1, w3, w2, ln_ple, w_ple_gate, w_ple_proj):
    prm = dict(ln_mix_e=ln_mix_e, w_in_e=w_in_e, q_gain=q_gain, k_gain=k_gain, sink=sink,
               w_out_e=w_out_e, ln_mix_o=ln_mix_o, w_in_o=w_in_o, conv_w=conv_w, conv_b=conv_b,
               dt_bias=dt_bias, a_log=a_log, d_skip=d_skip, ssm_gain=ssm_gain, w_out_o=w_out_o,
               ln_ffn=ln_ffn, w_router_g=w_router_g, b_router_g=b_router_g,
               w_router_e=w_router_e, b_router_e=b_router_e, w1=w1, w3=w3, w2=w2,
               ln_ple=ln_ple, w_ple_gate=w_ple_gate, w_ple_proj=w_ple_proj)
    return (_trunk(x_prompt, p_prompt, prm), _trunk(x_sample, p_sample, prm))
```

```python
import functools
import math

import jax
import jax.numpy as jnp
from jax import lax
from jax.experimental import pallas as pl
from jax.experimental.pallas import tpu as pltpu

D_MODEL = 1024
DEPTH = 4
RMS_EPS = 1e-6
N_FOURIER_GROUPS = 4
FOURIER_WIDTH = 512
FOURIER_GROUP = 128
FFT_N2 = 128
HEAD_DIM = 64
N_Q_HEADS = 8
N_KV_HEADS = 2
ATTN_WIDTH = 512
KV_WIDTH = 128
ATTN_BLOCK = 128
ROPE_THETA = 10000.0
D_INNER = 2048
SSM_HEAD_DIM = 64
SSM_HEADS = 32
SSM_STATE = 128
SSM_GROUPS = 4
HEADS_PER_GROUP = 8
CONV_K = 5
CHUNK = 128
GN = SSM_GROUPS * SSM_STATE
CONV_DIM = D_INNER + 2 * GN
N_EXPERT_GROUPS = 4
EXPERTS_PER_GROUP = 8
N_EXPERTS = 32
D_EXPERT = 512
MOE_BLOCK = 256
PLE_DIM = 256

LANES = 128
NEG_BIG = -1e30
VMEM_LIMIT = 52 * 1024 * 1024

F32 = jnp.float32
BF16 = jnp.bfloat16


def _params(sem, vmem=VMEM_LIMIT):
    return pltpu.CompilerParams(dimension_semantics=sem, vmem_limit_bytes=vmem)


def _dot(a, b):
    return jnp.dot(a, b, preferred_element_type=F32)


def _dot_nt(a, b):
    return lax.dot_general(a, b, (((1,), (1,)), ((), ())), preferred_element_type=F32)


def _split3(v):
    hi = v.astype(BF16)
    r1 = v - hi.astype(F32)
    mid = r1.astype(BF16)
    lo = (r1 - mid.astype(F32)).astype(BF16)
    return hi, mid, lo


def _dot3(v, m_bf16):
    hi, mid, lo = _split3(v)
    return _dot(hi, m_bf16) + _dot(mid, m_bf16) + _dot(lo, m_bf16)


def _dot3_left(m_bf16, v):
    hi, mid, lo = _split3(v)
    return _dot(m_bf16, hi) + _dot(m_bf16, mid) + _dot(m_bf16, lo)


def _rms_rows(x, g):
    return x * lax.rsqrt(jnp.mean(x * x, axis=-1, keepdims=True) + RMS_EPS) * g


def _silu(x):
    return x * (1.0 / (1.0 + jnp.exp(-x)))


def _full(shape):
    return pl.BlockSpec(shape, lambda *_: (0,) * len(shape))


EVEN_IN_SUB = 256


def _even_in_kernel(x_ref, g_ref, w_ref, dft_ref, bd_ref, qg_ref, kg_ref, cos_ref, sin_ref,
                    gout_ref, q_ref, k_ref, v_ref):
    for j in range(x_ref.shape[0] // EVEN_IN_SUB):
        _even_in_rows(slice(j * EVEN_IN_SUB, (j + 1) * EVEN_IN_SUB), x_ref, g_ref, w_ref, dft_ref, bd_ref,
                      qg_ref, kg_ref, cos_ref, sin_ref, gout_ref, q_ref, k_ref, v_ref)


def _even_in_rows(rows, x_ref, g_ref, w_ref, dft_ref, bd_ref, qg_ref, kg_ref, cos_ref, sin_ref,
                  gout_ref, q_ref, k_ref, v_ref):
    h = _rms_rows(x_ref[rows, :], g_ref[...]).astype(BF16)
    u = _dot(h, w_ref[...])
    a = u[:, :FOURIER_WIDTH].astype(BF16)
    for g in range(N_FOURIER_GROUPS):
        r = _dot(a[:, g * LANES:(g + 1) * LANES], dft_ref[...])
        gout_ref[rows, g * LANES:(g + 1) * LANES] = r[:, :LANES].astype(BF16)
        gout_ref[rows, FOURIER_WIDTH + g * LANES:FOURIER_WIDTH + (g + 1) * LANES] = r[:, LANES:].astype(BF16)

    cos = cos_ref[rows, :]
    sin = sin_ref[rows, :]

    def norm_rope(t, gain, width):
        n = width // LANES
        bd = bd_ref[:width, :width]
        t2 = t * t
        hi = t2.astype(BF16)
        lo = (t2 - hi.astype(F32)).astype(BF16)
        ms = _dot(hi, bd) + _dot(lo, bd)
        tn = t * lax.rsqrt(ms + RMS_EPS) * gain
        lane = lax.broadcasted_iota(jnp.int32, tn.shape, 1)
        first_half = (lane % HEAD_DIM) < (HEAD_DIM // 2)
        rot = jnp.where(first_half, pltpu.roll(tn, width - HEAD_DIM // 2, 1), pltpu.roll(tn, HEAD_DIM // 2, 1))
        c = jnp.concatenate([cos] * n, axis=1)
        s = jnp.concatenate([sin] * n, axis=1)
        return tn * c + rot * s

    q = norm_rope(u[:, 512:1024], qg_ref[...], ATTN_WIDTH) * (HEAD_DIM ** -0.5)
    k = norm_rope(u[:, 1024:1280], kg_ref[...], 2 * KV_WIDTH)
    q_ref[rows, :] = q.astype(BF16)
    k_ref[rows, :] = k.astype(BF16)
    v_ref[rows, :] = u[:, 1280:1536].astype(BF16)


def _even_in(x2, seq, ln, w, dft, bd, qg, kg, cos_t, sin_t, tm=512):
    t = x2.shape[0]
    nseq = seq // tm
    row = lambda i: (i, 0)
    pos = lambda i: (i % nseq, 0)
    return pl.pallas_call(
        _even_in_kernel,
        grid=(t // tm,),
        in_specs=[pl.BlockSpec((tm, D_MODEL), row), _full((1, D_MODEL)), _full((D_MODEL, 1536)),
                  _full((LANES, 2 * LANES)), _full((512, 512)), _full((1, 512)), _full((1, 256)),
                  pl.BlockSpec((tm, LANES), pos), pl.BlockSpec((tm, LANES), pos)],
        out_specs=[pl.BlockSpec((tm, 1024), row), pl.BlockSpec((tm, 512), row),
                   pl.BlockSpec((tm, 256), row), pl.BlockSpec((tm, 256), row)],
        out_shape=[jax.ShapeDtypeStruct((t, 1024), BF16), jax.ShapeDtypeStruct((t, 512), BF16),
                   jax.ShapeDtypeStruct((t, 256), BF16), jax.ShapeDtypeStruct((t, 256), BF16)],
        compiler_params=_params(("parallel",)),
        name="even_in",
    )(x2, ln, w, dft, bd, qg, kg, cos_t, sin_t)


def _fft1_kernel(g_ref, c_ref, s_ref, y_ref):
    g = g_ref[0]
    p = _dot(c_ref[...], g)
    q = _dot(s_ref[...], g)
    for j in range(g.shape[1] // 1024):
        o = j * 1024
        y_ref[0, :, o:o + 512] = (p[:, o:o + 512] + q[:, o + 512:o + 1024]).astype(BF16)
        y_ref[0, :, o + 512:o + 1024] = (p[:, o + 512:o + 1024] - q[:, o:o + 512]).astype(BF16)


def _fft1(g3, c1, s1, cb=8192):
    b, n1, width = g3.shape
    blk = lambda i, j: (i, 0, j)
    return pl.pallas_call(
        _fft1_kernel,
        grid=(b, width // cb),
        in_specs=[pl.BlockSpec((1, n1, cb), blk), _full((n1, n1)), _full((n1, n1))],
        out_specs=pl.BlockSpec((1, n1, cb), blk),
        out_shape=jax.ShapeDtypeStruct(g3.shape, BF16),
        compiler_params=_params(("parallel", "parallel")),
        name="fft_stage1",
    )(g3, c1, s1)


def _fft2_kernel(y_ref, mc_ref, ms_ref, o_ref, *, scale):
    for j in range(y_ref.shape[1]):
        yr = y_ref[0, j, :, :512]
        yi = y_ref[0, j, :, 512:]
        r = _dot(mc_ref[j], yr) + _dot(ms_ref[j], yi)
        o_ref[0, :, j * 512:(j + 1) * 512] = (r * scale).astype(BF16)


def _fft2(y4, mc, ms, scale, kb=8):
    b, n1, n2, _ = y4.shape
    return pl.pallas_call(
        functools.partial(_fft2_kernel, scale=scale),
        grid=(n1 // kb, b),
        in_specs=[pl.BlockSpec((1, kb, n2, 1024), lambda k, i: (i, k, 0, 0)),
                  pl.BlockSpec((kb, n2, n2), lambda k, i: (k, 0, 0)),
                  pl.BlockSpec((kb, n2, n2), lambda k, i: (k, 0, 0))],
        out_specs=pl.BlockSpec((1, n2, kb * 512), lambda k, i: (i, 0, k)),
        out_shape=jax.ShapeDtypeStruct((b, n2, n1 * 512), BF16),
        compiler_params=_params(("parallel", "parallel")),
        name="fft_stage2",
    )(y4, mc, ms)


def _dft_tables(seq):
    n1, n2 = seq // FFT_N2, FFT_N2
    i1 = jnp.arange(n1, dtype=jnp.int32)
    ang1 = ((i1[:, None] * i1[None, :]) % n1).astype(F32) * (2.0 * math.pi / n1)
    c1, s1 = jnp.cos(ang1).astype(BF16), jnp.sin(ang1).astype(BF16)
    i2 = jnp.arange(n2, dtype=jnp.int32)
    ph = (i2[None, None, :] * i1[:, None, None] + n1 * i2[None, :, None] * i2[None, None, :]) % seq
    ang2 = ph.astype(F32) * (2.0 * math.pi / seq)
    mc, ms = jnp.cos(ang2).astype(BF16), jnp.sin(ang2).astype(BF16)
    ic = jnp.arange(FOURIER_GROUP, dtype=jnp.int32)
    angc = ((ic[:, None] * ic[None, :]) % FOURIER_GROUP).astype(F32) * (2.0 * math.pi / FOURIER_GROUP)
    dftc = jnp.concatenate([jnp.cos(angc), -jnp.sin(angc)], axis=1).astype(BF16)
    return c1, s1, mc, ms, dftc


def _attn_kernel(sink_ref, q_ref, kp_ref, ko_ref, kn_ref, vp_ref, vo_ref, vn_ref, o_ref, kc_ref, vc_ref):
    i = pl.program_id(1)
    last = pl.num_programs(1) - 1
    tq = q_ref.shape[0]
    nb = tq // ATTN_BLOCK
    kc_ref[0:ATTN_BLOCK] = kp_ref[...]
    kc_ref[ATTN_BLOCK:ATTN_BLOCK + tq] = ko_ref[...]
    kc_ref[ATTN_BLOCK + tq:] = kn_ref[...]
    vc_ref[0:ATTN_BLOCK] = vp_ref[...]
    vc_ref[ATTN_BLOCK:ATTN_BLOCK + tq] = vo_ref[...]
    vc_ref[ATTN_BLOCK + tq:] = vn_ref[...]

    w3 = 3 * ATTN_BLOCK
    qpos = lax.broadcasted_iota(jnp.int32, (ATTN_BLOCK, w3), 0) + ATTN_BLOCK
    kpos = lax.broadcasted_iota(jnp.int32, (ATTN_BLOCK, w3), 1)
    band = jnp.abs(kpos - qpos) <= ATTN_BLOCK
    lane = lax.broadcasted_iota(jnp.int32, (w3, LANES), 1)
    lo_half = lane < HEAD_DIM
    out_lane = lax.broadcasted_iota(jnp.int32, (ATTN_BLOCK, LANES), 1) < HEAD_DIM

    for jb in range(nb):
        ok = band
        if jb == 0:
            ok = ok & ((kpos >= ATTN_BLOCK) | (i > 0))
        if jb == nb - 1:
            ok = ok & ((kpos < 2 * ATTN_BLOCK) | (i < last))
        bias = jnp.where(ok, 0.0, NEG_BIG).astype(F32)
        qb = q_ref[jb * ATTN_BLOCK:(jb + 1) * ATTN_BLOCK, :]
        kw = kc_ref[jb * ATTN_BLOCK:jb * ATTN_BLOCK + w3, :]
        vw = vc_ref[jb * ATTN_BLOCK:jb * ATTN_BLOCK + w3, :]
        vbds, scores = [], []
        for g in range(N_KV_HEADS):
            kg = kw[:, g * LANES:(g + 1) * LANES]
            vg = vw[:, g * LANES:(g + 1) * LANES]
            zero = jnp.zeros_like(kg)
            kbd = jnp.concatenate([jnp.where(lo_half, kg, zero), jnp.where(lo_half, zero, kg)], axis=0)
            vbds.append(jnp.concatenate([jnp.where(lo_half, vg, zero), jnp.where(lo_half, zero, vg)], axis=0))
            for pr in range(2):
                col = (2 * g + pr) * LANES
                scores.append(_dot_nt(qb[:, col:col + LANES], kbd))
        probs, scales = [], []
        for pair, s in enumerate(scores):
            es, rs = [], []
            for half in range(2):
                sk = sink_ref[2 * pair + half]
                sh = s[:, half * w3:(half + 1) * w3] + bias
                m = jnp.maximum(jnp.max(sh, axis=-1, keepdims=True), sk)
                e = jnp.exp(sh - m)
                den = jnp.sum(e, axis=-1, keepdims=True) + jnp.exp(sk - m)
                es.append(e.astype(BF16))
                rs.append(1.0 / den)
            probs.append(jnp.concatenate(es, axis=1))
            scales.append(jnp.where(out_lane, rs[0], rs[1]))
        for pair in range(2 * N_KV_HEADS):
            o = _dot(probs[pair], vbds[pair // 2]) * scales[pair]
            o_ref[jb * ATTN_BLOCK:(jb + 1) * ATTN_BLOCK, pair * LANES:(pair + 1) * LANES] = o.astype(BF16)


def _attention(q, k, v, sink, bsz, seq, tq=512):
    t = q.shape[0]
    nq = seq // tq
    r = tq // ATTN_BLOCK
    nblk = t // ATTN_BLOCK
    own = lambda b, i, s: (b * nq + i, 0)
    prev = lambda b, i, s: (jnp.maximum((b * nq + i) * r - 1, 0), 0)
    nxt = lambda b, i, s: (jnp.minimum((b * nq + i + 1) * r, nblk - 1), 0)
    kv = lambda m: pl.BlockSpec((ATTN_BLOCK if m is not own else tq, 256), m)
    return pl.pallas_call(
        _attn_kernel,
        grid_spec=pltpu.PrefetchScalarGridSpec(
            num_scalar_prefetch=1,
            grid=(bsz, nq),
            in_specs=[pl.BlockSpec((tq, 512), own), kv(prev), kv(own), kv(nxt), kv(prev), kv(own), kv(nxt)],
            out_specs=pl.BlockSpec((tq, 512), own),
            scratch_shapes=[pltpu.VMEM((tq + 2 * ATTN_BLOCK, 256), BF16),
                            pltpu.VMEM((tq + 2 * ATTN_BLOCK, 256), BF16)]),
        out_shape=jax.ShapeDtypeStruct((t, 512), BF16),
        compiler_params=_params(("parallel", "parallel")),
        name="window_attention",
    )(sink, q, k, k, k, v, v, v)


def _even_out_kernel(x_ref, f_ref, o_ref, w_ref, out_ref):
    out_ref[...] = (x_ref[...] + _dot(f_ref[...], w_ref[:512, :]) + _dot(o_ref[...], w_ref[512:, :]))


def _even_out(x2, f, o, w, tm=512):
    t = x2.shape[0]
    row = lambda i: (i, 0)
    return pl.pallas_call(
        _even_out_kernel,
        grid=(t // tm,),
        in_specs=[pl.BlockSpec((tm, D_MODEL), row), pl.BlockSpec((tm, 512), row),
                  pl.BlockSpec((tm, 512), row), _full((1024, D_MODEL))],
        out_specs=pl.BlockSpec((tm, D_MODEL), row),
        out_shape=jax.ShapeDtypeStruct((t, D_MODEL), F32),
        compiler_params=_params(("parallel",)),
        name="even_out",
    )(x2, f, o, w)


def _even_mixer(x2, bsz, seq, ln, w_in, q_gain, k_gain, sink, w_out):
    c1, s1, mc, ms, dftc = _dft_tables(seq)
    n1 = seq // FFT_N2
    kcols = w_in[:, 1024:1152]
    vcols = w_in[:, 1152:1280]
    dup = lambda c: jnp.concatenate([c[:, :64], c[:, :64], c[:, 64:], c[:, 64:]], axis=1)
    w = jnp.concatenate([w_in[:, :1024], dup(kcols), dup(vcols)], axis=1).astype(BF16)
    hid = jnp.arange(512, dtype=jnp.int32) // HEAD_DIM
    bd = jnp.where(hid[:, None] == hid[None, :], 1.0 / HEAD_DIM, 0.0).astype(BF16)
    inv = 1.0 / (ROPE_THETA ** (jnp.arange(0, HEAD_DIM, 2, dtype=F32) / HEAD_DIM))
    ang = jnp.arange(seq, dtype=F32)[:, None] * inv[None, :]
    cos, sin = jnp.cos(ang), jnp.sin(ang)
    cos_t = jnp.concatenate([cos, cos, cos, cos], axis=1)
    sin_t = jnp.concatenate([-sin, sin, -sin, sin], axis=1)
    qg = jnp.tile(q_gain.astype(F32), N_Q_HEADS)[None, :]
    kg = jnp.tile(k_gain.astype(F32), 2 * N_KV_HEADS)[None, :]
    g, q, k, v = _even_in(x2, seq, ln[None, :], w, dftc, bd, qg, kg, cos_t, sin_t)
    y = _fft1(g.reshape(bsz, n1, FFT_N2 * 1024), c1, s1)
    scale = 1.0 / math.sqrt(seq * FOURIER_GROUP)
    f = _fft2(y.reshape(bsz, n1, FFT_N2, 1024), mc, ms, scale).reshape(bsz * seq, 512)
    o = _attention(q, k, v, sink.astype(F32), bsz, seq)
    return _even_out(x2, f, o, w_out.astype(BF16))


def _odd_in_kernel(x_ref, g_ref, w_ref, z_ref, xbc_ref, dt_ref):
    h = _rms_rows(x_ref[...], g_ref[...]).astype(BF16)
    z_ref[...] = _dot(h, w_ref[:, :D_INNER]).astype(BF16)
    xbc_ref[...] = _dot(h, w_ref[:, D_INNER:D_INNER + CONV_DIM]).astype(BF16)
    dt_ref[...] = _dot(h, w_ref[:, D_INNER + CONV_DIM:])


def _odd_in(x2, ln, w, tm=256):
    t = x2.shape[0]
    row = lambda i: (i, 0)
    wcols = D_INNER + CONV_DIM + LANES
    return pl.pallas_call(
        _odd_in_kernel,
        grid=(t // tm,),
        in_specs=[pl.BlockSpec((tm, D_MODEL), row), _full((1, D_MODEL)), _full((D_MODEL, wcols))],
        out_specs=[pl.BlockSpec((tm, D_INNER), row), pl.BlockSpec((tm, CONV_DIM), row),
                   pl.BlockSpec((tm, LANES), row)],
        out_shape=[jax.ShapeDtypeStruct((t, D_INNER), BF16), jax.ShapeDtypeStruct((t, CONV_DIM), BF16),
                   jax.ShapeDtypeStruct((t, LANES), F32)],
        compiler_params=_params(("parallel",)),
        name="odd_in",
    )(x2, ln, w)


CONV_HALO = 16


def _conv_kernel(prev_ref, main_ref, next_ref, w_ref, b_ref, o_ref, win_ref, *, tiles_per_seq):
    i = pl.program_id(0) % tiles_per_seq
    tc = main_ref.shape[0]
    keep_prev = (i > 0).astype(F32)
    keep_next = (i < tiles_per_seq - 1).astype(F32)
    win_ref[0:CONV_HALO, :] = prev_ref[...].astype(F32) * keep_prev
    win_ref[CONV_HALO:CONV_HALO + tc, :] = main_ref[...].astype(F32)
    win_ref[CONV_HALO + tc:, :] = next_ref[...].astype(F32) * keep_next
    cb = 512
    for c in range(CONV_DIM // cb):
        cols = slice(c * cb, (c + 1) * cb)
        win = win_ref[:, cols]
        acc = jnp.broadcast_to(b_ref[:, cols], (tc, cb))
        for k in range(CONV_K):
            shift = CONV_K // 2 - k
            rolled = win if shift == 0 else pltpu.roll(win, shift % win.shape[0], 0)
            acc = acc + rolled[CONV_HALO:CONV_HALO + tc, :] * w_ref[k:k + 1, cols]
        o_ref[:, cols] = _silu(acc).astype(BF16)


def _conv(xbc, seq, w, b, tc=512):
    t = xbc.shape[0]
    r = tc // CONV_HALO
    nh = t // CONV_HALO
    return pl.pallas_call(
        functools.partial(_conv_kernel, tiles_per_seq=seq // tc),
        grid=(t // tc,),
        in_specs=[pl.BlockSpec((CONV_HALO, CONV_DIM), lambda i: (jnp.maximum(i * r - 1, 0), 0)),
                  pl.BlockSpec((tc, CONV_DIM), lambda i: (i, 0)),
                  pl.BlockSpec((CONV_HALO, CONV_DIM), lambda i: (jnp.minimum((i + 1) * r, nh - 1), 0)),
                  _full((CONV_K, CONV_DIM)), _full((1, CONV_DIM))],
        out_specs=pl.BlockSpec((tc, CONV_DIM), lambda i: (i, 0)),
        out_shape=jax.ShapeDtypeStruct((t, CONV_DIM), BF16),
        scratch_shapes=[pltpu.VMEM((tc + 2 * CONV_HALO, CONV_DIM), F32)],
        compiler_params=_params(("parallel",)),
        name="ssd_conv",
    )(xbc, xbc, xbc, w, b)


N_PAIRS = SSM_HEADS // 2
PAIRS_PER_GROUP = HEADS_PER_GROUP // 2


LOG2E = math.log2(math.e)
SSD_STEP = 2 * CHUNK


def _ssd_chunk(xbc_ref, dt_ref, bias, a_row, expand, y_ref, state_ref, r0, reverse, mask, tri_col, lo_half):
    off = SSM_HEADS if reverse else 0
    rows = slice(r0, r0 + CHUNK)
    dt_lh = dt_ref[rows, :] + bias
    dt_lh = jnp.maximum(dt_lh, 0.0) + jnp.log(1.0 + jnp.exp(-jnp.abs(dt_lh)))
    acs_lh = _dot3_left(tri_col, dt_lh * a_row)
    edge = 0 if reverse else CHUNK - 1
    acs2_lh = acs_lh * LOG2E
    tot2_row = acs2_lh[edge:edge + 1, :]
    r_hl = acs2_lh.T
    dt_hi = dt_lh.astype(BF16)
    dt_mid = (dt_lh - dt_hi.astype(F32)).astype(BF16)
    dt_exp = _dot(dt_hi, expand) + _dot(dt_mid, expand)
    lo_row = lo_half[0:1, :]

    for g in range(SSM_GROUPS):
        b_g = xbc_ref[rows, D_INNER + g * SSM_STATE:D_INNER + (g + 1) * SSM_STATE]
        c_g = xbc_ref[rows, D_INNER + GN + g * SSM_STATE:D_INNER + GN + (g + 1) * SSM_STATE]
        cb_b = _dot_nt(c_g, b_g).astype(BF16)
        bt = b_g.astype(F32).T.astype(BF16)
        gcols = slice(g * PAIRS_PER_GROUP * LANES, (g + 1) * PAIRS_PER_GROUP * LANES)
        y_off = _dot(c_g, state_ref[:, gcols].astype(BF16))
        for pp in range(PAIRS_PER_GROUP):
            p = g * PAIRS_PER_GROUP + pp
            hd0, hd1 = off + 2 * p, off + 2 * p + 1
            colb = [jnp.broadcast_to(acs2_lh[:, hd:hd + 1], (CHUNK, CHUNK)) for hd in (hd0, hd1)]
            ms = [cb_b * jnp.exp2(jnp.where(mask, cb_l - r_hl[hd:hd + 1, :], NEG_BIG)).astype(BF16)
                  for cb_l, hd in zip(colb, (hd0, hd1))]
            pcols = slice(p * LANES, (p + 1) * LANES)
            xdt = xbc_ref[rows, pcols].astype(F32) * dt_exp[:, pcols]
            xdt_b = xdt.astype(BF16)
            zx = jnp.zeros_like(xdt_b)
            xbd = jnp.concatenate([jnp.where(lo_half, xdt_b, zx), jnp.where(lo_half, zx, xdt_b)], axis=0)
            colsel = jnp.where(lo_half, colb[0], colb[1])
            y = _dot(jnp.concatenate(ms, axis=1), xbd)
            y = y + jnp.exp2(colsel) * y_off[:, pp * LANES:(pp + 1) * LANES]
            y_ref[rows, pcols] = y.astype(BF16)
            tot2 = jnp.where(lo_row, tot2_row[:, hd0:hd0 + 1], tot2_row[:, hd1:hd1 + 1])
            snew = _dot(bt, (xdt * jnp.exp2(tot2 - colsel)).astype(BF16))
            state_ref[:, pcols] = state_ref[:, pcols] * jnp.exp2(tot2) + snew


def _ssd_kernel(xbc_ref, dt_ref, bias_ref, alog_ref, exp_ref, y_ref, state_ref, *, reverse):
    @pl.when(pl.program_id(1) == 0)
    def _():
        state_ref[...] = jnp.zeros_like(state_ref)

    r_io = lax.broadcasted_iota(jnp.int32, (CHUNK, CHUNK), 0)
    c_io = lax.broadcasted_iota(jnp.int32, (CHUNK, CHUNK), 1)
    mask = (r_io <= c_io) if reverse else (r_io >= c_io)
    tri_col = jnp.where(mask, 1.0, 0.0).astype(BF16)
    lo_half = lax.broadcasted_iota(jnp.int32, (CHUNK, LANES), 1) < SSM_HEAD_DIM
    bias = bias_ref[...]
    a_row = -jnp.exp(alog_ref[...])
    n_sub = SSD_STEP // CHUNK
    order = range(n_sub - 1, -1, -1) if reverse else range(n_sub)
    for sub in order:
        _ssd_chunk(xbc_ref, dt_ref, bias, a_row, exp_ref[...], y_ref, state_ref, sub * CHUNK, reverse,
                   mask, tri_col, lo_half)


def _ssd(xbc_c, dt_raw, bias_row, alog_row, expand, bsz, seq, reverse):
    t = xbc_c.shape[0]
    ns = seq // SSD_STEP
    if reverse:
        blk = lambda b, c: (b * ns + ns - 1 - c, 0)
    else:
        blk = lambda b, c: (b * ns + c, 0)
    return pl.pallas_call(
        functools.partial(_ssd_kernel, reverse=reverse),
        grid=(bsz, ns),
        in_specs=[pl.BlockSpec((SSD_STEP, CONV_DIM), blk), pl.BlockSpec((SSD_STEP, LANES), blk),
                  _full((1, LANES)), _full((1, LANES)), _full((LANES, D_INNER))],
        out_specs=pl.BlockSpec((SSD_STEP, D_INNER), blk),
        out_shape=jax.ShapeDtypeStruct((t, D_INNER), BF16),
        scratch_shapes=[pltpu.VMEM((SSM_STATE, D_INNER), F32)],
        compiler_params=_params(("parallel", "arbitrary")),
        name="ssd_scan_bwd" if reverse else "ssd_scan_fwd",
    )(xbc_c, dt_raw, bias_row, alog_row, expand)


ODD_OUT_SUB = 512


def _odd_out_kernel(x_ref, yf_ref, yb_ref, xs_ref, z_ref, d_ref, g_ref, w_ref, out_ref):
    gw = D_INNER // SSM_GROUPS
    sub = ODD_OUT_SUB
    for j in range(x_ref.shape[0] // sub):
        rows = slice(j * sub, (j + 1) * sub)
        y = yf_ref[rows, :].astype(F32) + yb_ref[rows, :].astype(F32) + d_ref[...] * xs_ref[rows, :].astype(F32)
        y = y * _silu(z_ref[rows, :].astype(F32))
        parts = []
        for g in range(SSM_GROUPS):
            yg = y[:, g * gw:(g + 1) * gw]
            parts.append(yg * lax.rsqrt(jnp.mean(yg * yg, axis=-1, keepdims=True) + RMS_EPS))
        yn = (jnp.concatenate(parts, axis=1) * g_ref[...]).astype(BF16)
        out_ref[rows, :] = x_ref[rows, :] + _dot(yn, w_ref[...])


def _odd_out(x2, yf, yb, xbc_c, z, dskip, gain, w, tm=512):
    t = x2.shape[0]
    row = lambda i: (i, 0)
    return pl.pallas_call(
        _odd_out_kernel,
        grid=(t // tm,),
        in_specs=[pl.BlockSpec((tm, D_MODEL), row), pl.BlockSpec((tm, D_INNER), row),
                  pl.BlockSpec((tm, D_INNER), row), pl.BlockSpec((tm, D_INNER), row),
                  pl.BlockSpec((tm, D_INNER), row), _full((1, D_INNER)), _full((1, D_INNER)),
                  _full((D_INNER, D_MODEL))],
        out_specs=pl.BlockSpec((tm, D_MODEL), row),
        out_shape=jax.ShapeDtypeStruct((t, D_MODEL), F32),
        compiler_params=_params(("parallel",)),
        name="odd_out",
    )(x2, yf, yb, xbc_c, z, dskip, gain, w)


def _odd_mixer(x2, bsz, seq, ln, w_in, conv_w, conv_b, dt_bias, a_log, d_skip, norm_g, w_out):
    pad = jnp.zeros((D_MODEL, LANES - 2 * SSM_HEADS), w_in.dtype)
    w = jnp.concatenate([w_in, pad], axis=1).astype(BF16)
    z, xbc, dt_raw = _odd_in(x2, ln[None, :], w)
    xbc_c = _conv(xbc, seq, conv_w.astype(F32), conv_b.astype(F32)[None, :])
    lane_pad = jnp.zeros((LANES - 2 * SSM_HEADS,), F32)
    bias_row = jnp.concatenate([dt_bias.astype(F32).reshape(-1), lane_pad])[None, :]
    alog_row = jnp.concatenate([a_log.astype(F32).reshape(-1), lane_pad])[None, :]
    ch = jnp.arange(D_INNER, dtype=jnp.int32) // SSM_HEAD_DIM
    hd = jnp.arange(LANES, dtype=jnp.int32)
    ys = []
    for reverse in (False, True):
        off = SSM_HEADS if reverse else 0
        expand = jnp.where(hd[:, None] == off + ch[None, :], 1.0, 0.0).astype(BF16)
        ys.append(_ssd(xbc_c, dt_raw, bias_row, alog_row, expand, bsz, seq, reverse))
    dskip = jnp.repeat(d_skip.astype(F32), SSM_HEAD_DIM)[None, :]
    return _odd_out(x2, ys[0], ys[1], xbc_c, z, dskip, norm_g.astype(F32)[None, :], w_out.astype(BF16))


ROUTE_LANES = N_EXPERT_GROUPS + N_EXPERTS


def _router_kernel(x_ref, g_ref, w_ref, b_ref, ls_ref, meta_ref, cnt_ref, run_ref):
    i = pl.program_id(0)

    @pl.when(i == 0)
    def _():
        run_ref[...] = jnp.zeros_like(run_ref)

    h = _rms_rows(x_ref[...], g_ref[...])
    hh = h.astype(BF16)
    hl = (h - hh.astype(F32)).astype(BF16)
    w_hi = w_ref[0]
    w_lo = w_ref[1]
    logits = _dot(hh, w_hi) + _dot(hl, w_hi) + _dot(hh, w_lo) + b_ref[...]
    tm = logits.shape[0]
    lane = lax.broadcasted_iota(jnp.int32, (tm, LANES), 1).astype(F32)

    def top1(v):
        m = jnp.max(v, axis=-1, keepdims=True)
        idx = jnp.min(jnp.where(v == m, lane, float(LANES)), axis=-1, keepdims=True)
        return m, idx

    gl = jnp.where(lane < N_EXPERT_GROUPS, logits, NEG_BIG)
    gmax, gidx = top1(gl)
    g_p = 1.0 / jnp.sum(jnp.exp(gl - gmax), axis=-1, keepdims=True)
    lo = N_EXPERT_GROUPS + EXPERTS_PER_GROUP * gidx
    e1 = jnp.where((lane >= lo) & (lane < lo + EXPERTS_PER_GROUP), logits, NEG_BIG)
    m1, i1 = top1(e1)
    e2 = jnp.where(lane == i1, NEG_BIG, e1)
    m2, i2 = top1(e2)
    r = jnp.exp(m2 - m1)
    w_a = g_p / (1.0 + r)
    w_b = g_p * r / (1.0 + r)
    e_a = i1 - N_EXPERT_GROUPS
    e_b = i2 - N_EXPERT_GROUPS

    onehot = jnp.where((lane == e_a) | (lane == e_b), 1.0, 0.0)
    before = _dot(ls_ref[...], onehot.astype(BF16)) + run_ref[...]
    rank_a = jnp.sum(jnp.where(lane == e_a, before, 0.0), axis=-1, keepdims=True)
    rank_b = jnp.sum(jnp.where(lane == e_b, before, 0.0), axis=-1, keepdims=True)
    run_ref[...] = run_ref[...] + jnp.sum(onehot, axis=0, keepdims=True)
    cnt_ref[...] = jnp.broadcast_to(run_ref[...], cnt_ref.shape)

    meta = jnp.where(lane == 0, e_a, 0.0)
    meta = jnp.where(lane == 1, e_b, meta)
    meta = jnp.where(lane == 2, rank_a, meta)
    meta = jnp.where(lane == 3, rank_b, meta)
    meta = jnp.where(lane == 4, w_a, meta)
    meta = jnp.where(lane == 5, w_b, meta)
    meta_ref[...] = meta


def _router(x2, ln, w2, b, lstrict, tm):
    t = x2.shape[0]
    row = lambda i: (i, 0)
    return pl.pallas_call(
        _router_kernel,
        grid=(t // tm,),
        in_specs=[pl.BlockSpec((tm, D_MODEL), row), _full((1, D_MODEL)), _full((2, D_MODEL, LANES)),
                  _full((1, LANES)), _full((tm, tm))],
        out_specs=[pl.BlockSpec((tm, LANES), row), _full((8, LANES))],
        out_shape=[jax.ShapeDtypeStruct((t, LANES), F32), jax.ShapeDtypeStruct((8, LANES), F32)],
        scratch_shapes=[pltpu.VMEM((1, LANES), F32)],
        compiler_params=_params(("arbitrary",)),
        name="moe_router",
    )(x2, ln, w2, b, lstrict)


SUBLANES = 8
PACKED = D_MODEL // 2
DMA_THREADS = 2
U32 = jnp.uint32


def _pack_rows(v):
    bits = lax.bitcast_convert_type(v.astype(BF16).astype(F32), U32)
    return (bits[:, :PACKED] >> 16) | bits[:, PACKED:]


def _unpack_rows(w):
    lo = lax.bitcast_convert_type(w << 16, F32)
    hi = lax.bitcast_convert_type(w & jnp.uint32(0xFFFF0000), F32)
    return jnp.concatenate([lo, hi], axis=1)


def _dispatch_kernel(pad_end_ref, nused_ref, dest_ref, x_ref, g_ref, xin_hbm, h_ref, zero_ref, sem, zsem,
                     *, tm, nb):
    def zero_block(first_row):
        return pltpu.make_async_copy(zero_ref, xin_hbm.at[pl.ds(pl.multiple_of(first_row, MOE_BLOCK), MOE_BLOCK), :],
                                     zsem)

    def seg_zero(action):
        for e in range(N_EXPERTS):
            end = pad_end_ref[e]
            prev = pad_end_ref[e - 1] if e else 0

            @pl.when(end > prev)
            def _():
                action(zero_block(end - MOE_BLOCK))

    def tail_zero(action):
        def body(b, carry):
            action(zero_block(b * MOE_BLOCK))
            return carry

        lax.fori_loop(nused_ref[0], nb, body, 0)

    @pl.when(pl.program_id(0) == 0)
    def _():
        zero_ref[...] = jnp.zeros_like(zero_ref)
        seg_zero(lambda c: c.start())
        tail_zero(lambda c: c.start())
        seg_zero(lambda c: c.wait())
        tail_zero(lambda c: c.wait())

    def row_copy(grp, c, dst_row):
        return pltpu.make_async_copy(h_ref.at[grp, pl.ds(c, 1), :], xin_hbm.at[pl.ds(dst_row, 1), :], sem)

    def issue(grp, carry):
        for c in range(SUBLANES):
            for k in range(2):
                row_copy(grp, c, dest_ref[0, 0, 2 * SUBLANES * grp + 2 * c + k]).start(
                    priority=(2 * c + k) % DMA_THREADS)
        return carry

    h_ref[...] = _pack_rows(_rms_rows(x_ref[...], g_ref[...])).reshape(h_ref.shape)
    lax.fori_loop(0, tm // SUBLANES, issue, 0)

    def drain(grp, carry):
        for _ in range(2 * SUBLANES):
            row_copy(0, 0, 0).wait()
        return carry

    lax.fori_loop(0, tm // SUBLANES, drain, 0)


def _dispatch(pad_end, nused, dest, x2, ln, nb, tm):
    t = x2.shape[0]
    return pl.pallas_call(
        functools.partial(_dispatch_kernel, tm=tm, nb=nb),
        grid_spec=pltpu.PrefetchScalarGridSpec(
            num_scalar_prefetch=2,
            grid=(t // tm,),
            in_specs=[pl.BlockSpec((1, 1, 2 * tm), lambda i, pe, nu: (i, 0, 0), memory_space=pltpu.SMEM),
                      pl.BlockSpec((tm, D_MODEL), lambda i, pe, nu: (i, 0)),
                      pl.BlockSpec((1, D_MODEL), lambda i, pe, nu: (0, 0))],
            out_specs=pl.BlockSpec(memory_space=pl.ANY),
            scratch_shapes=[pltpu.VMEM((tm // SUBLANES, SUBLANES, PACKED), U32),
                            pltpu.VMEM((MOE_BLOCK, PACKED), U32),
                            pltpu.SemaphoreType.DMA, pltpu.SemaphoreType.DMA]),
        out_shape=jax.ShapeDtypeStruct((nb * MOE_BLOCK, PACKED), U32),
        compiler_params=_params(("arbitrary",)),
        name="moe_dispatch",
    )(pad_end, nused, dest, x2, ln)


EXPERT_BLOCKS_PER_STEP = 2


def _experts_kernel(be_ref, nused_ref, x_ref, w1a_ref, w3a_ref, w2a_ref, w1b_ref, w3b_ref, w2b_ref, o_ref):
    i = pl.program_id(0)
    first = EXPERT_BLOCKS_PER_STEP * i

    def swiglu(rows, w1_ref, w3_ref, w2_ref):
        x = _unpack_rows(x_ref[rows, :]).astype(BF16)
        a = _silu(_dot(x, w1_ref[0])) * _dot(x, w3_ref[0])
        o_ref[rows, :] = _pack_rows(_dot(a.astype(BF16), w2_ref[0]))

    fused = (first + 1 < nused_ref[0]) & (be_ref[first] == be_ref[first + 1])

    @pl.when(fused)
    def _():
        swiglu(slice(0, EXPERT_BLOCKS_PER_STEP * MOE_BLOCK), w1a_ref, w3a_ref, w2a_ref)

    for half, wrefs in enumerate(((w1a_ref, w3a_ref, w2a_ref), (w1b_ref, w3b_ref, w2b_ref))):
        rows = slice(half * MOE_BLOCK, (half + 1) * MOE_BLOCK)
        live = first + half < nused_ref[0]

        @pl.when(jnp.logical_not(fused) & live)
        def _():
            swiglu(rows, *wrefs)

        @pl.when(jnp.logical_not(live))
        def _():
            o_ref[rows, :] = jnp.zeros((MOE_BLOCK, PACKED), U32)


def _experts(block_e, nused, xin, w1, w3, w2):
    n = EXPERT_BLOCKS_PER_STEP
    nb = xin.shape[0] // MOE_BLOCK
    assert nb % n == 0
    row = lambda i, be, nu: (i, 0)
    specs = []
    for half in range(n):
        wsel = functools.partial(lambda i, be, nu, half: (be[n * i + half], 0, 0), half=half)
        specs += [pl.BlockSpec((1, D_MODEL, D_EXPERT), wsel), pl.BlockSpec((1, D_MODEL, D_EXPERT), wsel),
                  pl.BlockSpec((1, D_EXPERT, D_MODEL), wsel)]
    return pl.pallas_call(
        _experts_kernel,
        grid_spec=pltpu.PrefetchScalarGridSpec(
            num_scalar_prefetch=2,
            grid=(nb // n,),
            in_specs=[pl.BlockSpec((n * MOE_BLOCK, PACKED), row)] + specs,
            out_specs=pl.BlockSpec((n * MOE_BLOCK, PACKED), row)),
        out_shape=jax.ShapeDtypeStruct(xin.shape, U32),
        compiler_params=_params(("parallel",)),
        name="moe_experts",
    )(block_e, nused, xin, *([w1, w3, w2] * n))


def _combine_ple_kernel(dest_ref, dest_next_ref, x_ref, meta_ref, p_ref, g_ref, wg_ref, wp_ref, y_hbm,
                        out_ref, ya_ref, yb_ref, sem, *, tm):
    i = pl.program_id(0)
    slot = i % 2

    def row_gather(src_row, buf, s, grp, c):
        return pltpu.make_async_copy(y_hbm.at[pl.ds(src_row, 1), :], buf.at[s, grp, pl.ds(c, 1), :], sem.at[s])

    def gather_tile(idx_ref, s):
        def issue(grp, carry):
            for c in range(SUBLANES):
                for k, buf in enumerate((ya_ref, yb_ref)):
                    row_gather(idx_ref[0, 0, 2 * SUBLANES * grp + 2 * c + k], buf, s, grp, c).start(
                        priority=(2 * c + k) % DMA_THREADS)
            return carry

        lax.fori_loop(0, tm // SUBLANES, issue, 0)

    @pl.when(i == 0)
    def _():
        gather_tile(dest_ref, 0)

    @pl.when(i + 1 < pl.num_programs(0))
    def _():
        gather_tile(dest_next_ref, 1 - slot)

    def drain(grp, carry):
        for _ in range(SUBLANES):
            for buf in (ya_ref, yb_ref):
                row_gather(0, buf, slot, 0, 0).wait()
        return carry

    lax.fori_loop(0, tm // SUBLANES, drain, 0)

    sub = COMBINE_SUB
    for j in range(tm // sub):
        rows = slice(j * sub, (j + 1) * sub)
        grps = slice(j * sub // SUBLANES, (j + 1) * sub // SUBLANES)
        meta = meta_ref[rows, :]
        ya = _unpack_rows(ya_ref[slot, grps].reshape(sub, PACKED))
        yb = _unpack_rows(yb_ref[slot, grps].reshape(sub, PACKED))
        x = x_ref[rows, :] + (meta[:, 4:5] * ya + meta[:, 5:6] * yb)
        hn = _rms_rows(x, g_ref[...]).astype(BF16)
        gate = 1.0 / (1.0 + jnp.exp(-_dot(hn, wg_ref[...])))
        out_ref[rows, :] = x + _dot(p_ref[rows, :].astype(BF16), wp_ref[...]) * gate


def _combine_ple(dest, x2, meta, p2, ln, wg, wp, yout, tm):
    t = x2.shape[0]
    nt = t // tm
    row = lambda i: (i, 0)
    full = lambda shape: pl.BlockSpec(shape, lambda i: (0,) * len(shape))
    return pl.pallas_call(
        functools.partial(_combine_ple_kernel, tm=tm),
        grid_spec=pltpu.PrefetchScalarGridSpec(
            num_scalar_prefetch=0,
            grid=(nt,),
            in_specs=[pl.BlockSpec((1, 1, 2 * tm), lambda i: (i, 0, 0), memory_space=pltpu.SMEM),
                      pl.BlockSpec((1, 1, 2 * tm), lambda i: (jnp.minimum(i + 1, nt - 1), 0, 0),
                                   memory_space=pltpu.SMEM),
                      pl.BlockSpec((tm, D_MODEL), row), pl.BlockSpec((tm, LANES), row),
                      pl.BlockSpec((tm, PLE_DIM), row), full((1, D_MODEL)), full((D_MODEL, D_MODEL)),
                      full((PLE_DIM, D_MODEL)), pl.BlockSpec(memory_space=pl.ANY)],
            out_specs=pl.BlockSpec((tm, D_MODEL), row),
            scratch_shapes=[pltpu.VMEM((2, tm // SUBLANES, SUBLANES, PACKED), U32),
                            pltpu.VMEM((2, tm // SUBLANES, SUBLANES, PACKED), U32),
                            pltpu.SemaphoreType.DMA((2,))]),
        out_shape=jax.ShapeDtypeStruct((t, D_MODEL), F32),
        compiler_params=_params(("arbitrary",)),
        name="moe_combine_ple",
    )(dest, dest, x2, meta, p2, ln, wg, wp, yout)


MOE_TM = 512
COMBINE_SUB = 256


def _moe_ple(x2, p2, ln_ffn, w_rg, b_rg, w_re, b_re, w1, w3, w2, ln_ple, w_gate, w_proj):
    t = x2.shape[0]
    tm = MOE_TM
    wr = jnp.concatenate([w_rg, w_re, jnp.zeros((D_MODEL, LANES - ROUTE_LANES), w_rg.dtype)], axis=1).astype(F32)
    wr_hi = wr.astype(BF16)
    wr_lo = (wr - wr_hi.astype(F32)).astype(BF16)
    br = jnp.concatenate([b_rg, b_re, jnp.zeros((LANES - ROUTE_LANES,), b_rg.dtype)]).astype(F32)[None, :]
    ii = jnp.arange(tm, dtype=jnp.int32)
    lstrict = jnp.where(ii[:, None] > ii[None, :], 1.0, 0.0).astype(BF16)
    meta, cnt = _router(x2, ln_ffn[None, :], jnp.stack([wr_hi, wr_lo]), br, lstrict, tm)

    counts = cnt[0, :N_EXPERTS].astype(jnp.int32)
    padded = (counts + MOE_BLOCK - 1) // MOE_BLOCK * MOE_BLOCK
    pad_end = jnp.cumsum(padded)
    start = (pad_end - padded).astype(jnp.int32)
    nb = (t * 2) // MOE_BLOCK + N_EXPERTS
    first_row = jnp.arange(nb, dtype=jnp.int32) * MOE_BLOCK
    block_e = jnp.minimum(jnp.sum((pad_end[None, :] <= first_row[:, None]).astype(jnp.int32), axis=1),
                          N_EXPERTS - 1).astype(jnp.int32)
    nused = (pad_end[-1:] // MOE_BLOCK).astype(jnp.int32)
    er = meta[:, :4].astype(jnp.int32)
    eid = jnp.arange(N_EXPERTS, dtype=jnp.int32)
    seg_start = jnp.sum(jnp.where(er[:, :2, None] == eid, start, 0), axis=-1)
    dest = (seg_start + er[:, 2:4]).reshape(t // tm, 1, 2 * tm)

    xin = _dispatch(pad_end.astype(jnp.int32), nused, dest, x2, ln_ffn[None, :], nb, tm)
    yout = _experts(block_e, nused, xin, w1.astype(BF16), w3.astype(BF16), w2.astype(BF16))
    return _combine_ple(dest, x2, meta, p2, ln_ple[None, :], w_gate.astype(BF16), w_proj.astype(BF16), yout, tm)


def _trunk(x, p, prm):
    bsz, seq, _ = x.shape
    x2 = x.reshape(bsz * seq, D_MODEL)
    for i in range(DEPTH):
        j = i // 2
        if i % 2 == 0:
            x2 = _even_mixer(x2, bsz, seq, prm["ln_mix_e"][j], prm["w_in_e"][j], prm["q_gain"][j],
                             prm["k_gain"][j], prm["sink"][j], prm["w_out_e"][j])
        else:
            x2 = _odd_mixer(x2, bsz, seq, prm["ln_mix_o"][j], prm["w_in_o"][j], prm["conv_w"][j],
                            prm["conv_b"][j], prm["dt_bias"][j], prm["a_log"][j], prm["d_skip"][j],
                            prm["ssm_gain"][j], prm["w_out_o"][j])
        x2 = _moe_ple(x2, p[i].reshape(bsz * seq, PLE_DIM), prm["ln_ffn"][i], prm["w_router_g"][i],
                      prm["b_router_g"][i], prm["w_router_e"][i], prm["b_router_e"][i], prm["w1"][i],
                      prm["w3"][i], prm["w2"][i], prm["ln_ple"][i], prm["w_ple_gate"][i], prm["w_ple_proj"][i])
    return x2.reshape(bsz, seq, D_MODEL)


def kernel(x_prompt, x_sample, p_prompt, p_sample, ln_mix_e, w_in_e, q_gain, k_gain, sink, w_out_e, ln_mix_o, w_in_o, conv_w, conv_b, dt_bias, a_log, d_skip, ssm_gain, w_out_o, ln_ffn, w_router_g, b_router_g, w_router_e, b_router_e, w1, w3, w2, ln_ple, w_ple_gate, w_ple_proj):
    prm = dict(ln_mix_e=ln_mix_e, w_in_e=w_in_e, q_gain=q_gain, k_gain=k_gain, sink=sink,
               w_out_e=w_out_e, ln_mix_o=ln_mix_o, w_in_o=w_in_o, conv_w=conv_w, conv_b=conv_b,
               dt_bias=dt_bias, a_log=a_log, d_skip=d_skip, ssm_gain=ssm_gain, w_out_o=w_out_o,
               ln_ffn=ln_ffn, w_router_g=w_router_g, b_router_g=b_router_g,
               w_router_e=w_router_e, b_router_e=b_router_e, w1=w1, w3=w3, w2=w2,
               ln_ple=ln_ple, w_ple_gate=w_ple_gate, w_ple_proj=w_ple_proj)
    return (_trunk(x_prompt, p_prompt, prm), _trunk(x_sample, p_sample, prm))
```
